```python
import jax
import jax.numpy as jnp
from jax import lax
import numpy as np

D_MODEL = 2048
BATCH = 2
SEQ = 8192
DEPTH = 1
DEC_BATCH = 32
DEC_SEQ = 8
PAST_LEN = 16384
PAGE_SIZE = 128

HEAD_DIM = 128
N_HEADS = (D_MODEL // 2) // HEAD_DIM
KV_HEADS = 2
GROUP = N_HEADS // KV_HEADS
CMP_LEN = 32
CMP_STRIDE = 16
SEL_BLOCK = 64
SEL_TOPN = 16
WINDOW = 512
Q_BLOCK = 128
ROPE_THETA = 10000.0
FORCE_SCORE = 1e4
NSA_KV_SLOTS = 4
POOL_WIDTH = D_MODEL // 2
POOL_WINDOWS = (2, 4, 8, 16)
POOL_GROUP = POOL_WIDTH // len(POOL_WINDOWS)
POOL_BUF = max(POOL_WINDOWS) - 1
PEER_HEADS = 8
PEER_NKEYS = 128
PEER_EXPERTS = PEER_NKEYS * PEER_NKEYS
PEER_DKEY = 256
PEER_TOPK = 16
PEER_TOKEN_BLOCK = 128
EPS = 1e-6

Q_W = N_HEADS * HEAD_DIM
KV_W = KV_HEADS * HEAD_DIM
IN_WIDTH = Q_W + 6 * KV_W + 3 * N_HEADS + POOL_WIDTH + 2 * D_MODEL

kernel_name = "nsa_pool_gated_peer_decoder_step"

F32 = jnp.float32


def rmsnorm(x, g):
    xf = x.astype(F32)
    y = xf * lax.rsqrt(jnp.mean(xf * xf, axis=-1, keepdims=True) + EPS)
    return (y * g.astype(F32)).astype(x.dtype)


def rope(x, pos):
    half = HEAD_DIM // 2
    inv = ROPE_THETA ** (-jnp.arange(half, dtype=F32) / half)
    ang = pos.astype(F32)[:, None] * inv[None, :]
    cos = jnp.cos(ang)[:, None, :]
    sin = jnp.sin(ang)[:, None, :]
    xf = x.astype(F32)
    x1, x2 = xf[..., :half], xf[..., half:]
    return jnp.concatenate([x1 * cos - x2 * sin, x2 * cos + x1 * sin], axis=-1).astype(x.dtype)


def masked_softmax(s, mask):
    s = jnp.where(mask, s, -jnp.inf)
    m = jnp.max(s, axis=-1, keepdims=True)
    m = jnp.where(jnp.isfinite(m), m, 0.0)
    e = jnp.where(mask, jnp.exp(s - m), 0.0)
    return e / jnp.maximum(jnp.sum(e, axis=-1, keepdims=True), 1e-30)


def mixer_inputs(x, pos, g_norm_mix, w_in):
    B, T, _ = x.shape
    h = rmsnorm(x, g_norm_mix)
    z = h @ w_in
    sizes = (Q_W, 6 * KV_W, 3 * N_HEADS, POOL_WIDTH, D_MODEL, D_MODEL)
    offs = [int(v) for v in np.cumsum(sizes)[:-1]]
    q, kv, ng, u, ga, gb = jnp.split(z, offs, axis=-1)
    q = q.reshape(B, T, N_HEADS, HEAD_DIM)
    kv = kv.reshape(B, T, 6, KV_HEADS, HEAD_DIM)
    nsa_kv = jnp.stack([kv[:, :, 0], kv[:, :, 1], rope(kv[:, :, 2], pos), kv[:, :, 3]], axis=2)
    win_kv = jnp.stack([rope(kv[:, :, 4], pos), kv[:, :, 5]], axis=2)
    gates = jax.nn.sigmoid(ng.reshape(B, T, 3, N_HEADS))
    return q, rope(q, pos), nsa_kv, win_kv, gates, u, ga, gb


def compress(k, w_phi):
    B, L, G, D = k.shape
    n_cmp = -(-L // CMP_STRIDE)
    n_sub = n_cmp + 1
    k = jnp.pad(k, ((0, 0), (0, n_sub * CMP_STRIDE - L), (0, 0), (0, 0)))
    sub = k.reshape(B, n_sub, CMP_STRIDE, G, D)
    first = jnp.einsum('bsigd,ide->bsge', sub[:, :n_cmp], w_phi[:CMP_STRIDE])
    second = jnp.einsum('bsigd,ide->bsge', sub[:, 1:], w_phi[CMP_STRIDE:])
    return first + second


def nsa_cmp_sel(q, q_rot, q_pos, kc, vc, ks, vs):
    B, L = ks.shape[:2]
    n_cmp = kc.shape[1]
    n_sel = -(-L // SEL_BLOCK)
    topn = min(SEL_TOPN, n_sel)
    scale = HEAD_DIM ** -0.5
    pad = ((0, 0), (0, n_sel * SEL_BLOCK - L), (0, 0), (0, 0))
    ks_blk = jnp.pad(ks, pad).reshape(B, n_sel, SEL_BLOCK, KV_HEADS, HEAD_DIM).transpose(0, 3, 1, 2, 4)
    vs_blk = jnp.pad(vs, pad).reshape(B, n_sel, SEL_BLOCK, KV_HEADS, HEAD_DIM).transpose(0, 3, 1, 2, 4)
    cmp_end = jnp.arange(n_cmp) * CMP_STRIDE + CMP_LEN - 1
    blk = jnp.arange(n_sel)
    bi = jnp.arange(B)[:, None, None, None]
    gi = jnp.arange(KV_HEADS)[None, None, :, None]

    def block(args):
        qc_b, qr_b, pos_b = args
        QB = pos_b.shape[0]
        qg = qc_b.reshape(B, QB, KV_HEADS, GROUP, HEAD_DIM)
        s = jnp.einsum('bqgrd,bngd->bqgrn', qg, kc).astype(F32) * scale
        cmask = (cmp_end[None, :] <= pos_b[:, None])[None, :, None, None, :]
        p = masked_softmax(s, cmask)
        o_cmp = jnp.einsum('bqgrn,bngd->bqgrd', p.astype(vc.dtype), vc)
        imp = jnp.sum(p, axis=3)
        imp = jnp.pad(imp, ((0, 0), (0, 0), (0, 0), (0, n_sel * 4 - n_cmp)))
        imp = imp.reshape(B, QB, KV_HEADS, n_sel, SEL_BLOCK // CMP_STRIDE).sum(-1)
        cur = pos_b // SEL_BLOCK
        forced = ((blk[None, :] == cur[:, None]) | (blk[None, :] == 0))[None, :, None, :]
        causal = (blk[None, :] <= cur[:, None])[None, :, None, :]
        imp = jnp.where(forced, FORCE_SCORE, jnp.where(causal, imp, -1.0))
        _, idx = lax.top_k(imp, topn)
        kg = ks_blk[bi, gi, idx]
        vg = vs_blk[bi, gi, idx]
        qrg = qr_b.reshape(B, QB, KV_HEADS, GROUP, HEAD_DIM)
        s2 = jnp.einsum('bqgrd,bqgkcd->bqgrkc', qrg, kg).astype(F32) * scale
        s2 = s2.reshape(B, QB, KV_HEADS, GROUP, topn * SEL_BLOCK)
        tok_pos = idx[..., None] * SEL_BLOCK + jnp.arange(SEL_BLOCK)
        smask = (tok_pos <= pos_b[None, :, None, None, None]).reshape(B, QB, KV_HEADS, 1, topn * SEL_BLOCK)
        p2 = masked_softmax(s2, smask)
        o_sel = jnp.einsum('bqgrn,bqgnd->bqgrd', p2.astype(vg.dtype),
                           vg.reshape(B, QB, KV_HEADS, topn * SEL_BLOCK, HEAD_DIM))
        return (o_cmp.reshape(B, QB, N_HEADS, HEAD_DIM), o_sel.reshape(B, QB, N_HEADS, HEAD_DIM))

    o_cmp, o_sel = lax.map(block, (jnp.moveaxis(q, 1, 0), jnp.moveaxis(q_rot, 1, 0), q_pos))
    nb, QB = q_pos.shape
    o_cmp = jnp.moveaxis(o_cmp, 0, 1).reshape(B, nb * QB, N_HEADS, HEAD_DIM)
    o_sel = jnp.moveaxis(o_sel, 0, 1).reshape(B, nb * QB, N_HEADS, HEAD_DIM)
    return o_cmp, o_sel


def window_band(win_kv):
    B, T = win_kv.shape[:2]
    nb = T // Q_BLOCK
    nw = WINDOW // Q_BLOCK
    padded = jnp.pad(win_kv, ((0, 0), (nw * Q_BLOCK, 0), (0, 0), (0, 0), (0, 0)))
    blocks = padded.reshape(B, nb + nw, Q_BLOCK, 2, KV_HEADS, HEAD_DIM)
    band = jnp.concatenate([blocks[:, j:j + nb] for j in range(nw + 1)], axis=2)
    k_pos = (jnp.arange(nb)[:, None] - nw) * Q_BLOCK + jnp.arange((nw + 1) * Q_BLOCK)[None, :]
    return band, k_pos


def window_attn(q, q_pos, kb, vb, k_pos):
    B, nb, QB = q.shape[:3]
    qg = q.reshape(B, nb, QB, KV_HEADS, GROUP, HEAD_DIM)
    s = jnp.einsum('bnqgrd,bnkgd->bnqgrk', qg, kb).astype(F32) * (HEAD_DIM ** -0.5)
    diff = q_pos[:, :, None] - k_pos[:, None, :]
    mask = (k_pos[:, None, :] >= 0) & (diff >= 0) & (diff < WINDOW)
    p = masked_softmax(s, mask[None, :, :, None, None, :])
    o = jnp.einsum('bnqgrk,bnkgd->bnqgrd', p.astype(vb.dtype), vb)
    return o.reshape(B, nb * QB, N_HEADS, HEAD_DIM)


def pool_mix(u_ext, n_new, w_pool_group, pool_scale):
    B, Le, C = u_ext.shape
    uf = u_ext.astype(F32)
    c = jnp.concatenate([jnp.zeros((B, 1, C), F32), jnp.cumsum(uf, axis=1)], axis=1)
    hi = jnp.arange(Le - n_new, Le) + 1
    outs = []
    for gidx, w in enumerate(POOL_WINDOWS):
        sl = slice(gidx * POOL_GROUP, (gidx + 1) * POOL_GROUP)
        lo = jnp.maximum(hi - w, 0)
        mean = (c[:, hi, sl] - c[:, lo, sl]) / (hi - lo).astype(F32)[None, :, None]
        d = mean - uf[:, Le - n_new:, sl]
        outs.append(jnp.einsum('btc,ce->bte', d.astype(u_ext.dtype), w_pool_group[gidx]))
    return jnp.concatenate(outs, axis=-1) * pool_scale


def peer(h, w_query, sub_keys, u_table, v_table):
    shape = h.shape
    xt = h.reshape(-1, D_MODEL)
    n_tok = xt.shape[0]
    n_blk = -(-n_tok // PEER_TOKEN_BLOCK)
    xt = jnp.pad(xt, ((0, n_blk * PEER_TOKEN_BLOCK - n_tok), (0, 0)))

    def block(xb):
        q = (xb @ w_query).reshape(PEER_TOKEN_BLOCK, PEER_HEADS, 2, PEER_DKEY // 2)
        s = jnp.einsum('thcd,hckd->thck', q, sub_keys).astype(F32)
        sv, si = lax.top_k(s, PEER_TOPK)
        comb = (sv[:, :, 0, :, None] + sv[:, :, 1, None, :]).reshape(
            PEER_TOKEN_BLOCK, PEER_HEADS, PEER_TOPK * PEER_TOPK)
        cv, ci = lax.top_k(comb, PEER_TOPK)
        i1 = jnp.take_along_axis(si[:, :, 0], ci // PEER_TOPK, axis=-1)
        i2 = jnp.take_along_axis(si[:, :, 1], ci % PEER_TOPK, axis=-1)
        e = i1 * PEER_NKEYS + i2
        g = jax.nn.softmax(cv, axis=-1)
        act = jax.nn.gelu(jnp.einsum('td,thkd->thk', xb, u_table[e]).astype(F32), approximate=False)
        return jnp.einsum('thk,thkd->td', (g * act).astype(xb.dtype), v_table[e])

    out = lax.map(block, xt.reshape(n_blk, PEER_TOKEN_BLOCK, D_MODEL))
    return out.reshape(-1, D_MODEL)[:n_tok].reshape(shape)


def finish_layer(x, o_cmp, o_sel, o_win, gates, pool_out, ga, gb, lp):
    B, T, _ = x.shape
    o = (gates[:, :, 0, :, None] * o_cmp + gates[:, :, 1, :, None] * o_sel
         + gates[:, :, 2, :, None] * o_win)
    a = o.reshape(B, T, Q_W) @ lp["w_branch_attn"]
    p = pool_out @ lp["w_branch_pool"]
    mix = jax.nn.sigmoid(ga) * a + jax.nn.sigmoid(gb) * p
    x = x + mix @ lp["w_out"]
    h = rmsnorm(x, lp["g_norm_ffn"])
    return x + peer(h, lp["peer_w_query"], lp["peer_sub_keys"], lp["peer_u"], lp["peer_v"])


def prompt_layer(x, lp):
    B, T, _ = x.shape
    pos = jnp.arange(T)
    q, q_rot, nsa_kv, win_kv, gates, u, ga, gb = mixer_inputs(x, pos, lp["g_norm_mix"], lp["w_in"])
    nb = T // Q_BLOCK
    q_pos = pos.reshape(nb, Q_BLOCK)
    kc = compress(nsa_kv[:, :, 0], lp["w_phi_k"])
    vc = compress(nsa_kv[:, :, 1], lp["w_phi_v"])
    qb = q.reshape(B, nb, Q_BLOCK, N_HEADS, HEAD_DIM)
    qrb = q_rot.reshape(B, nb, Q_BLOCK, N_HEADS, HEAD_DIM)
    o_cmp, o_sel = nsa_cmp_sel(qb, qrb, q_pos, kc, vc, nsa_kv[:, :, 2], nsa_kv[:, :, 3])
    band, k_pos = window_band(win_kv)
    o_win = window_attn(qrb, q_pos, band[:, :, :, 0], band[:, :, :, 1], k_pos)
    pool_out = pool_mix(u, T, lp["w_pool_group"], lp["pool_scale"])
    x = finish_layer(x, o_cmp, o_sel, o_win, gates, pool_out, ga, gb, lp)
    return x, nsa_kv, win_kv[:, -min(WINDOW, T):], u[:, -POOL_BUF:]


def sample_layer(x, cache_kv, state_win, state_pool, page_table, lp):
    B, T, _ = x.shape
    n_pages = PAST_LEN // PAGE_SIZE
    pos = PAST_LEN + jnp.arange(T)
    q, q_rot, nsa_kv, win_kv, gates, u, ga, gb = mixer_inputs(x, pos, lp["g_norm_mix"], lp["w_in"])
    past = cache_kv[page_table].reshape(B, n_pages * PAGE_SIZE, NSA_KV_SLOTS, KV_HEADS, HEAD_DIM)
    full = jnp.concatenate([past, nsa_kv], axis=1)
    kc = compress(full[:, :, 0], lp["w_phi_k"])
    vc = compress(full[:, :, 1], lp["w_phi_v"])
    q_pos = pos[None, :]
    o_cmp, o_sel = nsa_cmp_sel(q[:, None], q_rot[:, None], q_pos, kc, vc, full[:, :, 2], full[:, :, 3])
    win_buf = state_win.shape[1]
    win_ext = jnp.concatenate([state_win, win_kv], axis=1)
    k_pos = (PAST_LEN - win_buf + jnp.arange(win_buf + T))[None, :]
    o_win = window_attn(q_rot[:, None], q_pos, win_ext[:, None, :, 0], win_ext[:, None, :, 1], k_pos)
    u_ext = jnp.concatenate([state_pool, u], axis=1)
    pool_out = pool_mix(u_ext, T, lp["w_pool_group"], lp["pool_scale"])
    x = finish_layer(x, o_cmp, o_sel, o_win, gates, pool_out, ga, gb, lp)
    return x, nsa_kv, win_ext[:, -min(WINDOW, win_buf + T):], u_ext[:, -POOL_BUF:]


def setup_inputs(seed: int = 0) -> dict:
    key = jax.random.key(seed)
    ks = jax.random.split(key, 24)
    n_pages = PAST_LEN // PAGE_SIZE
    n_used = DEC_BATCH * n_pages
    n_phys = n_used + max(1, n_used // 4)
    win_buf = min(WINDOW, PAST_LEN)

    def nrm(k, shape, scale):
        return jax.random.normal(k, shape, F32) * scale

    x_prompt = nrm(ks[0], (BATCH, SEQ, D_MODEL), 1.0)
    x_sample = nrm(ks[1], (DEC_BATCH, DEC_SEQ, D_MODEL), 1.0)
    cache_kv_nsa = nrm(ks[2], (DEPTH, n_phys, PAGE_SIZE, NSA_KV_SLOTS, KV_HEADS, HEAD_DIM), 1.0)
    state_win_kv = nrm(ks[3], (DEPTH, DEC_BATCH, win_buf, 2, KV_HEADS, HEAD_DIM), 1.0)
    state_pool = nrm(ks[4], (DEPTH, DEC_BATCH, POOL_BUF, POOL_WIDTH), 1.0)
    page_table = jax.random.permutation(ks[5], n_phys)[:n_used].reshape(DEC_BATCH, n_pages).astype(jnp.int32)
    return {
        "x_prompt": x_prompt,
        "x_sample": x_sample,
        "cache_kv_nsa": cache_kv_nsa,
        "state_win_kv": state_win_kv,
        "state_pool": state_pool,
        "page_table": page_table,
        "g_norm_mix": 1.0 + nrm(ks[6], (DEPTH, D_MODEL), 0.02),
        "w_in": nrm(ks[7], (DEPTH, D_MODEL, IN_WIDTH), D_MODEL ** -0.5),
        "w_phi_k": nrm(ks[8], (DEPTH, CMP_LEN, HEAD_DIM, HEAD_DIM), (CMP_LEN * HEAD_DIM) ** -0.5),
        "w_phi_v": nrm(ks[9], (DEPTH, CMP_LEN, HEAD_DIM, HEAD_DIM), (CMP_LEN * HEAD_DIM) ** -0.5),
        "w_pool_group": nrm(ks[10], (DEPTH, len(POOL_WINDOWS), POOL_GROUP, POOL_GROUP), POOL_GROUP ** -0.5),
        "pool_scale": 1.0 + nrm(ks[11], (DEPTH, POOL_WIDTH), 0.02),
        "w_branch_attn": nrm(ks[12], (DEPTH, Q_W, D_MODEL), Q_W ** -0.5),
        "w_branch_pool": nrm(ks[13], (DEPTH, POOL_WIDTH, D_MODEL), POOL_WIDTH ** -0.5),
        "w_out": nrm(ks[14], (DEPTH, D_MODEL, D_MODEL), D_MODEL ** -0.5),
        "g_norm_ffn": 1.0 + nrm(ks[15], (DEPTH, D_MODEL), 0.02),
        "peer_w_query": nrm(ks[16], (DEPTH, D_MODEL, PEER_HEADS * PEER_DKEY), D_MODEL ** -0.5),
        "peer_sub_keys": nrm(ks[17], (DEPTH, PEER_HEADS, 2, PEER_NKEYS, PEER_DKEY // 2), (PEER_DKEY // 2) ** -0.5),
        "peer_u": nrm(ks[18], (DEPTH, PEER_EXPERTS, D_MODEL), D_MODEL ** -0.5),
        "peer_v": nrm(ks[19], (DEPTH, PEER_EXPERTS, D_MODEL), PEER_HEADS ** -0.5),
        "g_norm_final": 1.0 + nrm(ks[20], (D_MODEL,), 0.02),
    }


def reference(x_prompt, x_sample, cache_kv_nsa, state_win_kv, state_pool, page_table,
              g_norm_mix, w_in, w_phi_k, w_phi_v, w_pool_group, pool_scale,
              w_branch_attn, w_branch_pool, w_out, g_norm_ffn,
              peer_w_query, peer_sub_keys, peer_u, peer_v, g_norm_final):
    xp, xs = x_prompt, x_sample
    kv_p, kv_s, win_p, win_s, pool_p, pool_s = [], [], [], [], [], []
    for i in range(DEPTH):
        lp = {
            "g_norm_mix": g_norm_mix[i], "w_in": w_in[i], "w_phi_k": w_phi_k[i], "w_phi_v": w_phi_v[i],
            "w_pool_group": w_pool_group[i], "pool_scale": pool_scale[i],
            "w_branch_attn": w_branch_attn[i], "w_branch_pool": w_branch_pool[i], "w_out": w_out[i],
            "g_norm_ffn": g_norm_ffn[i], "peer_w_query": peer_w_query[i],
            "peer_sub_keys": peer_sub_keys[i], "peer_u": peer_u[i], "peer_v": peer_v[i],
        }
        xp, a, b, c = prompt_layer(xp, lp)
        kv_p.append(a)
        win_p.append(b)
        pool_p.append(c)
        xs, a, b, c = sample_layer(xs, cache_kv_nsa[i], state_win_kv[i], state_pool[i], page_table, lp)
        kv_s.append(a)
        win_s.append(b)
        pool_s.append(c)
    y_prompt = rmsnorm(xp, g_norm_final)
    y_sample = rmsnorm(xs, g_norm_final)
    return (y_prompt, y_sample, jnp.stack(kv_p), jnp.stack(kv_s), jnp.stack(win_p), jnp.stack(win_s),
            jnp.stack(pool_p), jnp.stack(pool_s))
```

```python
import functools

import jax
import jax.numpy as jnp
import numpy as np
from jax import lax
from jax.experimental import pallas as pl
from jax.experimental.pallas import tpu as pltpu

F32 = jnp.float32
BF16 = jnp.bfloat16

HEAD_DIM = 128
N_HEADS = 8
KV_HEADS = 2
GROUP = N_HEADS // KV_HEADS
CMP_LEN = 32
CMP_STRIDE = 16
SEL_BLOCK = 64
SEL_TOPN = 16
CMP_PER_SEL = SEL_BLOCK // CMP_STRIDE
WINDOW = 512
Q_BLOCK = 128
ROPE_THETA = 10000.0
FORCE_SCORE = 1e4
POOL_WINDOWS = (2, 4, 8, 16)
POOL_PREV = 16
PEER_HEADS = 8
PEER_NKEYS = 128
PEER_TOPK = 16
EPS = 1e-6
SCALE = HEAD_DIM ** -0.5
NEG_INF = float("-inf")

VMEM_LIMIT = 56 * 1024 * 1024


def _params(*sem):
    return pltpu.CompilerParams(dimension_semantics=sem, vmem_limit_bytes=VMEM_LIMIT)


def _dot(a, b):
    return jnp.dot(a, b, preferred_element_type=F32)


def _dot_nt(a, b):
    return lax.dot_general(a, b, (((1,), (1,)), ((), ())), preferred_element_type=F32)


def _rms(x, g):
    return x * lax.rsqrt(jnp.mean(x * x, axis=-1, keepdims=True) + EPS) * g


def _masked_softmax_parts(parts, masks):
    parts = [jnp.where(mk, s, NEG_INF) for s, mk in zip(parts, masks)]
    m = functools.reduce(jnp.maximum, [jnp.max(s, axis=-1, keepdims=True) for s in parts])
    m = jnp.where(m == NEG_INF, 0.0, m)
    es = [jnp.exp(s - m) for s in parts]
    den = functools.reduce(jnp.add, [jnp.sum(e, axis=-1, keepdims=True) for e in es])
    inv = 1.0 / jnp.maximum(den, 1e-30)
    return [e * inv for e in es]


def _topk_select(v, n, axis):
    size = v.shape[axis]
    idx = lax.broadcasted_iota(jnp.int32, v.shape, axis).astype(F32)
    sel = jnp.zeros(v.shape, F32)
    for _ in range(n):
        m = jnp.max(v, axis=axis, keepdims=True)
        first = jnp.min(jnp.where(v == m, idx, float(size)), axis=axis, keepdims=True)
        hit = idx == first
        v = jnp.where(hit, NEG_INF, v)
        sel = jnp.where(hit, 1.0, sel)
    return sel


def _topk_sorted(v, n, payload=None):
    size = v.shape[0]
    idx = lax.broadcasted_iota(jnp.int32, v.shape, 0).astype(F32)
    vals, picks = [], []
    for _ in range(n):
        m = jnp.max(v, axis=0, keepdims=True)
        first = jnp.min(jnp.where(v == m, idx, float(size)), axis=0, keepdims=True)
        hit = idx == first
        vals.append(m)
        if payload is None:
            picks.append(first)
        else:
            picks.append(jnp.max(jnp.where(hit, payload, -1.0), axis=0, keepdims=True))
        v = jnp.where(hit, NEG_INF, v)
    return jnp.concatenate(vals, axis=0), jnp.concatenate(picks, axis=0)


def _proj_kernel(x_ref, g_ref, w_ref, o_ref, *, act, tn):
    h = _rms(x_ref[...], g_ref[...]).astype(BF16)
    for c in range(w_ref.shape[1] // tn):
        z = _dot(h, w_ref[:, c * tn:(c + 1) * tn])
        if act == "sigmoid":
            z = jax.nn.sigmoid(z)
        o_ref[:, c * tn:(c + 1) * tn] = z.astype(o_ref.dtype)


def _norm_proj(x, g, w, *, act=None, out_dtype=F32, tm=256, tn=512, wn=2048):
    m, d = x.shape
    n = w.shape[1]
    tm = min(tm, m)
    wn = min(wn, n)
    return pl.pallas_call(
        functools.partial(_proj_kernel, act=act, tn=tn),
        grid=(n // wn, m // tm),
        in_specs=[pl.BlockSpec((tm, d), lambda j, i: (i, 0)),
                  pl.BlockSpec((1, d), lambda j, i: (0, 0)),
                  pl.BlockSpec((d, wn), lambda j, i: (0, j))],
        out_specs=pl.BlockSpec((tm, wn), lambda j, i: (i, j)),
        out_shape=jax.ShapeDtypeStruct((m, n), out_dtype),
        compiler_params=_params("parallel", "parallel"),
        name="norm_proj",
    )(x, g, w)


QKV_COLS = N_HEADS * HEAD_DIM + 6 * KV_HEADS * HEAD_DIM + HEAD_DIM


def _qkv_kernel(x_ref, g_ref, w_ref, cos_ref, sin_ref, qq_ref, nsa_ref, win_ref, gate_ref, kvb_ref):
    h = _rms(x_ref[...], g_ref[...]).astype(BF16)
    cos = cos_ref[...]
    sin = sin_ref[...]

    def rope(z):
        return z * cos + pltpu.roll(z, HEAD_DIM // 2, 1) * sin

    qw = N_HEADS * HEAD_DIM
    for c in range(qw // 512):
        z = _dot(h, w_ref[:, c * 512:(c + 1) * 512])
        for j in range(4):
            zh = z[:, j * 128:(j + 1) * 128]
            col = c * 512 + j * 128
            qq_ref[:, col:col + 128] = zh.astype(BF16)
            qq_ref[:, qw + col:qw + col + 128] = rope(zh).astype(BF16)
    for c in range(3):
        z = _dot(h, w_ref[:, qw + c * 512:qw + (c + 1) * 512])
        if c == 0:
            nsa_ref[:, 0:512] = z
        else:
            for j in range(2):
                zr = rope(z[:, j * 128:(j + 1) * 128])
                zv = z[:, 256 + j * 128:256 + (j + 1) * 128]
                if c == 1:
                    nsa_ref[:, 512 + j * 128:512 + (j + 1) * 128] = zr
                    nsa_ref[:, 768 + j * 128:768 + (j + 1) * 128] = zv
                    kvb_ref[:, j * 128:(j + 1) * 128] = zr.astype(BF16)
                    kvb_ref[:, 256 + j * 128:256 + (j + 1) * 128] = zv.astype(BF16)
                else:
                    win_ref[:, j * 128:(j + 1) * 128] = zr
                    win_ref[:, 256 + j * 128:256 + (j + 1) * 128] = zv
                    kvb_ref[:, 512 + j * 128:512 + (j + 1) * 128] = zr.astype(BF16)
                    kvb_ref[:, 768 + j * 128:768 + (j + 1) * 128] = zv.astype(BF16)
    z = _dot(h, w_ref[:, qw + 1536:qw + 1536 + 128])
    gate_ref[...] = jax.nn.sigmoid(z)


def _qkv_proj(x, g, w, cos, sin, *, tm=256):
    m, d = x.shape
    tm = min(tm, m)
    row = lambda i: (i, 0)
    const = lambda i: (0, 0)
    return pl.pallas_call(
        _qkv_kernel,
        grid=(m // tm,),
        in_specs=[pl.BlockSpec((tm, d), row), pl.BlockSpec((1, d), const),
                  pl.BlockSpec((d, QKV_COLS), const),
                  pl.BlockSpec((tm, HEAD_DIM), row), pl.BlockSpec((tm, HEAD_DIM), row)],
        out_specs=[pl.BlockSpec((tm, 2048), row), pl.BlockSpec((tm, 1024), row),
                   pl.BlockSpec((tm, 512), row), pl.BlockSpec((tm, 128), row),
                   pl.BlockSpec((tm, 1024), row)],
        out_shape=[jax.ShapeDtypeStruct((m, 2048), BF16), jax.ShapeDtypeStruct((m, 1024), F32),
                   jax.ShapeDtypeStruct((m, 512), F32), jax.ShapeDtypeStruct((m, 128), F32),
                   jax.ShapeDtypeStruct((m, 1024), BF16)],
        compiler_params=_params("parallel"),
        name="qkv_proj",
    )(x, g, w, cos, sin)


ROW_W = 4 * KV_HEADS * HEAD_DIM
SUB_W = CMP_STRIDE * ROW_W


def _compress_rows(load, n_rows, wk_ref, wv_ref, y_ref):
    for kv in range(2):
        w_ref = wk_ref if kv == 0 else wv_ref
        for g in range(KV_HEADS):
            col = (kv * KV_HEADS + g) * HEAD_DIM
            acc = jnp.zeros((n_rows, 2 * HEAD_DIM), F32)
            for i in range(CMP_STRIDE):
                a = load(i * ROW_W + col, i * ROW_W + col + HEAD_DIM).astype(BF16)
                acc = acc + _dot(a, w_ref[i])
            y_ref[0, :, 2 * col:2 * col + 2 * HEAD_DIM] = acc


def _compress_prompt_kernel(x_ref, wk_ref, wv_ref, y_ref):
    _compress_rows(lambda lo, hi: x_ref[0, :, lo:hi], x_ref.shape[1], wk_ref, wv_ref, y_ref)


def _compress_prompt(nsa, wk, wv, *, ts=128):
    b, t, _ = nsa.shape
    n_sub = t // CMP_STRIDE
    ts = min(ts, n_sub)
    wspec = pl.BlockSpec((CMP_STRIDE, HEAD_DIM, 2 * HEAD_DIM), lambda bi, ci: (0, 0, 0))
    return pl.pallas_call(
        _compress_prompt_kernel,
        grid=(b, n_sub // ts),
        in_specs=[pl.BlockSpec((1, ts, SUB_W), lambda bi, ci: (bi, ci, 0)), wspec, wspec],
        out_specs=pl.BlockSpec((1, ts, 1024), lambda bi, ci: (bi, ci, 0)),
        out_shape=jax.ShapeDtypeStruct((b, n_sub, 1024), F32),
        compiler_params=_params("parallel", "parallel"),
        name="compress_prompt",
    )(nsa.reshape(b, n_sub, SUB_W), wk, wv)


PAGES_PER_STEP = 16


def _compress_pages_kernel(pt_ref, *refs):
    pages = refs[:PAGES_PER_STEP]
    wk_ref, wv_ref, y_ref = refs[PAGES_PER_STEP:]
    per_page = pages[0].shape[1]
    load = lambda lo, hi: jnp.concatenate([p[0, :, lo:hi] for p in pages], axis=0)
    _compress_rows(load, PAGES_PER_STEP * per_page, wk_ref, wv_ref, y_ref)


def _page_specs(block, col):
    def spec(k):
        return pl.BlockSpec(block, lambda bi, ci, pt: (pt[bi, ci * PAGES_PER_STEP + k], 0, col))
    return [spec(k) for k in range(PAGES_PER_STEP)]


def _compress_pages(cache, page_table, wk, wv):
    nb, n_pages = page_table.shape
    n_phys, page_size, _ = cache.shape
    per_page = page_size // CMP_STRIDE
    rows = PAGES_PER_STEP * per_page
    const3 = lambda bi, ci, pt: (0, 0, 0)
    grid_spec = pltpu.PrefetchScalarGridSpec(
        num_scalar_prefetch=1,
        grid=(nb, n_pages // PAGES_PER_STEP),
        in_specs=_page_specs((1, per_page, SUB_W), 0) + [
            pl.BlockSpec((CMP_STRIDE, HEAD_DIM, 2 * HEAD_DIM), const3),
            pl.BlockSpec((CMP_STRIDE, HEAD_DIM, 2 * HEAD_DIM), const3)],
        out_specs=pl.BlockSpec((1, rows, 1024), lambda bi, ci, pt: (bi, ci, 0)),
    )
    sub = cache.reshape(n_phys, per_page, SUB_W)
    return pl.pallas_call(
        _compress_pages_kernel,
        grid_spec=grid_spec,
        out_shape=jax.ShapeDtypeStruct((nb, n_pages * per_page, 1024), F32),
        compiler_params=_params("parallel", "parallel"),
        name="compress_pages",
    )(page_table, *([sub] * PAGES_PER_STEP), wk, wv)


def _combine_compressed(y_ref, kcp_ref, tmp_ref):
    ns = y_ref.shape[1]
    nb = ns // CMP_PER_SEL
    last = lax.broadcasted_iota(jnp.int32, (ns, 1), 0) == ns - 1
    for a in range(2 * KV_HEADS):
        y1 = y_ref[0, :, a * 256:a * 256 + 128]
        y2 = y_ref[0, :, a * 256 + 128:(a + 1) * 256]
        nxt = jnp.where(last, 0.0, pltpu.roll(y2, ns - 1, 0))
        tmp_ref[...] = y1 + nxt
        for c in range(CMP_PER_SEL):
            kcp_ref[a * CMP_PER_SEL + c] = tmp_ref[pl.ds(c, nb, stride=CMP_PER_SEL), :].astype(BF16)


def _cmp_attention(q_rows, pos, kcp_ref, g, rows_per_tok):
    nb = kcp_ref.shape[1]
    blk = lax.broadcasted_iota(jnp.int32, (1, nb), 1)
    masks = [(SEL_BLOCK * blk + CMP_STRIDE * c + CMP_LEN - 1) <= pos for c in range(CMP_PER_SEL)]
    outs = []
    imp = None
    for qh in q_rows:
        s = [_dot_nt(qh, kcp_ref[g * CMP_PER_SEL + c]) * SCALE for c in range(CMP_PER_SEL)]
        p = _masked_softmax_parts(s, masks)
        o = functools.reduce(jnp.add, [
            _dot(p[c].astype(BF16), kcp_ref[(KV_HEADS + g) * CMP_PER_SEL + c]) for c in range(CMP_PER_SEL)])
        outs.append(o)
        ps = functools.reduce(jnp.add, p)
        imp = ps if imp is None else imp + ps
    return outs, imp


def _cmp_prompt_kernel(q_ref, y_ref, o_ref, sel_ref, kcp_ref, tmp_ref):
    i = pl.program_id(1)

    @pl.when(i == 0)
    def _():
        _combine_compressed(y_ref, kcp_ref, tmp_ref)

    nb = kcp_ref.shape[1]
    pos = i * Q_BLOCK + lax.broadcasted_iota(jnp.int32, (Q_BLOCK, 1), 0)
    blk = lax.broadcasted_iota(jnp.int32, (1, nb), 1)
    cur = pos // SEL_BLOCK
    forced = (blk == cur) | (blk == 0)
    causal = blk <= cur
    for g in range(KV_HEADS):
        q_rows = [q_ref[0, :, (g * GROUP + r) * HEAD_DIM:(g * GROUP + r + 1) * HEAD_DIM] for r in range(GROUP)]
        outs, imp = _cmp_attention(q_rows, pos, kcp_ref, g, 1)
        for r in range(GROUP):
            hcol = (g * GROUP + r) * HEAD_DIM
            o_ref[0, :, hcol:hcol + HEAD_DIM] = outs[r]
        v = jnp.where(forced, FORCE_SCORE, jnp.where(causal, imp, -1.0))
        sel_t = _topk_select(v.T, SEL_TOPN, 0)
        sel = jnp.where(causal, sel_t.T, 0.0)
        sel_ref[0, :, g * nb:(g + 1) * nb] = sel.astype(BF16)


def _cmp_prompt(qq3, y):
    b, t, _ = qq3.shape
    ns = y.shape[1]
    nb = ns // CMP_PER_SEL
    return pl.pallas_call(
        _cmp_prompt_kernel,
        grid=(b, t // Q_BLOCK),
        in_specs=[pl.BlockSpec((1, Q_BLOCK, 1024), lambda bi, i: (bi, i, 0)),
                  pl.BlockSpec((1, ns, 1024), lambda bi, i: (bi, 0, 0))],
        out_specs=[pl.BlockSpec((1, Q_BLOCK, 1024), lambda bi, i: (bi, i, 0)),
                   pl.BlockSpec((1, Q_BLOCK, KV_HEADS * nb), lambda bi, i: (bi, i, 0))],
        out_shape=[jax.ShapeDtypeStruct((b, t, 1024), F32),
                   jax.ShapeDtypeStruct((b, t, KV_HEADS * nb), BF16)],
        scratch_shapes=[pltpu.VMEM((2 * KV_HEADS * CMP_PER_SEL, nb, HEAD_DIM), BF16),
                        pltpu.VMEM((ns, HEAD_DIM), F32)],
        compiler_params=_params("parallel", "arbitrary"),
        name="cmp_prompt",
    )(qq3, y)


SEL_CHUNK = 512


def _expand_blocks(sel, first_block, width):
    nb = sel.shape[1]
    b = lax.broadcasted_iota(jnp.int32, (nb, width), 0)
    t = lax.broadcasted_iota(jnp.int32, (nb, width), 1)
    e = jnp.where(b == first_block + t // SEL_BLOCK, 1.0, 0.0).astype(BF16)
    return _dot(sel, e)


def _sel_prompt_kernel(q_ref, sel_ref, kv_ref, o_ref):
    i = pl.program_id(1)
    nb = sel_ref.shape[2] // KV_HEADS
    pos = i * Q_BLOCK + lax.broadcasted_iota(jnp.int32, (Q_BLOCK, 1), 0)
    pos = jnp.concatenate([pos] * GROUP, axis=0)
    n_chunks = (i * Q_BLOCK + Q_BLOCK + SEL_CHUNK - 1) // SEL_CHUNK
    lane = lax.broadcasted_iota(jnp.int32, (1, SEL_CHUNK), 1)
    rows = GROUP * Q_BLOCK
    for g in range(KV_HEADS):
        q = jnp.concatenate(
            [q_ref[0, :, (g * GROUP + r) * HEAD_DIM:(g * GROUP + r + 1) * HEAD_DIM] for r in range(GROUP)], axis=0)
        sel = sel_ref[0, :, g * nb:(g + 1) * nb]

        def body(c, carry):
            m, l, acc = carry
            start = pl.multiple_of(c * SEL_CHUNK, SEL_CHUNK)
            k = kv_ref[0, pl.ds(start, SEL_CHUNK), g * HEAD_DIM:(g + 1) * HEAD_DIM]
            v = kv_ref[0, pl.ds(start, SEL_CHUNK), (KV_HEADS + g) * HEAD_DIM:(KV_HEADS + g + 1) * HEAD_DIM]
            s = _dot_nt(q, k) * SCALE
            picked = _expand_blocks(sel, c * (SEL_CHUNK // SEL_BLOCK), SEL_CHUNK)
            picked = jnp.concatenate([picked] * GROUP, axis=0)
            ok = (picked > 0.5) & ((start + lane) <= pos)
            s = jnp.where(ok, s, NEG_INF)
            m_new = jnp.maximum(m, jnp.max(s, axis=-1, keepdims=True))
            m_safe = jnp.where(m_new == NEG_INF, 0.0, m_new)
            p = jnp.exp(s - m_safe)
            alpha = jnp.exp(m - m_safe)
            l = alpha * l + jnp.sum(p, axis=-1, keepdims=True)
            acc = alpha * acc + _dot(p.astype(BF16), v)
            return m_new, l, acc

        init = (jnp.full((rows, 1), NEG_INF, F32), jnp.zeros((rows, 1), F32), jnp.zeros((rows, HEAD_DIM), F32))
        m, l, acc = lax.fori_loop(0, n_chunks, body, init)
        o = acc * (1.0 / jnp.maximum(l, 1e-30))
        for r in range(GROUP):
            hcol = (g * GROUP + r) * HEAD_DIM
            o_ref[0, :, hcol:hcol + HEAD_DIM] = o[r * Q_BLOCK:(r + 1) * Q_BLOCK]


def _sel_prompt(qq3, sel, kvb3):
    b, t, _ = qq3.shape
    nb2 = sel.shape[2]
    return pl.pallas_call(
        _sel_prompt_kernel,
        grid=(b, t // Q_BLOCK),
        in_specs=[pl.BlockSpec((1, Q_BLOCK, 1024), lambda bi, i: (bi, i, 1)),
                  pl.BlockSpec((1, Q_BLOCK, nb2), lambda bi, i: (bi, i, 0)),
                  pl.BlockSpec((1, t, 512), lambda bi, i: (bi, 0, 0))],
        out_specs=pl.BlockSpec((1, Q_BLOCK, 1024), lambda bi, i: (bi, i, 0)),
        out_shape=jax.ShapeDtypeStruct((b, t, 1024), F32),
        compiler_params=_params("parallel", "arbitrary"),
        name="sel_prompt",
    )(qq3, sel, kvb3)


def _win_prompt_kernel(q_ref, kv_ref, o_ref, *, span):
    i = pl.program_id(1)
    pos = i * Q_BLOCK + lax.broadcasted_iota(jnp.int32, (Q_BLOCK, 1), 0)
    pos = jnp.concatenate([pos] * GROUP, axis=0)
    start = pl.multiple_of(jnp.maximum(i * Q_BLOCK + Q_BLOCK - span, 0), Q_BLOCK)
    kpos = start + lax.broadcasted_iota(jnp.int32, (1, span), 1)
    diff = pos - kpos
    ok = (diff >= 0) & (diff < WINDOW)
    for g in range(KV_HEADS):
        q = jnp.concatenate(
            [q_ref[0, :, (g * GROUP + r) * HEAD_DIM:(g * GROUP + r + 1) * HEAD_DIM] for r in range(GROUP)], axis=0)
        k = kv_ref[0, pl.ds(start, span), g * HEAD_DIM:(g + 1) * HEAD_DIM]
        v = kv_ref[0, pl.ds(start, span), (KV_HEADS + g) * HEAD_DIM:(KV_HEADS + g + 1) * HEAD_DIM]
        s = _dot_nt(q, k) * SCALE
        (p,) = _masked_softmax_parts([s], [ok])
        o = _dot(p.astype(BF16), v)
        for r in range(GROUP):
            hcol = (g * GROUP + r) * HEAD_DIM
            o_ref[0, :, hcol:hcol + HEAD_DIM] = o[r * Q_BLOCK:(r + 1) * Q_BLOCK]


def _win_prompt(qq3, kvb3):
    b, t, _ = qq3.shape
    span = min(WINDOW + Q_BLOCK, t)
    return pl.pallas_call(
        functools.partial(_win_prompt_kernel, span=span),
        grid=(b, t // Q_BLOCK),
        in_specs=[pl.BlockSpec((1, Q_BLOCK, 1024), lambda bi, i: (bi, i, 1)),
                  pl.BlockSpec((1, t, 512), lambda bi, i: (bi, 0, 1))],
        out_specs=pl.BlockSpec((1, Q_BLOCK, 1024), lambda bi, i: (bi, i, 0)),
        out_shape=jax.ShapeDtypeStruct((b, t, 1024), F32),
        compiler_params=_params("parallel", "arbitrary"),
        name="win_prompt",
    )(qq3, kvb3)


def _pool_kernel(prev_ref, u_ref, w_ref, sc_ref, o_ref, *, base, zero_first_prev):
    i = pl.program_id(1)
    tq = u_ref.shape[1]
    cur = u_ref[0]
    prev = prev_ref[0]
    if zero_first_prev:
        prev = jnp.where(i == 0, 0.0, prev)
    ext = jnp.concatenate([prev, cur], axis=0)
    gpos = base + i * tq + lax.broadcasted_iota(jnp.int32, (tq, 1), 0)
    gw = ext.shape[1] // len(POOL_WINDOWS)
    for gi, w in enumerate(POOL_WINDOWS):
        s = ext[:, gi * gw:(gi + 1) * gw]
        span = 1
        while span < w:
            s = s + pltpu.roll(s, span, 0)
            span *= 2
        cnt = jnp.minimum(gpos + 1, w).astype(F32)
        d = s[POOL_PREV:] / cnt - cur[:, gi * gw:(gi + 1) * gw]
        o = _dot(d.astype(BF16), w_ref[gi]) * sc_ref[:, gi * gw:(gi + 1) * gw]
        o_ref[0, :, gi * gw:(gi + 1) * gw] = o


def _pool(prev, u3, w, scale, *, base, zero_first_prev, tq=512):
    b, t, c = u3.shape
    tq = min(tq, t)
    ratio = tq // POOL_PREV
    if zero_first_prev:
        prev_map = lambda bi, i: (bi, jnp.maximum(i * ratio - 1, 0), 0)
    else:
        prev_map = lambda bi, i: (bi, 0, 0)
    ng = len(POOL_WINDOWS)
    return pl.pallas_call(
        functools.partial(_pool_kernel, base=base, zero_first_prev=zero_first_prev),
        grid=(b, t // tq),
        in_specs=[pl.BlockSpec((1, POOL_PREV, c), prev_map),
                  pl.BlockSpec((1, tq, c), lambda bi, i: (bi, i, 0)),
                  pl.BlockSpec((ng, c // ng, c // ng), lambda bi, i: (0, 0, 0)),
                  pl.BlockSpec((1, c), lambda bi, i: (0, 0))],
        out_specs=pl.BlockSpec((1, tq, c), lambda bi, i: (bi, i, 0)),
        out_shape=jax.ShapeDtypeStruct((b, t, c), F32),
        compiler_params=_params("parallel", "parallel"),
        name="pool_mix",
    )(prev, u3, w, scale)


def _mix_kernel(oc_ref, os_ref, ow_ref, gt_ref, po_ref, ga_ref, gb_ref, wa_ref, wp_ref, mix_ref):
    gt = gt_ref[...]
    cols = []
    for h in range(N_HEADS):
        sl = slice(h * HEAD_DIM, (h + 1) * HEAD_DIM)
        o = (gt[:, h:h + 1] * oc_ref[:, sl] + gt[:, N_HEADS + h:N_HEADS + h + 1] * os_ref[:, sl]
             + gt[:, 2 * N_HEADS + h:2 * N_HEADS + h + 1] * ow_ref[:, sl])
        cols.append(o.astype(BF16))
    a = _dot(jnp.concatenate(cols, axis=1), wa_ref[...])
    p = _dot(po_ref[...].astype(BF16), wp_ref[...])
    mix_ref[...] = (ga_ref[...] * a + gb_ref[...] * p).astype(BF16)


def _mix(oc, os_, ow, gt, po, gab, wa, wp, *, tm=256):
    m = oc.shape[0]
    d = wa.shape[1]
    tm = min(tm, m)
    row = lambda i: (i, 0)
    const = lambda i: (0, 0)
    return pl.pallas_call(
        _mix_kernel,
        grid=(m // tm,),
        in_specs=[pl.BlockSpec((tm, 1024), row), pl.BlockSpec((tm, 1024), row), pl.BlockSpec((tm, 1024), row),
                  pl.BlockSpec((tm, 128), row), pl.BlockSpec((tm, 1024), row),
                  pl.BlockSpec((tm, d), lambda i: (i, 0)), pl.BlockSpec((tm, d), lambda i: (i, 1)),
                  pl.BlockSpec((1024, d), const), pl.BlockSpec((1024, d), const)],
        out_specs=pl.BlockSpec((tm, d), row),
        out_shape=jax.ShapeDtypeStruct((m, d), BF16),
        compiler_params=_params("parallel"),
        name="branch_mix",
    )(oc, os_, ow, gt, po, gab, gab, wa, wp)


def _out_kernel(x_ref, mix_ref, w_ref, o_ref):
    o_ref[...] = x_ref[...] + _dot(mix_ref[...], w_ref[...])


def _out_proj(x, mix, w, *, tm=256):
    m, d = x.shape
    tm = min(tm, m)
    return pl.pallas_call(
        _out_kernel,
        grid=(m // tm,),
        in_specs=[pl.BlockSpec((tm, d), lambda i: (i, 0)), pl.BlockSpec((tm, d), lambda i: (i, 0)),
                  pl.BlockSpec((d, d), lambda i: (0, 0))],
        out_specs=pl.BlockSpec((tm, d), lambda i: (i, 0)),
        out_shape=jax.ShapeDtypeStruct((m, d), F32),
        compiler_params=_params("parallel"),
        name="out_proj",
    )(x, mix, w)


def _peer_score_kernel(x_ref, g_ref, wq_ref, keys_ref, ids_ref, gw_ref):
    h = _rms(x_ref[...], g_ref[...]).astype(BF16)
    q = _dot(h, wq_ref[...]).astype(BF16)
    dk = PEER_NKEYS
    ids, gws = [], []
    for hd in range(PEER_HEADS):
        sv, si = [], []
        for c in range(2):
            qhc = q[:, (hd * 2 + c) * dk:(hd * 2 + c + 1) * dk]
            st = _dot_nt(keys_ref[hd * 2 + c], qhc)
            v, ix = _topk_sorted(st, PEER_TOPK)
            sv.append(v)
            si.append(ix)
        comb = jnp.concatenate([sv[0][a:a + 1] + sv[1] for a in range(PEER_TOPK)], axis=0)
        eid = jnp.concatenate([si[0][a:a + 1] * float(PEER_NKEYS) + si[1] for a in range(PEER_TOPK)], axis=0)
        cv, ce = _topk_sorted(comb, PEER_TOPK, payload=eid)
        e = jnp.exp(cv - cv[0:1])
        gws.append(e / jnp.sum(e, axis=0, keepdims=True))
        ids.append(ce)
    ids_ref[...] = jnp.concatenate(ids, axis=0).T.astype(jnp.int32)
    gw_ref[...] = jnp.concatenate(gws, axis=0)


def _peer_score(x, g, wq, keys, *, tb=128):
    m, d = x.shape
    nk = PEER_HEADS * PEER_TOPK
    return pl.pallas_call(
        _peer_score_kernel,
        grid=(m // tb,),
        in_specs=[pl.BlockSpec((tb, d), lambda i: (i, 0)), pl.BlockSpec((1, d), lambda i: (0, 0)),
                  pl.BlockSpec(wq.shape, lambda i: (0, 0)),
                  pl.BlockSpec(keys.shape, lambda i: (0, 0, 0))],
        out_specs=[pl.BlockSpec((tb, nk), lambda i: (i, 0)), pl.BlockSpec((nk, tb), lambda i: (0, i))],
        out_shape=[jax.ShapeDtypeStruct((m, nk), jnp.int32), jax.ShapeDtypeStruct((nk, m), F32)],
        compiler_params=_params("parallel"),
        name="peer_score",
    )(x, g, wq, keys)


def _gelu(x):
    return 0.5 * x * (1.0 + lax.erf(x * (2.0 ** -0.5)))


def _peer_apply_kernel(ids_ref, x_ref, gffn_ref, gw_ref, gfin_ref, tab_ref, y_ref, buf, sem, hbuf, obuf):
    tb, d = x_ref.shape
    nk = buf.shape[1]
    nchunk = d // 128
    hbuf[...] = _rms(x_ref[...], gffn_ref[...])

    def row_copy(t, k, slot):
        e = ids_ref[t, k]
        return pltpu.make_async_copy(tab_ref.at[pl.ds(e, 1), :], buf.at[slot, pl.ds(k, 1), :], sem.at[slot])

    def issue(t, slot):
        for k in range(nk):
            row_copy(t, k, slot).start()

    def drain(t, slot):
        for k in range(nk):
            row_copy(t, k, slot).wait()

    issue(0, 0)
    lane = lax.broadcasted_iota(jnp.int32, (nk, tb), 1)

    def body(t, carry):
        slot = t % 2

        @pl.when(t + 1 < tb)
        def _():
            issue(t + 1, 1 - slot)

        drain(t, slot)
        hrow = hbuf[pl.ds(t, 1), :]
        acc = jnp.zeros((nk, 128), F32)
        for j in range(nchunk):
            w = buf[slot, :, j * 128:(j + 1) * 128]
            u = pltpu.bitcast(w << 16, F32)
            acc = acc + u * hrow[:, j * 128:(j + 1) * 128]
        act = jnp.sum(acc, axis=1, keepdims=True)
        gcol = jnp.sum(jnp.where(lane == t, gw_ref[...], 0.0), axis=1, keepdims=True)
        coef = _gelu(act) * gcol
        outs = []
        for j in range(nchunk):
            w = buf[slot, :, j * 128:(j + 1) * 128]
            v = pltpu.bitcast(w & jnp.uint32(0xFFFF0000), F32)
            outs.append(jnp.sum(v * coef, axis=0, keepdims=True))
        obuf[pl.ds(t, 1), :] = jnp.concatenate(outs, axis=1)
        return carry

    lax.fori_loop(0, tb, body, 0)
    y_ref[...] = _rms(x_ref[...] + obuf[...], gfin_ref[...])


def _peer_apply(ids, x, gffn, gw, gfin, table, *, tb=128):
    m, d = x.shape
    nk = ids.shape[1]
    return pl.pallas_call(
        _peer_apply_kernel,
        grid=(m // tb,),
        in_specs=[pl.BlockSpec((tb, nk), lambda i: (i, 0), memory_space=pltpu.SMEM),
                  pl.BlockSpec((tb, d), lambda i: (i, 0)),
                  pl.BlockSpec((1, d), lambda i: (0, 0)),
                  pl.BlockSpec((nk, tb), lambda i: (0, i)),
                  pl.BlockSpec((1, d), lambda i: (0, 0)),
                  pl.BlockSpec(memory_space=pl.ANY)],
        out_specs=pl.BlockSpec((tb, d), lambda i: (i, 0)),
        out_shape=jax.ShapeDtypeStruct((m, d), F32),
        scratch_shapes=[pltpu.VMEM((2, nk, d), jnp.uint32), pltpu.SemaphoreType.DMA((2,)),
                        pltpu.VMEM((tb, d), F32), pltpu.VMEM((tb, d), F32)],
        compiler_params=_params("arbitrary"),
        name="peer_apply",
    )(ids, x, gffn, gw, gfin, table)


def _stack_heads(q_ref, g):
    return jnp.concatenate(
        [q_ref[0, :, (g * GROUP + r) * HEAD_DIM:(g * GROUP + r + 1) * HEAD_DIM] for r in range(GROUP)], axis=0)


NEW_PAD = 128


def _pad_new(x):
    return jnp.concatenate([x, jnp.zeros((NEW_PAD - x.shape[0], x.shape[1]), F32)], axis=0).astype(BF16)


def _cmp_sample_kernel(q_ref, y_ref, o_ref, sel_ref, kcp_ref, tmp_ref, *, past):
    _combine_compressed(y_ref, kcp_ref, tmp_ref)
    t = q_ref.shape[1]
    nb = kcp_ref.shape[1]
    tpos = past + lax.broadcasted_iota(jnp.int32, (t, 1), 0)
    pos = jnp.concatenate([tpos] * GROUP, axis=0)
    blk = lax.broadcasted_iota(jnp.int32, (1, nb), 1)
    for g in range(KV_HEADS):
        q = _stack_heads(q_ref, g)
        outs, imp = _cmp_attention([q], pos, kcp_ref, g, t)
        o = outs[0]
        for r in range(GROUP):
            hcol = (g * GROUP + r) * HEAD_DIM
            o_ref[0, :, hcol:hcol + HEAD_DIM] = o[r * t:(r + 1) * t]
        imp_t = functools.reduce(jnp.add, [imp[r * t:(r + 1) * t] for r in range(GROUP)])
        v = jnp.where(blk == 0, FORCE_SCORE, imp_t)
        sel = _topk_select(v, SEL_TOPN - 1, 1)
        sel_ref[0, :, g * nb:(g + 1) * nb] = sel.astype(BF16)


def _cmp_sample(qq3, y, *, past):
    b, t, _ = qq3.shape
    ns = y.shape[1]
    nb = ns // CMP_PER_SEL
    return pl.pallas_call(
        functools.partial(_cmp_sample_kernel, past=past),
        grid=(b,),
        in_specs=[pl.BlockSpec((1, t, 1024), lambda bi: (bi, 0, 0)),
                  pl.BlockSpec((1, ns, 1024), lambda bi: (bi, 0, 0))],
        out_specs=[pl.BlockSpec((1, t, 1024), lambda bi: (bi, 0, 0)),
                   pl.BlockSpec((1, t, KV_HEADS * nb), lambda bi: (bi, 0, 0))],
        out_shape=[jax.ShapeDtypeStruct((b, t, 1024), F32),
                   jax.ShapeDtypeStruct((b, t, KV_HEADS * nb), BF16)],
        scratch_shapes=[pltpu.VMEM((2 * KV_HEADS * CMP_PER_SEL, nb, HEAD_DIM), BF16),
                        pltpu.VMEM((ns, HEAD_DIM), F32)],
        compiler_params=_params("parallel"),
        name="cmp_sample",
    )(qq3, y)


def _sel_sample_kernel(pt_ref, *refs, past):
    pages = refs[:PAGES_PER_STEP]
    q_ref, sel_ref, new_ref, o_ref, m_ref, l_ref, acc_ref = refs[PAGES_PER_STEP:]
    c = pl.program_id(1)
    nc = pl.num_programs(1)
    t = q_ref.shape[1]
    rows = GROUP * t
    nb = sel_ref.shape[2] // KV_HEADS
    page = pages[0].shape[1]
    width = PAGES_PER_STEP * page

    @pl.when(c == 0)
    def _():
        m_ref[...] = jnp.full(m_ref.shape, NEG_INF, F32)
        l_ref[...] = jnp.zeros(l_ref.shape, F32)
        acc_ref[...] = jnp.zeros(acc_ref.shape, F32)

    def update(g, s, ok, v):
        s = jnp.where(ok, s, NEG_INF)
        m = m_ref[g]
        m_new = jnp.maximum(m, jnp.max(s, axis=-1, keepdims=True))
        m_safe = jnp.where(m_new == NEG_INF, 0.0, m_new)
        p = jnp.exp(s - m_safe)
        alpha = jnp.exp(m - m_safe)
        l_ref[g] = alpha * l_ref[g] + jnp.sum(p, axis=-1, keepdims=True)
        acc_ref[g] = alpha * acc_ref[g] + _dot(p.astype(BF16), v)
        m_ref[g] = m_new

    for g in range(KV_HEADS):
        q = _stack_heads(q_ref, g)
        k = jnp.concatenate([p[0, :, g * HEAD_DIM:(g + 1) * HEAD_DIM] for p in pages], axis=0).astype(BF16)
        v = jnp.concatenate(
            [p[0, :, (KV_HEADS + g) * HEAD_DIM:(KV_HEADS + g + 1) * HEAD_DIM] for p in pages], axis=0).astype(BF16)
        s = _dot_nt(q, k) * SCALE
        picked = _expand_blocks(sel_ref[0, :, g * nb:(g + 1) * nb], c * (width // SEL_BLOCK), width)
        picked = jnp.concatenate([picked] * GROUP, axis=0)
        update(g, s, picked > 0.5, v)

    @pl.when(c == nc - 1)
    def _():
        qi = lax.broadcasted_iota(jnp.int32, (t, 1), 0)
        qi = jnp.concatenate([qi] * GROUP, axis=0)
        ok = lax.broadcasted_iota(jnp.int32, (1, NEW_PAD), 1) <= qi
        for g in range(KV_HEADS):
            q = _stack_heads(q_ref, g)
            k = _pad_new(new_ref[0, :, 512 + g * HEAD_DIM:512 + (g + 1) * HEAD_DIM])
            v = _pad_new(new_ref[0, :, 768 + g * HEAD_DIM:768 + (g + 1) * HEAD_DIM])
            update(g, _dot_nt(q, k) * SCALE, ok, v)
            o = acc_ref[g] * (1.0 / jnp.maximum(l_ref[g], 1e-30))
            for r in range(GROUP):
                hcol = (g * GROUP + r) * HEAD_DIM
                o_ref[0, :, hcol:hcol + HEAD_DIM] = o[r * t:(r + 1) * t]


def _sel_sample(cache, page_table, qq3, sel, nsa3, *, past):
    nbatch, n_pages = page_table.shape
    page_size = cache.shape[1]
    t = qq3.shape[1]
    nb2 = sel.shape[2]
    rows = GROUP * t
    grid_spec = pltpu.PrefetchScalarGridSpec(
        num_scalar_prefetch=1,
        grid=(nbatch, n_pages // PAGES_PER_STEP),
        in_specs=_page_specs((1, page_size, 512), 1) + [
            pl.BlockSpec((1, t, 1024), lambda bi, ci, pt: (bi, 0, 1)),
            pl.BlockSpec((1, t, nb2), lambda bi, ci, pt: (bi, 0, 0)),
            pl.BlockSpec((1, t, 1024), lambda bi, ci, pt: (bi, 0, 0))],
        out_specs=pl.BlockSpec((1, t, 1024), lambda bi, ci, pt: (bi, 0, 0)),
        scratch_shapes=[pltpu.VMEM((KV_HEADS, rows, 1), F32), pltpu.VMEM((KV_HEADS, rows, 1), F32),
                        pltpu.VMEM((KV_HEADS, rows, HEAD_DIM), F32)],
    )
    return pl.pallas_call(
        functools.partial(_sel_sample_kernel, past=past),
        grid_spec=grid_spec,
        out_shape=jax.ShapeDtypeStruct((nbatch, t, 1024), F32),
        compiler_params=_params("parallel", "arbitrary"),
        name="sel_sample",
    )(page_table, *([cache] * PAGES_PER_STEP), qq3, sel, nsa3)


def _win_sample_kernel(q_ref, st_ref, new_ref, o_ref):
    t = q_ref.shape[1]
    nw = st_ref.shape[1]
    qi = lax.broadcasted_iota(jnp.int32, (t, 1), 0)
    qi = jnp.concatenate([qi] * GROUP, axis=0)
    d_old = (nw + qi) - lax.broadcasted_iota(jnp.int32, (1, nw), 1)
    ok_old = (d_old >= 0) & (d_old < WINDOW)
    d_new = qi - lax.broadcasted_iota(jnp.int32, (1, NEW_PAD), 1)
    ok_new = (d_new >= 0) & (d_new < WINDOW)
    for g in range(KV_HEADS):
        q = _stack_heads(q_ref, g)
        k_old = st_ref[0, :, g * HEAD_DIM:(g + 1) * HEAD_DIM].astype(BF16)
        v_old = st_ref[0, :, (KV_HEADS + g) * HEAD_DIM:(KV_HEADS + g + 1) * HEAD_DIM].astype(BF16)
        k_new = _pad_new(new_ref[0, :, g * HEAD_DIM:(g + 1) * HEAD_DIM])
        v_new = _pad_new(new_ref[0, :, (KV_HEADS + g) * HEAD_DIM:(KV_HEADS + g + 1) * HEAD_DIM])
        p_old, p_new = _masked_softmax_parts(
            [_dot_nt(q, k_old) * SCALE, _dot_nt(q, k_new) * SCALE], [ok_old, ok_new])
        o = _dot(p_old.astype(BF16), v_old) + _dot(p_new.astype(BF16), v_new)
        for r in range(GROUP):
            hcol = (g * GROUP + r) * HEAD_DIM
            o_ref[0, :, hcol:hcol + HEAD_DIM] = o[r * t:(r + 1) * t]


def _win_sample(qq3, state_win, win3):
    b, t, _ = qq3.shape
    nw = state_win.shape[1]
    return pl.pallas_call(
        _win_sample_kernel,
        grid=(b,),
        in_specs=[pl.BlockSpec((1, t, 1024), lambda bi: (bi, 0, 1)),
                  pl.BlockSpec((1, nw, 512), lambda bi: (bi, 0, 0)),
                  pl.BlockSpec((1, t, 512), lambda bi: (bi, 0, 0))],
        out_specs=pl.BlockSpec((1, t, 1024), lambda bi: (bi, 0, 0)),
        out_shape=jax.ShapeDtypeStruct((b, t, 1024), F32),
        compiler_params=_params("parallel"),
        name="win_sample",
    )(qq3, state_win, win3)


def _rope_tables(pos):
    half = HEAD_DIM // 2
    inv = ROPE_THETA ** (-jnp.arange(half, dtype=F32) / half)
    ang = pos.astype(F32)[:, None] * inv[None, :]
    cos, sin = jnp.cos(ang), jnp.sin(ang)
    return jnp.concatenate([cos, cos], axis=-1), jnp.concatenate([-sin, sin], axis=-1)


def _prep_weights(w_in, w_phi_k, w_phi_v, w_pool_group, pool_scale, w_branch_attn, w_branch_pool, w_out,
                  peer_w_query, peer_sub_keys, peer_u, peer_v):
    d = w_in.shape[0]
    qw = N_HEADS * HEAD_DIM
    kvw = 6 * KV_HEADS * HEAD_DIM
    ngw = 3 * N_HEADS
    pw = d // 2
    o1, o2, o3, o4 = qw, qw + kvw, qw + kvw + ngw, qw + kvw + ngw + pw
    wb = w_in.astype(BF16)
    w_qkv = jnp.concatenate([wb[:, :o2], wb[:, o2:o3], jnp.zeros((d, HEAD_DIM - ngw), BF16)], axis=1)
    cat = lambda w: jnp.concatenate([w[:CMP_STRIDE], w[CMP_STRIDE:]], axis=-1).astype(BF16)
    u16 = lax.bitcast_convert_type(peer_u.astype(BF16), jnp.uint16).astype(jnp.uint32)
    v16 = lax.bitcast_convert_type(peer_v.astype(BF16), jnp.uint16).astype(jnp.uint32)
    return dict(
        w_qkv=w_qkv, w_u=wb[:, o3:o4], w_gab=wb[:, o4:],
        wk=cat(w_phi_k), wv=cat(w_phi_v),
        w_pool=w_pool_group.astype(BF16), pool_scale=pool_scale.reshape(1, -1),
        w_ba=w_branch_attn.astype(BF16), w_bp=w_branch_pool.astype(BF16), w_out=w_out.astype(BF16),
        w_query=peer_w_query.astype(BF16),
        keys=peer_sub_keys.reshape(PEER_HEADS * 2, PEER_NKEYS, -1).astype(BF16),
        table=u16 | (v16 << 16),
    )


def _token_tail(x2d, o_cmp, o_sel, o_win, gates, pool_out, gab, wp, g_ffn, g_final):
    mix = _mix(o_cmp, o_sel, o_win, gates, pool_out, gab, wp["w_ba"], wp["w_bp"])
    x2 = _out_proj(x2d, mix, wp["w_out"])
    ids, gw = _peer_score(x2, g_ffn, wp["w_query"], wp["keys"])
    return _peer_apply(ids, x2, g_ffn, gw, g_final, wp["table"])


def kernel(x_prompt, x_sample, cache_kv_nsa, state_win_kv, state_pool, page_table, g_norm_mix, w_in, w_phi_k,
           w_phi_v, w_pool_group, pool_scale, w_branch_attn, w_branch_pool, w_out, g_norm_ffn, peer_w_query,
           peer_sub_keys, peer_u, peer_v, g_norm_final):
    assert g_norm_mix.shape[0] == 1, "single-layer step"
    bp, t, d = x_prompt.shape
    bs, ts, _ = x_sample.shape
    n_pages = page_table.shape[1]
    page_size = cache_kv_nsa.shape[2]
    past = n_pages * page_size
    wp = _prep_weights(w_in[0], w_phi_k[0], w_phi_v[0], w_pool_group[0], pool_scale[0], w_branch_attn[0],
                       w_branch_pool[0], w_out[0], peer_w_query[0], peer_sub_keys[0], peer_u[0], peer_v[0])
    g_mix = g_norm_mix[0].reshape(1, d)
    g_ffn = g_norm_ffn[0].reshape(1, d)
    g_fin = g_norm_final.reshape(1, d)

    xp = x_prompt.reshape(bp * t, d)
    cos, sin = _rope_tables(jnp.arange(t))
    cos, sin = jnp.tile(cos, (bp, 1)), jnp.tile(sin, (bp, 1))
    qq, nsa, win, gates, kvb = _qkv_proj(xp, g_mix, wp["w_qkv"], cos, sin)
    u = _norm_proj(xp, g_mix, wp["w_u"])
    gab = _norm_proj(xp, g_mix, wp["w_gab"], act="sigmoid")
    qq3, nsa3, kvb3 = qq.reshape(bp, t, -1), nsa.reshape(bp, t, -1), kvb.reshape(bp, t, -1)
    y = _compress_prompt(nsa3, wp["wk"], wp["wv"])
    o_cmp, sel = _cmp_prompt(qq3, y)
    o_sel = _sel_prompt(qq3, sel, kvb3)
    o_win = _win_prompt(qq3, kvb3)
    u3 = u.reshape(bp, t, -1)
    pool_out = _pool(u3, u3, wp["w_pool"], wp["pool_scale"], base=0, zero_first_prev=True)
    y_prompt = _token_tail(xp, o_cmp.reshape(bp * t, -1), o_sel.reshape(bp * t, -1), o_win.reshape(bp * t, -1),
                           gates, pool_out.reshape(bp * t, -1), gab, wp, g_ffn, g_fin)
    wlen = min(WINDOW, t)
    new_kv_p = nsa.reshape(1, bp, t, 4, KV_HEADS, HEAD_DIM)
    new_win_p = win.reshape(bp, t, 2, KV_HEADS, HEAD_DIM)[None, :, t - wlen:]
    new_pool_p = u3[None, :, t - (POOL_PREV - 1):]

    xs = x_sample.reshape(bs * ts, d)
    cos_s, sin_s = _rope_tables(past + jnp.arange(ts))
    cos_s, sin_s = jnp.tile(cos_s, (bs, 1)), jnp.tile(sin_s, (bs, 1))
    qq_s, nsa_s, win_s, gates_s, _ = _qkv_proj(xs, g_mix, wp["w_qkv"], cos_s, sin_s)
    u_s = _norm_proj(xs, g_mix, wp["w_u"])
    gab_s = _norm_proj(xs, g_mix, wp["w_gab"], act="sigmoid")
    qq_s3, nsa_s3, win_s3 = qq_s.reshape(bs, ts, -1), nsa_s.reshape(bs, ts, -1), win_s.reshape(bs, ts, -1)
    cache = cache_kv_nsa[0].reshape(-1, page_size, 4 * KV_HEADS * HEAD_DIM)
    y_s = _compress_pages(cache, page_table, wp["wk"], wp["wv"])
    o_cmp_s, sel_s = _cmp_sample(qq_s3, y_s, past=past)
    o_sel_s = _sel_sample(cache, page_table, qq_s3, sel_s, nsa_s3, past=past)
    st_win = state_win_kv[0].reshape(bs, -1, 2 * KV_HEADS * HEAD_DIM)
    o_win_s = _win_sample(qq_s3, st_win, win_s3)
    u_s3 = u_s.reshape(bs, ts, -1)
    st_pool = state_pool[0]
    prev = jnp.pad(st_pool, ((0, 0), (POOL_PREV - st_pool.shape[1], 0), (0, 0)))
    pool_out_s = _pool(prev, u_s3, wp["w_pool"], wp["pool_scale"], base=st_pool.shape[1], zero_first_prev=False)
    y_sample = _token_tail(xs, o_cmp_s.reshape(bs * ts, -1), o_sel_s.reshape(bs * ts, -1),
                           o_win_s.reshape(bs * ts, -1), gates_s, pool_out_s.reshape(bs * ts, -1), gab_s, wp,
                           g_ffn, g_fin)
    new_kv_s = nsa_s.reshape(1, bs, ts, 4, KV_HEADS, HEAD_DIM)
    win_ext = jnp.concatenate([st_win, win_s3], axis=1)
    wlen_s = min(WINDOW, win_ext.shape[1])
    new_win_s = win_ext[:, win_ext.shape[1] - wlen_s:].reshape(1, bs, wlen_s, 2, KV_HEADS, HEAD_DIM)
    pool_ext = jnp.concatenate([st_pool, u_s3], axis=1)
    new_pool_s = pool_ext[None, :, pool_ext.shape[1] - (POOL_PREV - 1):]

    return (y_prompt.reshape(bp, t, d), y_sample.reshape(bs, ts, d), new_kv_p, new_kv_s, new_win_p, new_win_s,
            new_pool_p, new_pool_s)
```

```python
import functools

import jax
import jax.numpy as jnp
import numpy as np
from jax import lax
from jax.experimental import pallas as pl
from jax.experimental.pallas import tpu as pltpu

F32 = jnp.float32
BF16 = jnp.bfloat16

HEAD_DIM = 128
N_HEADS = 8
KV_HEADS = 2
GROUP = N_HEADS // KV_HEADS
CMP_LEN = 32
CMP_STRIDE = 16
SEL_BLOCK = 64
SEL_TOPN = 16
CMP_PER_SEL = SEL_BLOCK // CMP_STRIDE
WINDOW = 512
Q_BLOCK = 128
ROPE_THETA = 10000.0
FORCE_SCORE = 1e4
POOL_WINDOWS = (2, 4, 8, 16)
POOL_PREV = 16
PEER_HEADS = 8
PEER_NKEYS = 128
PEER_TOPK = 16
EPS = 1e-6
SCALE = HEAD_DIM ** -0.5
NEG_INF = float("-inf")

VMEM_LIMIT = 56 * 1024 * 1024


def _params(*sem):
    return pltpu.CompilerParams(dimension_semantics=sem, vmem_limit_bytes=VMEM_LIMIT)


def _dot(a, b):
    return jnp.dot(a, b, preferred_element_type=F32)


def _dot_nt(a, b):
    return lax.dot_general(a, b, (((1,), (1,)), ((), ())), preferred_element_type=F32)


def _rms(x, g):
    return x * lax.rsqrt(jnp.mean(x * x, axis=-1, keepdims=True) + EPS) * g


def _masked_softmax_parts(parts, masks):
    parts = [jnp.where(mk, s, NEG_INF) for s, mk in zip(parts, masks)]
    m = functools.reduce(jnp.maximum, [jnp.max(s, axis=-1, keepdims=True) for s in parts])
    m = jnp.where(m == NEG_INF, 0.0, m)
    es = [jnp.exp(s - m) for s in parts]
    den = functools.reduce(jnp.add, [jnp.sum(e, axis=-1, keepdims=True) for e in es])
    inv = 1.0 / jnp.maximum(den, 1e-30)
    return [e * inv for e in es]


def _topk_select(v, n, axis):
    size = v.shape[axis]
    idx = lax.broadcasted_iota(jnp.int32, v.shape, axis).astype(F32)
    sel = jnp.zeros(v.shape, F32)
    for _ in range(n):
        m = jnp.max(v, axis=axis, keepdims=True)
        first = jnp.min(jnp.where(v == m, idx, float(size)), axis=axis, keepdims=True)
        hit = idx == first
        v = jnp.where(hit, NEG_INF, v)
        sel = jnp.where(hit, 1.0, sel)
    return sel


def _topk_sorted(v, n, payload=None):
    size = v.shape[0]
    idx = lax.broadcasted_iota(jnp.int32, v.shape, 0).astype(F32)
    vals, picks = [], []
    for _ in range(n):
        m = jnp.max(v, axis=0, keepdims=True)
        first = jnp.min(jnp.where(v == m, idx, float(size)), axis=0, keepdims=True)
        hit = idx == first
        vals.append(m)
        if payload is None:
            picks.append(first)
        else:
            picks.append(jnp.max(jnp.where(hit, payload, -1.0), axis=0, keepdims=True))
        v = jnp.where(hit, NEG_INF, v)
    return jnp.concatenate(vals, axis=0), jnp.concatenate(picks, axis=0)


def _proj_kernel(x_ref, g_ref, w_ref, o_ref, *, act, tn):
    h = _rms(x_ref[...], g_ref[...]).astype(BF16)
    for c in range(w_ref.shape[1] // tn):
        z = _dot(h, w_ref[:, c * tn:(c + 1) * tn])
        if act == "sigmoid":
            z = jax.nn.sigmoid(z)
        o_ref[:, c * tn:(c + 1) * tn] = z.astype(o_ref.dtype)


def _norm_proj(x, g, w, *, act=None, out_dtype=F32, tm=256, tn=512, wn=2048):
    m, d = x.shape
    n = w.shape[1]
    tm = min(tm, m)
    wn = min(wn, n)
    return pl.pallas_call(
        functools.partial(_proj_kernel, act=act, tn=tn),
        grid=(n // wn, m // tm),
        in_specs=[pl.BlockSpec((tm, d), lambda j, i: (i, 0)),
                  pl.BlockSpec((1, d), lambda j, i: (0, 0)),
                  pl.BlockSpec((d, wn), lambda j, i: (0, j))],
        out_specs=pl.BlockSpec((tm, wn), lambda j, i: (i, j)),
        out_shape=jax.ShapeDtypeStruct((m, n), out_dtype),
        compiler_params=_params("parallel", "parallel"),
        name="norm_proj",
    )(x, g, w)


QKV_COLS = N_HEADS * HEAD_DIM + 6 * KV_HEADS * HEAD_DIM + HEAD_DIM


def _qkv_kernel(x_ref, g_ref, w_ref, cos_ref, sin_ref, qq_ref, nsa_ref, win_ref, gate_ref, kvb_ref):
    h = _rms(x_ref[...], g_ref[...]).astype(BF16)
    cos = cos_ref[...]
    sin = sin_ref[...]

    def rope(z):
        return z * cos + pltpu.roll(z, HEAD_DIM // 2, 1) * sin

    qw = N_HEADS * HEAD_DIM
    for c in range(qw // 512):
        z = _dot(h, w_ref[:, c * 512:(c + 1) * 512])
        for j in range(4):
            zh = z[:, j * 128:(j + 1) * 128]
            col = c * 512 + j * 128
            qq_ref[:, col:col + 128] = zh.astype(BF16)
            qq_ref[:, qw + col:qw + col + 128] = rope(zh).astype(BF16)
    for c in range(3):
        z = _dot(h, w_ref[:, qw + c * 512:qw + (c + 1) * 512])
        if c == 0:
            nsa_ref[:, 0:512] = z
        else:
            for j in range(2):
                zr = rope(z[:, j * 128:(j + 1) * 128])
                zv = z[:, 256 + j * 128:256 + (j + 1) * 128]
                if c == 1:
                    nsa_ref[:, 512 + j * 128:512 + (j + 1) * 128] = zr
                    nsa_ref[:, 768 + j * 128:768 + (j + 1) * 128] = zv
                    kvb_ref[:, j * 128:(j + 1) * 128] = zr.astype(BF16)
                    kvb_ref[:, 256 + j * 128:256 + (j + 1) * 128] = zv.astype(BF16)
                else:
                    win_ref[:, j * 128:(j + 1) * 128] = zr
                    win_ref[:, 256 + j * 128:256 + (j + 1) * 128] = zv
                    kvb_ref[:, 512 + j * 128:512 + (j + 1) * 128] = zr.astype(BF16)
                    kvb_ref[:, 768 + j * 128:768 + (j + 1) * 128] = zv.astype(BF16)
    z = _dot(h, w_ref[:, qw + 1536:qw + 1536 + 128])
    gate_ref[...] = jax.nn.sigmoid(z)


def _qkv_proj(x, g, w, cos, sin, *, tm=256):
    m, d = x.shape
    tm = min(tm, m)
    row = lambda i: (i, 0)
    const = lambda i: (0, 0)
    return pl.pallas_call(
        _qkv_kernel,
        grid=(m // tm,),
        in_specs=[pl.BlockSpec((tm, d), row), pl.BlockSpec((1, d), const),
                  pl.BlockSpec((d, QKV_COLS), const),
                  pl.BlockSpec((tm, HEAD_DIM), row), pl.BlockSpec((tm, HEAD_DIM), row)],
        out_specs=[pl.BlockSpec((tm, 2048), row), pl.BlockSpec((tm, 1024), row),
                   pl.BlockSpec((tm, 512), row), pl.BlockSpec((tm, 128), row),
                   pl.BlockSpec((tm, 1024), row)],
        out_shape=[jax.ShapeDtypeStruct((m, 2048), BF16), jax.ShapeDtypeStruct((m, 1024), F32),
                   jax.ShapeDtypeStruct((m, 512), F32), jax.ShapeDtypeStruct((m, 128), F32),
                   jax.ShapeDtypeStruct((m, 1024), BF16)],
        compiler_params=_params("parallel"),
        name="qkv_proj",
    )(x, g, w, cos, sin)


ROW_W = 4 * KV_HEADS * HEAD_DIM
SUB_W = CMP_STRIDE * ROW_W


def _compress_rows(load, n_rows, wk_ref, wv_ref, y_ref):
    for kv in range(2):
        w_ref = wk_ref if kv == 0 else wv_ref
        for g in range(KV_HEADS):
            comp = kv * KV_HEADS + g
            acc = jnp.zeros((n_rows, 2 * HEAD_DIM), F32)
            for i in range(CMP_STRIDE):
                acc = acc + _dot(load(i, comp).astype(BF16), w_ref[i])
            y_ref[0, :, 2 * comp * HEAD_DIM:2 * (comp + 1) * HEAD_DIM] = acc


def _compress_prompt_kernel(x_ref, wk_ref, wv_ref, y_ref):
    load = lambda i, comp: x_ref[0, :, i * ROW_W + comp * HEAD_DIM:i * ROW_W + (comp + 1) * HEAD_DIM]
    _compress_rows(load, x_ref.shape[1], wk_ref, wv_ref, y_ref)


def _compress_prompt(nsa, wk, wv, *, ts=128):
    b, t, _ = nsa.shape
    n_sub = t // CMP_STRIDE
    ts = min(ts, n_sub)
    wspec = pl.BlockSpec((CMP_STRIDE, HEAD_DIM, 2 * HEAD_DIM), lambda bi, ci: (0, 0, 0))
    return pl.pallas_call(
        _compress_prompt_kernel,
        grid=(b, n_sub // ts),
        in_specs=[pl.BlockSpec((1, ts, SUB_W), lambda bi, ci: (bi, ci, 0)), wspec, wspec],
        out_specs=pl.BlockSpec((1, ts, 1024), lambda bi, ci: (bi, ci, 0)),
        out_shape=jax.ShapeDtypeStruct((b, n_sub, 1024), F32),
        compiler_params=_params("parallel", "parallel"),
        name="compress_prompt",
    )(nsa.reshape(b, n_sub, SUB_W), wk, wv)


PAGES_PER_STEP = 16


ROWS_PER_POS = 4 * KV_HEADS


def _compress_pages_kernel(pt_ref, *refs):
    pages = refs[:PAGES_PER_STEP]
    wk_ref, wv_ref, y_ref = refs[PAGES_PER_STEP:]
    per_page = pages[0].shape[0] // (ROWS_PER_POS * CMP_STRIDE)

    def load(i, comp):
        rows = pl.ds(i * ROWS_PER_POS + comp, per_page, stride=ROWS_PER_POS * CMP_STRIDE)
        return jnp.concatenate([p[rows, :] for p in pages], axis=0)

    _compress_rows(load, PAGES_PER_STEP * per_page, wk_ref, wv_ref, y_ref)


def _page_specs(page_rows):
    def spec(k):
        return pl.BlockSpec((page_rows, HEAD_DIM), lambda bi, ci, pt: (pt[bi, ci * PAGES_PER_STEP + k], 0))
    return [spec(k) for k in range(PAGES_PER_STEP)]


def _compress_pages(cache_rows, page_table, page_size, wk, wv):
    nb, n_pages = page_table.shape
    per_page = page_size // CMP_STRIDE
    rows = PAGES_PER_STEP * per_page
    const3 = lambda bi, ci, pt: (0, 0, 0)
    grid_spec = pltpu.PrefetchScalarGridSpec(
        num_scalar_prefetch=1,
        grid=(nb, n_pages // PAGES_PER_STEP),
        in_specs=_page_specs(page_size * ROWS_PER_POS) + [
            pl.BlockSpec((CMP_STRIDE, HEAD_DIM, 2 * HEAD_DIM), const3),
            pl.BlockSpec((CMP_STRIDE, HEAD_DIM, 2 * HEAD_DIM), const3)],
        out_specs=pl.BlockSpec((1, rows, 1024), lambda bi, ci, pt: (bi, ci, 0)),
    )
    return pl.pallas_call(
        _compress_pages_kernel,
        grid_spec=grid_spec,
        out_shape=jax.ShapeDtypeStruct((nb, n_pages * per_page, 1024), F32),
        compiler_params=_params("parallel", "parallel"),
        name="compress_pages",
    )(page_table, *([cache_rows] * PAGES_PER_STEP), wk, wv)


def _combine_compressed(y_ref, kcp_ref, tmp_ref):
    ns = y_ref.shape[1]
    nb = ns // CMP_PER_SEL
    last = lax.broadcasted_iota(jnp.int32, (ns, 1), 0) == ns - 1
    for a in range(2 * KV_HEADS):
        y1 = y_ref[0, :, a * 256:a * 256 + 128]
        y2 = y_ref[0, :, a * 256 + 128:(a + 1) * 256]
        nxt = jnp.where(last, 0.0, pltpu.roll(y2, ns - 1, 0))
        tmp_ref[...] = y1 + nxt
        for c in range(CMP_PER_SEL):
            kcp_ref[a * CMP_PER_SEL + c] = tmp_ref[pl.ds(c, nb, stride=CMP_PER_SEL), :].astype(BF16)


def _cmp_attention(q_rows, pos, kcp_ref, g, rows_per_tok):
    nb = kcp_ref.shape[1]
    blk = lax.broadcasted_iota(jnp.int32, (1, nb), 1)
    masks = [(SEL_BLOCK * blk + CMP_STRIDE * c + CMP_LEN - 1) <= pos for c in range(CMP_PER_SEL)]
    outs = []
    imp = None
    for qh in q_rows:
        s = [_dot_nt(qh, kcp_ref[g * CMP_PER_SEL + c]) * SCALE for c in range(CMP_PER_SEL)]
        p = _masked_softmax_parts(s, masks)
        o = functools.reduce(jnp.add, [
            _dot(p[c].astype(BF16), kcp_ref[(KV_HEADS + g) * CMP_PER_SEL + c]) for c in range(CMP_PER_SEL)])
        outs.append(o)
        ps = functools.reduce(jnp.add, p)
        imp = ps if imp is None else imp + ps
    return outs, imp


def _cmp_prompt_kernel(q_ref, y_ref, o_ref, sel_ref, kcp_ref, tmp_ref):
    i = pl.program_id(1)

    @pl.when(i == 0)
    def _():
        _combine_compressed(y_ref, kcp_ref, tmp_ref)

    nb = kcp_ref.shape[1]
    pos = i * Q_BLOCK + lax.broadcasted_iota(jnp.int32, (Q_BLOCK, 1), 0)
    blk = lax.broadcasted_iota(jnp.int32, (1, nb), 1)
    cur = pos // SEL_BLOCK
    forced = (blk == cur) | (blk == 0)
    causal = blk <= cur
    for g in range(KV_HEADS):
        q_rows = [q_ref[0, :, (g * GROUP + r) * HEAD_DIM:(g * GROUP + r + 1) * HEAD_DIM] for r in range(GROUP)]
        outs, imp = _cmp_attention(q_rows, pos, kcp_ref, g, 1)
        for r in range(GROUP):
            hcol = (g * GROUP + r) * HEAD_DIM
            o_ref[0, :, hcol:hcol + HEAD_DIM] = outs[r]
        v = jnp.where(forced, FORCE_SCORE, jnp.where(causal, imp, -1.0))
        sel_t = _topk_select(v.T, SEL_TOPN, 0)
        sel = jnp.where(causal, sel_t.T, 0.0)
        sel_ref[0, :, g * nb:(g + 1) * nb] = sel.astype(BF16)


def _cmp_prompt(qq3, y):
    b, t, _ = qq3.shape
    ns = y.shape[1]
    nb = ns // CMP_PER_SEL
    return pl.pallas_call(
        _cmp_prompt_kernel,
        grid=(b, t // Q_BLOCK),
        in_specs=[pl.BlockSpec((1, Q_BLOCK, 1024), lambda bi, i: (bi, i, 0)),
                  pl.BlockSpec((1, ns, 1024), lambda bi, i: (bi, 0, 0))],
        out_specs=[pl.BlockSpec((1, Q_BLOCK, 1024), lambda bi, i: (bi, i, 0)),
                   pl.BlockSpec((1, Q_BLOCK, KV_HEADS * nb), lambda bi, i: (bi, i, 0))],
        out_shape=[jax.ShapeDtypeStruct((b, t, 1024), F32),
                   jax.ShapeDtypeStruct((b, t, KV_HEADS * nb), BF16)],
        scratch_shapes=[pltpu.VMEM((2 * KV_HEADS * CMP_PER_SEL, nb, HEAD_DIM), BF16),
                        pltpu.VMEM((ns, HEAD_DIM), F32)],
        compiler_params=_params("parallel", "arbitrary"),
        name="cmp_prompt",
    )(qq3, y)


SEL_CHUNK = 512


def _expand_blocks(sel, first_block, width):
    nb = sel.shape[1]
    b = lax.broadcasted_iota(jnp.int32, (nb, width), 0)
    t = lax.broadcasted_iota(jnp.int32, (nb, width), 1)
    e = jnp.where(b == first_block + t // SEL_BLOCK, 1.0, 0.0).astype(BF16)
    return _dot(sel, e)


def _sel_prompt_kernel(q_ref, sel_ref, kv_ref, o_ref):
    i = pl.program_id(1)
    nb = sel_ref.shape[2] // KV_HEADS
    pos = i * Q_BLOCK + lax.broadcasted_iota(jnp.int32, (Q_BLOCK, 1), 0)
    pos = jnp.concatenate([pos] * GROUP, axis=0)
    n_chunks = (i * Q_BLOCK + Q_BLOCK + SEL_CHUNK - 1) // SEL_CHUNK
    lane = lax.broadcasted_iota(jnp.int32, (1, SEL_CHUNK), 1)
    rows = GROUP * Q_BLOCK
    for g in range(KV_HEADS):
        q = jnp.concatenate(
            [q_ref[0, :, (g * GROUP + r) * HEAD_DIM:(g * GROUP + r + 1) * HEAD_DIM] for r in range(GROUP)], axis=0)
        sel = sel_ref[0, :, g * nb:(g + 1) * nb]

        def body(c, carry):
            m, l, acc = carry
            start = pl.multiple_of(c * SEL_CHUNK, SEL_CHUNK)
            k = kv_ref[0, pl.ds(start, SEL_CHUNK), g * HEAD_DIM:(g + 1) * HEAD_DIM]
            v = kv_ref[0, pl.ds(start, SEL_CHUNK), (KV_HEADS + g) * HEAD_DIM:(KV_HEADS + g + 1) * HEAD_DIM]
            s = _dot_nt(q, k) * SCALE
            picked = _expand_blocks(sel, c * (SEL_CHUNK // SEL_BLOCK), SEL_CHUNK)
            picked = jnp.concatenate([picked] * GROUP, axis=0)
            ok = (picked > 0.5) & ((start + lane) <= pos)
            s = jnp.where(ok, s, NEG_INF)
            m_new = jnp.maximum(m, jnp.max(s, axis=-1, keepdims=True))
            m_safe = jnp.where(m_new == NEG_INF, 0.0, m_new)
            p = jnp.exp(s - m_safe)
            alpha = jnp.exp(m - m_safe)
            l = alpha * l + jnp.sum(p, axis=-1, keepdims=True)
            acc = alpha * acc + _dot(p.astype(BF16), v)
            return m_new, l, acc

        init = (jnp.full((rows, 1), NEG_INF, F32), jnp.zeros((rows, 1), F32), jnp.zeros((rows, HEAD_DIM), F32))
        m, l, acc = lax.fori_loop(0, n_chunks, body, init)
        o = acc * (1.0 / jnp.maximum(l, 1e-30))
        for r in range(GROUP):
            hcol = (g * GROUP + r) * HEAD_DIM
            o_ref[0, :, hcol:hcol + HEAD_DIM] = o[r * Q_BLOCK:(r + 1) * Q_BLOCK]


def _sel_prompt(qq3, sel, kvb3):
    b, t, _ = qq3.shape
    nb2 = sel.shape[2]
    return pl.pallas_call(
        _sel_prompt_kernel,
        grid=(b, t // Q_BLOCK),
        in_specs=[pl.BlockSpec((1, Q_BLOCK, 1024), lambda bi, i: (bi, i, 1)),
                  pl.BlockSpec((1, Q_BLOCK, nb2), lambda bi, i: (bi, i, 0)),
                  pl.BlockSpec((1, t, 512), lambda bi, i: (bi, 0, 0))],
        out_specs=pl.BlockSpec((1, Q_BLOCK, 1024), lambda bi, i: (bi, i, 0)),
        out_shape=jax.ShapeDtypeStruct((b, t, 1024), F32),
        compiler_params=_params("parallel", "arbitrary"),
        name="sel_prompt",
    )(qq3, sel, kvb3)


def _win_prompt_kernel(q_ref, kv_ref, o_ref, *, span):
    i = pl.program_id(1)
    pos = i * Q_BLOCK + lax.broadcasted_iota(jnp.int32, (Q_BLOCK, 1), 0)
    pos = jnp.concatenate([pos] * GROUP, axis=0)
    start = pl.multiple_of(jnp.maximum(i * Q_BLOCK + Q_BLOCK - span, 0), Q_BLOCK)
    kpos = start + lax.broadcasted_iota(jnp.int32, (1, span), 1)
    diff = pos - kpos
    ok = (diff >= 0) & (diff < WINDOW)
    for g in range(KV_HEADS):
        q = jnp.concatenate(
            [q_ref[0, :, (g * GROUP + r) * HEAD_DIM:(g * GROUP + r + 1) * HEAD_DIM] for r in range(GROUP)], axis=0)
        k = kv_ref[0, pl.ds(start, span), g * HEAD_DIM:(g + 1) * HEAD_DIM]
        v = kv_ref[0, pl.ds(start, span), (KV_HEADS + g) * HEAD_DIM:(KV_HEADS + g + 1) * HEAD_DIM]
        s = _dot_nt(q, k) * SCALE
        (p,) = _masked_softmax_parts([s], [ok])
        o = _dot(p.astype(BF16), v)
        for r in range(GROUP):
            hcol = (g * GROUP + r) * HEAD_DIM
            o_ref[0, :, hcol:hcol + HEAD_DIM] = o[r * Q_BLOCK:(r + 1) * Q_BLOCK]


def _win_prompt(qq3, kvb3):
    b, t, _ = qq3.shape
    span = min(WINDOW + Q_BLOCK, t)
    return pl.pallas_call(
        functools.partial(_win_prompt_kernel, span=span),
        grid=(b, t // Q_BLOCK),
        in_specs=[pl.BlockSpec((1, Q_BLOCK, 1024), lambda bi, i: (bi, i, 1)),
                  pl.BlockSpec((1, t, 512), lambda bi, i: (bi, 0, 1))],
        out_specs=pl.BlockSpec((1, Q_BLOCK, 1024), lambda bi, i: (bi, i, 0)),
        out_shape=jax.ShapeDtypeStruct((b, t, 1024), F32),
        compiler_params=_params("parallel", "arbitrary"),
        name="win_prompt",
    )(qq3, kvb3)


def _pool_kernel(prev_ref, u_ref, w_ref, sc_ref, o_ref, *, base, zero_first_prev):
    i = pl.program_id(1)
    tq = u_ref.shape[1]
    cur = u_ref[0]
    prev = prev_ref[0]
    if zero_first_prev:
        prev = jnp.where(i == 0, 0.0, prev)
    ext = jnp.concatenate([prev, cur], axis=0)
    gpos = base + i * tq + lax.broadcasted_iota(jnp.int32, (tq, 1), 0)
    gw = ext.shape[1] // len(POOL_WINDOWS)
    for gi, w in enumerate(POOL_WINDOWS):
        s = ext[:, gi * gw:(gi + 1) * gw]
        span = 1
        while span < w:
            s = s + pltpu.roll(s, span, 0)
            span *= 2
        cnt = jnp.minimum(gpos + 1, w).astype(F32)
        d = s[POOL_PREV:] / cnt - cur[:, gi * gw:(gi + 1) * gw]
        o = _dot(d.astype(BF16), w_ref[gi]) * sc_ref[:, gi * gw:(gi + 1) * gw]
        o_ref[0, :, gi * gw:(gi + 1) * gw] = o


def _pool(prev, u3, w, scale, *, base, zero_first_prev, tq=512):
    b, t, c = u3.shape
    tq = min(tq, t)
    ratio = tq // POOL_PREV
    if zero_first_prev:
        prev_map = lambda bi, i: (bi, jnp.maximum(i * ratio - 1, 0), 0)
    else:
        prev_map = lambda bi, i: (bi, 0, 0)
    ng = len(POOL_WINDOWS)
    return pl.pallas_call(
        functools.partial(_pool_kernel, base=base, zero_first_prev=zero_first_prev),
        grid=(b, t // tq),
        in_specs=[pl.BlockSpec((1, POOL_PREV, c), prev_map),
                  pl.BlockSpec((1, tq, c), lambda bi, i: (bi, i, 0)),
                  pl.BlockSpec((ng, c // ng, c // ng), lambda bi, i: (0, 0, 0)),
                  pl.BlockSpec((1, c), lambda bi, i: (0, 0))],
        out_specs=pl.BlockSpec((1, tq, c), lambda bi, i: (bi, i, 0)),
        out_shape=jax.ShapeDtypeStruct((b, t, c), F32),
        compiler_params=_params("parallel", "parallel"),
        name="pool_mix",
    )(prev, u3, w, scale)


def _mix_kernel(oc_ref, os_ref, ow_ref, gt_ref, po_ref, ga_ref, gb_ref, wa_ref, wp_ref, mix_ref):
    gt = gt_ref[...]
    cols = []
    for h in range(N_HEADS):
        sl = slice(h * HEAD_DIM, (h + 1) * HEAD_DIM)
        o = (gt[:, h:h + 1] * oc_ref[:, sl] + gt[:, N_HEADS + h:N_HEADS + h + 1] * os_ref[:, sl]
             + gt[:, 2 * N_HEADS + h:2 * N_HEADS + h + 1] * ow_ref[:, sl])
        cols.append(o.astype(BF16))
    a = _dot(jnp.concatenate(cols, axis=1), wa_ref[...])
    p = _dot(po_ref[...].astype(BF16), wp_ref[...])
    mix_ref[...] = (ga_ref[...] * a + gb_ref[...] * p).astype(BF16)


def _mix(oc, os_, ow, gt, po, gab, wa, wp, *, tm=256):
    m = oc.shape[0]
    d = wa.shape[1]
    tm = min(tm, m)
    row = lambda i: (i, 0)
    const = lambda i: (0, 0)
    return pl.pallas_call(
        _mix_kernel,
        grid=(m // tm,),
        in_specs=[pl.BlockSpec((tm, 1024), row), pl.BlockSpec((tm, 1024), row), pl.BlockSpec((tm, 1024), row),
                  pl.BlockSpec((tm, 128), row), pl.BlockSpec((tm, 1024), row),
                  pl.BlockSpec((tm, d), lambda i: (i, 0)), pl.BlockSpec((tm, d), lambda i: (i, 1)),
                  pl.BlockSpec((1024, d), const), pl.BlockSpec((1024, d), const)],
        out_specs=pl.BlockSpec((tm, d), row),
        out_shape=jax.ShapeDtypeStruct((m, d), BF16),
        compiler_params=_params("parallel"),
        name="branch_mix",
    )(oc, os_, ow, gt, po, gab, gab, wa, wp)


def _out_kernel(x_ref, mix_ref, w_ref, o_ref):
    o_ref[...] = x_ref[...] + _dot(mix_ref[...], w_ref[...])


def _out_proj(x, mix, w, *, tm=256):
    m, d = x.shape
    tm = min(tm, m)
    return pl.pallas_call(
        _out_kernel,
        grid=(m // tm,),
        in_specs=[pl.BlockSpec((tm, d), lambda i: (i, 0)), pl.BlockSpec((tm, d), lambda i: (i, 0)),
                  pl.BlockSpec((d, d), lambda i: (0, 0))],
        out_specs=pl.BlockSpec((tm, d), lambda i: (i, 0)),
        out_shape=jax.ShapeDtypeStruct((m, d), F32),
        compiler_params=_params("parallel"),
        name="out_proj",
    )(x, mix, w)


def _peer_score_kernel(x_ref, g_ref, wq_ref, keys_ref, ids_ref, gw_ref):
    h = _rms(x_ref[...], g_ref[...]).astype(BF16)
    q = _dot(h, wq_ref[...]).astype(BF16)
    dk = PEER_NKEYS
    ids, gws = [], []
    for hd in range(PEER_HEADS):
        sv, si = [], []
        for c in range(2):
            qhc = q[:, (hd * 2 + c) * dk:(hd * 2 + c + 1) * dk]
            st = _dot_nt(keys_ref[hd * 2 + c], qhc)
            v, ix = _topk_sorted(st, PEER_TOPK)
            sv.append(v)
            si.append(ix)
        comb = jnp.concatenate([sv[0][a:a + 1] + sv[1] for a in range(PEER_TOPK)], axis=0)
        eid = jnp.concatenate([si[0][a:a + 1] * float(PEER_NKEYS) + si[1] for a in range(PEER_TOPK)], axis=0)
        cv, ce = _topk_sorted(comb, PEER_TOPK, payload=eid)
        e = jnp.exp(cv - cv[0:1])
        gws.append(e / jnp.sum(e, axis=0, keepdims=True))
        ids.append(ce)
    ids_ref[...] = jnp.concatenate(ids, axis=0).T.astype(jnp.int32)
    gw_ref[...] = jnp.concatenate(gws, axis=0)


def _peer_score(x, g, wq, keys, *, tb=128):
    m, d = x.shape
    nk = PEER_HEADS * PEER_TOPK
    return pl.pallas_call(
        _peer_score_kernel,
        grid=(m // tb,),
        in_specs=[pl.BlockSpec((tb, d), lambda i: (i, 0)), pl.BlockSpec((1, d), lambda i: (0, 0)),
                  pl.BlockSpec(wq.shape, lambda i: (0, 0)),
                  pl.BlockSpec(keys.shape, lambda i: (0, 0, 0))],
        out_specs=[pl.BlockSpec((tb, nk), lambda i: (i, 0)), pl.BlockSpec((nk, tb), lambda i: (0, i))],
        out_shape=[jax.ShapeDtypeStruct((m, nk), jnp.int32), jax.ShapeDtypeStruct((nk, m), F32)],
        compiler_params=_params("parallel"),
        name="peer_score",
    )(x, g, wq, keys)


def _gelu(x):
    return 0.5 * x * (1.0 + lax.erf(x * (2.0 ** -0.5)))


PEER_RING = 8


def _peer_apply_kernel(ids_ref, x_ref, gffn_ref, gw_ref, gfin_ref, tab_ref, y_ref, *scratch):
    bufs = scratch[:PEER_RING]
    sem, hbuf, obuf = scratch[PEER_RING:]
    tb, d = x_ref.shape
    nk = gw_ref.shape[0]
    nchunk = d // 128
    ahead = PEER_RING - 1
    hbuf[...] = _rms(x_ref[...], gffn_ref[...])

    def row_copy(t, k, slot):
        return pltpu.make_async_copy(tab_ref.at[ids_ref[t, k]], bufs[slot].at[pl.ds(k, 1), :], sem.at[slot])

    def issue(t, slot):
        for k in range(nk):
            row_copy(t, k, slot).start()

    def drain(t, slot):
        for k in range(nk):
            row_copy(t, k, slot).wait()

    for s in range(ahead):
        issue(s, s)
    lane = lax.broadcasted_iota(jnp.int32, (nk, tb), 1)

    def compute(t, slot):
        buf = bufs[slot]
        hrow = hbuf[pl.ds(t, 1), :]
        acc = jnp.zeros((nk, 128), F32)
        for j in range(nchunk):
            w = buf[:, j * 128:(j + 1) * 128]
            u = pltpu.bitcast(w << 16, F32)
            acc = acc + u * hrow[:, j * 128:(j + 1) * 128]
        act = jnp.sum(acc, axis=1, keepdims=True)
        gcol = jnp.sum(jnp.where(lane == t, gw_ref[...], 0.0), axis=1, keepdims=True)
        coef = _gelu(act) * gcol
        outs = []
        for j in range(nchunk):
            w = buf[:, j * 128:(j + 1) * 128]
            v = pltpu.bitcast(w & jnp.uint32(0xFFFF0000), F32)
            outs.append(jnp.sum(v * coef, axis=0, keepdims=True))
        obuf[pl.ds(t, 1), :] = jnp.concatenate(outs, axis=1)

    def body(p, carry):
        for s in range(PEER_RING):
            t = p * PEER_RING + s
            drain(t, s)
            issue(jnp.minimum(t + ahead, tb - 1), (s + ahead) % PEER_RING)
            compute(t, s)
        return carry

    lax.fori_loop(0, tb // PEER_RING, body, 0)
    for s in range(ahead):
        drain(tb - 1, (tb + s) % PEER_RING)
    y_ref[...] = _rms(x_ref[...] + obuf[...], gfin_ref[...])


def _peer_apply(ids, x, gffn, gw, gfin, table, *, tb=128):
    m, d = x.shape
    nk = ids.shape[1]
    return pl.pallas_call(
        _peer_apply_kernel,
        grid=(m // tb,),
        in_specs=[pl.BlockSpec((tb, nk), lambda i: (i, 0), memory_space=pltpu.SMEM),
                  pl.BlockSpec((tb, d), lambda i: (i, 0)),
                  pl.BlockSpec((1, d), lambda i: (0, 0)),
                  pl.BlockSpec((nk, tb), lambda i: (0, i)),
                  pl.BlockSpec((1, d), lambda i: (0, 0)),
                  pl.BlockSpec(memory_space=pl.ANY)],
        out_specs=pl.BlockSpec((tb, d), lambda i: (i, 0)),
        out_shape=jax.ShapeDtypeStruct((m, d), F32),
        scratch_shapes=[pltpu.VMEM((nk, d), jnp.uint32)] * PEER_RING + [
            pltpu.SemaphoreType.DMA((PEER_RING,)), pltpu.VMEM((tb, d), F32), pltpu.VMEM((tb, d), F32)],
        compiler_params=_params("arbitrary"),
        name="peer_apply",
    )(ids, x, gffn, gw, gfin, table)


def _stack_heads(q_ref, g):
    return jnp.concatenate(
        [q_ref[0, :, (g * GROUP + r) * HEAD_DIM:(g * GROUP + r + 1) * HEAD_DIM] for r in range(GROUP)], axis=0)


NEW_PAD = 128


def _pad_new(x):
    return jnp.concatenate([x, jnp.zeros((NEW_PAD - x.shape[0], x.shape[1]), F32)], axis=0).astype(BF16)


def _cmp_sample_kernel(q_ref, y_ref, o_ref, sel_ref, kcp_ref, tmp_ref, *, past):
    _combine_compressed(y_ref, kcp_ref, tmp_ref)
    t = q_ref.shape[1]
    nb = kcp_ref.shape[1]
    tpos = past + lax.broadcasted_iota(jnp.int32, (t, 1), 0)
    pos = jnp.concatenate([tpos] * GROUP, axis=0)
    blk = lax.broadcasted_iota(jnp.int32, (1, nb), 1)
    for g in range(KV_HEADS):
        q = _stack_heads(q_ref, g)
        outs, imp = _cmp_attention([q], pos, kcp_ref, g, t)
        o = outs[0]
        for r in range(GROUP):
            hcol = (g * GROUP + r) * HEAD_DIM
            o_ref[0, :, hcol:hcol + HEAD_DIM] = o[r * t:(r + 1) * t]
        imp_t = functools.reduce(jnp.add, [imp[r * t:(r + 1) * t] for r in range(GROUP)])
        v = jnp.where(blk == 0, FORCE_SCORE, imp_t)
        sel = _topk_select(v, SEL_TOPN - 1, 1)
        sel_ref[0, :, g * nb:(g + 1) * nb] = sel.astype(BF16)


def _cmp_sample(qq3, y, *, past):
    b, t, _ = qq3.shape
    ns = y.shape[1]
    nb = ns // CMP_PER_SEL
    return pl.pallas_call(
        functools.partial(_cmp_sample_kernel, past=past),
        grid=(b,),
        in_specs=[pl.BlockSpec((1, t, 1024), lambda bi: (bi, 0, 0)),
                  pl.BlockSpec((1, ns, 1024), lambda bi: (bi, 0, 0))],
        out_specs=[pl.BlockSpec((1, t, 1024), lambda bi: (bi, 0, 0)),
                   pl.BlockSpec((1, t, KV_HEADS * nb), lambda bi: (bi, 0, 0))],
        out_shape=[jax.ShapeDtypeStruct((b, t, 1024), F32),
                   jax.ShapeDtypeStruct((b, t, KV_HEADS * nb), BF16)],
        scratch_shapes=[pltpu.VMEM((2 * KV_HEADS * CMP_PER_SEL, nb, HEAD_DIM), BF16),
                        pltpu.VMEM((ns, HEAD_DIM), F32)],
        compiler_params=_params("parallel"),
        name="cmp_sample",
    )(qq3, y)


def _sel_sample_kernel(pt_ref, *refs, past):
    pages = refs[:PAGES_PER_STEP]
    q_ref, sel_ref, new_ref, o_ref, m_ref, l_ref, acc_ref = refs[PAGES_PER_STEP:]
    c = pl.program_id(1)
    nc = pl.num_programs(1)
    t = q_ref.shape[1]
    rows = GROUP * t
    nb = sel_ref.shape[2] // KV_HEADS
    page = pages[0].shape[0] // ROWS_PER_POS
    width = PAGES_PER_STEP * page

    def component(slot, g):
        rows = pl.ds(slot * KV_HEADS + g, page, stride=ROWS_PER_POS)
        return jnp.concatenate([p[rows, :] for p in pages], axis=0).astype(BF16)

    @pl.when(c == 0)
    def _():
        m_ref[...] = jnp.full(m_ref.shape, NEG_INF, F32)
        l_ref[...] = jnp.zeros(l_ref.shape, F32)
        acc_ref[...] = jnp.zeros(acc_ref.shape, F32)

    def update(g, s, ok, v):
        s = jnp.where(ok, s, NEG_INF)
        m = m_ref[g]
        m_new = jnp.maximum(m, jnp.max(s, axis=-1, keepdims=True))
        m_safe = jnp.where(m_new == NEG_INF, 0.0, m_new)
        p = jnp.exp(s - m_safe)
        alpha = jnp.exp(m - m_safe)
        l_ref[g] = alpha * l_ref[g] + jnp.sum(p, axis=-1, keepdims=True)
        acc_ref[g] = alpha * acc_ref[g] + _dot(p.astype(BF16), v)
        m_ref[g] = m_new

    for g in range(KV_HEADS):
        q = _stack_heads(q_ref, g)
        k = component(2, g)
        v = component(3, g)
        s = _dot_nt(q, k) * SCALE
        picked = _expand_blocks(sel_ref[0, :, g * nb:(g + 1) * nb], c * (width // SEL_BLOCK), width)
        picked = jnp.concatenate([picked] * GROUP, axis=0)
        update(g, s, picked > 0.5, v)

    @pl.when(c == nc - 1)
    def _():
        qi = lax.broadcasted_iota(jnp.int32, (t, 1), 0)
        qi = jnp.concatenate([qi] * GROUP, axis=0)
        ok = lax.broadcasted_iota(jnp.int32, (1, NEW_PAD), 1) <= qi
        for g in range(KV_HEADS):
            q = _stack_heads(q_ref, g)
            k = _pad_new(new_ref[0, :, 512 + g * HEAD_DIM:512 + (g + 1) * HEAD_DIM])
            v = _pad_new(new_ref[0, :, 768 + g * HEAD_DIM:768 + (g + 1) * HEAD_DIM])
            update(g, _dot_nt(q, k) * SCALE, ok, v)
            o = acc_ref[g] * (1.0 / jnp.maximum(l_ref[g], 1e-30))
            for r in range(GROUP):
                hcol = (g * GROUP + r) * HEAD_DIM
                o_ref[0, :, hcol:hcol + HEAD_DIM] = o[r * t:(r + 1) * t]


def _sel_sample(cache_rows, page_table, page_size, qq3, sel, nsa3, *, past):
    nbatch, n_pages = page_table.shape
    t = qq3.shape[1]
    nb2 = sel.shape[2]
    rows = GROUP * t
    grid_spec = pltpu.PrefetchScalarGridSpec(
        num_scalar_prefetch=1,
        grid=(nbatch, n_pages // PAGES_PER_STEP),
        in_specs=_page_specs(page_size * ROWS_PER_POS) + [
            pl.BlockSpec((1, t, 1024), lambda bi, ci, pt: (bi, 0, 1)),
            pl.BlockSpec((1, t, nb2), lambda bi, ci, pt: (bi, 0, 0)),
            pl.BlockSpec((1, t, 1024), lambda bi, ci, pt: (bi, 0, 0))],
        out_specs=pl.BlockSpec((1, t, 1024), lambda bi, ci, pt: (bi, 0, 0)),
        scratch_shapes=[pltpu.VMEM((KV_HEADS, rows, 1), F32), pltpu.VMEM((KV_HEADS, rows, 1), F32),
                        pltpu.VMEM((KV_HEADS, rows, HEAD_DIM), F32)],
    )
    return pl.pallas_call(
        functools.partial(_sel_sample_kernel, past=past),
        grid_spec=grid_spec,
        out_shape=jax.ShapeDtypeStruct((nbatch, t, 1024), F32),
        compiler_params=_params("parallel", "arbitrary"),
        name="sel_sample",
    )(page_table, *([cache_rows] * PAGES_PER_STEP), qq3, sel, nsa3)


def _win_sample_kernel(q_ref, st_ref, new_ref, o_ref):
    t = q_ref.shape[1]
    nw = st_ref.shape[1]
    qi = lax.broadcasted_iota(jnp.int32, (t, 1), 0)
    qi = jnp.concatenate([qi] * GROUP, axis=0)
    d_old = (nw + qi) - lax.broadcasted_iota(jnp.int32, (1, nw), 1)
    ok_old = (d_old >= 0) & (d_old < WINDOW)
    d_new = qi - lax.broadcasted_iota(jnp.int32, (1, NEW_PAD), 1)
    ok_new = (d_new >= 0) & (d_new < WINDOW)
    for g in range(KV_HEADS):
        q = _stack_heads(q_ref, g)
        k_old = st_ref[0, :, g * HEAD_DIM:(g + 1) * HEAD_DIM].astype(BF16)
        v_old = st_ref[0, :, (KV_HEADS + g) * HEAD_DIM:(KV_HEADS + g + 1) * HEAD_DIM].astype(BF16)
        k_new = _pad_new(new_ref[0, :, g * HEAD_DIM:(g + 1) * HEAD_DIM])
        v_new = _pad_new(new_ref[0, :, (KV_HEADS + g) * HEAD_DIM:(KV_HEADS + g + 1) * HEAD_DIM])
        p_old, p_new = _masked_softmax_parts(
            [_dot_nt(q, k_old) * SCALE, _dot_nt(q, k_new) * SCALE], [ok_old, ok_new])
        o = _dot(p_old.astype(BF16), v_old) + _dot(p_new.astype(BF16), v_new)
        for r in range(GROUP):
            hcol = (g * GROUP + r) * HEAD_DIM
            o_ref[0, :, hcol:hcol + HEAD_DIM] = o[r * t:(r + 1) * t]


def _win_sample(qq3, state_win, win3):
    b, t, _ = qq3.shape
    nw = state_win.shape[1]
    return pl.pallas_call(
        _win_sample_kernel,
        grid=(b,),
        in_specs=[pl.BlockSpec((1, t, 1024), lambda bi: (bi, 0, 1)),
                  pl.BlockSpec((1, nw, 512), lambda bi: (bi, 0, 0)),
                  pl.BlockSpec((1, t, 512), lambda bi: (bi, 0, 0))],
        out_specs=pl.BlockSpec((1, t, 1024), lambda bi: (bi, 0, 0)),
        out_shape=jax.ShapeDtypeStruct((b, t, 1024), F32),
        compiler_params=_params("parallel"),
        name="win_sample",
    )(qq3, state_win, win3)


def _rope_tables(pos):
    half = HEAD_DIM // 2
    inv = ROPE_THETA ** (-jnp.arange(half, dtype=F32) / half)
    ang = pos.astype(F32)[:, None] * inv[None, :]
    cos, sin = jnp.cos(ang), jnp.sin(ang)
    return jnp.concatenate([cos, cos], axis=-1), jnp.concatenate([-sin, sin], axis=-1)


def _prep_weights(w_in, w_phi_k, w_phi_v, w_pool_group, pool_scale, w_branch_attn, w_branch_pool, w_out,
                  peer_w_query, peer_sub_keys, peer_u, peer_v):
    d = w_in.shape[0]
    qw = N_HEADS * HEAD_DIM
    kvw = 6 * KV_HEADS * HEAD_DIM
    ngw = 3 * N_HEADS
    pw = d // 2
    o1, o2, o3, o4 = qw, qw + kvw, qw + kvw + ngw, qw + kvw + ngw + pw
    wb = w_in.astype(BF16)
    w_qkv = jnp.concatenate([wb[:, :o2], wb[:, o2:o3], jnp.zeros((d, HEAD_DIM - ngw), BF16)], axis=1)
    cat = lambda w: jnp.concatenate([w[:CMP_STRIDE], w[CMP_STRIDE:]], axis=-1).astype(BF16)
    u16 = lax.bitcast_convert_type(peer_u.astype(BF16), jnp.uint16).astype(jnp.uint32)
    v16 = lax.bitcast_convert_type(peer_v.astype(BF16), jnp.uint16).astype(jnp.uint32)
    return dict(
        w_qkv=w_qkv, w_u=wb[:, o3:o4], w_gab=wb[:, o4:],
        wk=cat(w_phi_k), wv=cat(w_phi_v),
        w_pool=w_pool_group.astype(BF16), pool_scale=pool_scale.reshape(1, -1),
        w_ba=w_branch_attn.astype(BF16), w_bp=w_branch_pool.astype(BF16), w_out=w_out.astype(BF16),
        w_query=peer_w_query.astype(BF16),
        keys=peer_sub_keys.reshape(PEER_HEADS * 2, PEER_NKEYS, -1).astype(BF16),
        table=(u16 | (v16 << 16))[:, None, :],
    )


def _token_tail(x2d, o_cmp, o_sel, o_win, gates, pool_out, gab, wp, g_ffn, g_final):
    mix = _mix(o_cmp, o_sel, o_win, gates, pool_out, gab, wp["w_ba"], wp["w_bp"])
    x2 = _out_proj(x2d, mix, wp["w_out"])
    ids, gw = _peer_score(x2, g_ffn, wp["w_query"], wp["keys"])
    return _peer_apply(ids, x2, g_ffn, gw, g_final, wp["table"])


def kernel(x_prompt, x_sample, cache_kv_nsa, state_win_kv, state_pool, page_table, g_norm_mix, w_in, w_phi_k,
           w_phi_v, w_pool_group, pool_scale, w_branch_attn, w_branch_pool, w_out, g_norm_ffn, peer_w_query,
           peer_sub_keys, peer_u, peer_v, g_norm_final):
    assert g_norm_mix.shape[0] == 1, "single-layer step"
    bp, t, d = x_prompt.shape
    bs, ts, _ = x_sample.shape
    n_pages = page_table.shape[1]
    page_size = cache_kv_nsa.shape[2]
    past = n_pages * page_size
    wp = _prep_weights(w_in[0], w_phi_k[0], w_phi_v[0], w_pool_group[0], pool_scale[0], w_branch_attn[0],
                       w_branch_pool[0], w_out[0], peer_w_query[0], peer_sub_keys[0], peer_u[0], peer_v[0])
    g_mix = g_norm_mix[0].reshape(1, d)
    g_ffn = g_norm_ffn[0].reshape(1, d)
    g_fin = g_norm_final.reshape(1, d)

    xp = x_prompt.reshape(bp * t, d)
    cos, sin = _rope_tables(jnp.arange(t))
    cos, sin = jnp.tile(cos, (bp, 1)), jnp.tile(sin, (bp, 1))
    qq, nsa, win, gates, kvb = _qkv_proj(xp, g_mix, wp["w_qkv"], cos, sin)
    u = _norm_proj(xp, g_mix, wp["w_u"])
    gab = _norm_proj(xp, g_mix, wp["w_gab"], act="sigmoid")
    qq3, nsa3, kvb3 = qq.reshape(bp, t, -1), nsa.reshape(bp, t, -1), kvb.reshape(bp, t, -1)
    y = _compress_prompt(nsa3, wp["wk"], wp["wv"])
    o_cmp, sel = _cmp_prompt(qq3, y)
    o_sel = _sel_prompt(qq3, sel, kvb3)
    o_win = _win_prompt(qq3, kvb3)
    u3 = u.reshape(bp, t, -1)
    pool_out = _pool(u3, u3, wp["w_pool"], wp["pool_scale"], base=0, zero_first_prev=True)
    y_prompt = _token_tail(xp, o_cmp.reshape(bp * t, -1), o_sel.reshape(bp * t, -1), o_win.reshape(bp * t, -1),
                           gates, pool_out.reshape(bp * t, -1), gab, wp, g_ffn, g_fin)
    wlen = min(WINDOW, t)
    new_kv_p = nsa.reshape(1, bp, t, 4, KV_HEADS, HEAD_DIM)
    new_win_p = win.reshape(bp, t, 2, KV_HEADS, HEAD_DIM)[None, :, t - wlen:]
    new_pool_p = u3[None, :, t - (POOL_PREV - 1):]

    xs = x_sample.reshape(bs * ts, d)
    cos_s, sin_s = _rope_tables(past + jnp.arange(ts))
    cos_s, sin_s = jnp.tile(cos_s, (bs, 1)), jnp.tile(sin_s, (bs, 1))
    qq_s, nsa_s, win_s, gates_s, _ = _qkv_proj(xs, g_mix, wp["w_qkv"], cos_s, sin_s)
    u_s = _norm_proj(xs, g_mix, wp["w_u"])
    gab_s = _norm_proj(xs, g_mix, wp["w_gab"], act="sigmoid")
    qq_s3, nsa_s3, win_s3 = qq_s.reshape(bs, ts, -1), nsa_s.reshape(bs, ts, -1), win_s.reshape(bs, ts, -1)
    cache_rows = cache_kv_nsa.reshape(-1, HEAD_DIM)
    y_s = _compress_pages(cache_rows, page_table, page_size, wp["wk"], wp["wv"])
    o_cmp_s, sel_s = _cmp_sample(qq_s3, y_s, past=past)
    o_sel_s = _sel_sample(cache_rows, page_table, page_size, qq_s3, sel_s, nsa_s3, past=past)
    st_win = state_win_kv[0].reshape(bs, -1, 2 * KV_HEADS * HEAD_DIM)
    o_win_s = _win_sample(qq_s3, st_win, win_s3)
    u_s3 = u_s.reshape(bs, ts, -1)
    st_pool = state_pool[0]
    prev = jnp.pad(st_pool, ((0, 0), (POOL_PREV - st_pool.shape[1], 0), (0, 0)))
    pool_out_s = _pool(prev, u_s3, wp["w_pool"], wp["pool_scale"], base=st_pool.shape[1], zero_first_prev=False)
    y_sample = _token_tail(xs, o_cmp_s.reshape(bs * ts, -1), o_sel_s.reshape(bs * ts, -1),
                           o_win_s.reshape(bs * ts, -1), gates_s, pool_out_s.reshape(bs * ts, -1), gab_s, wp,
                           g_ffn, g_fin)
    new_kv_s = nsa_s.reshape(1, bs, ts, 4, KV_HEADS, HEAD_DIM)
    win_ext = jnp.concatenate([st_win, win_s3], axis=1)
    wlen_s = min(WINDOW, win_ext.shape[1])
    new_win_s = win_ext[:, win_ext.shape[1] - wlen_s:].reshape(1, bs, wlen_s, 2, KV_HEADS, HEAD_DIM)
    pool_ext = jnp.concatenate([st_pool, u_s3], axis=1)
    new_pool_s = pool_ext[None, :, pool_ext.shape[1] - (POOL_PREV - 1):]

    return (y_prompt.reshape(bp, t, d), y_sample.reshape(bs, ts, d), new_kv_p, new_kv_s, new_win_p, new_win_s,
            new_pool_p, new_pool_s)
```

```python
import functools

import jax
import jax.numpy as jnp
import numpy as np
from jax import lax
from jax.experimental import pallas as pl
from jax.experimental.pallas import tpu as pltpu

F32 = jnp.float32
BF16 = jnp.bfloat16

HEAD_DIM = 128
N_HEADS = 8
KV_HEADS = 2
GROUP = N_HEADS // KV_HEADS
CMP_LEN = 32
CMP_STRIDE = 16
SEL_BLOCK = 64
SEL_TOPN = 16
CMP_PER_SEL = SEL_BLOCK // CMP_STRIDE
WINDOW = 512
Q_BLOCK = 128
ROPE_THETA = 10000.0
FORCE_SCORE = 1e4
POOL_WINDOWS = (2, 4, 8, 16)
POOL_PREV = 16
PEER_HEADS = 8
PEER_NKEYS = 128
PEER_TOPK = 16
EPS = 1e-6
SCALE = HEAD_DIM ** -0.5
LOG2E = 1.4426950408889634
NEG_INF = float("-inf")

VMEM_LIMIT = 56 * 1024 * 1024


def _params(*sem):
    return pltpu.CompilerParams(dimension_semantics=sem, vmem_limit_bytes=VMEM_LIMIT)


def _dot(a, b):
    return jnp.dot(a, b, preferred_element_type=F32)


def _dot_nt(a, b):
    return lax.dot_general(a, b, (((1,), (1,)), ((), ())), preferred_element_type=F32)


def _rms(x, g):
    return x * lax.rsqrt(jnp.mean(x * x, axis=-1, keepdims=True) + EPS) * g


def _masked_softmax_parts(parts, masks):
    parts = [jnp.where(mk, s, NEG_INF) for s, mk in zip(parts, masks)]
    m = functools.reduce(jnp.maximum, [jnp.max(s, axis=-1, keepdims=True) for s in parts])
    m = jnp.where(m == NEG_INF, 0.0, m)
    es = [jnp.exp(s - m) for s in parts]
    den = functools.reduce(jnp.add, [jnp.sum(e, axis=-1, keepdims=True) for e in es])
    inv = 1.0 / jnp.maximum(den, 1e-30)
    return [e * inv for e in es]


def _topk_select(v, n, axis):
    size = v.shape[axis]
    idx = lax.broadcasted_iota(jnp.int32, v.shape, axis).astype(F32)
    sel = jnp.zeros(v.shape, F32)
    for _ in range(n):
        m = jnp.max(v, axis=axis, keepdims=True)
        first = jnp.min(jnp.where(v == m, idx, float(size)), axis=axis, keepdims=True)
        hit = idx == first
        v = jnp.where(hit, NEG_INF, v)
        sel = jnp.where(hit, 1.0, sel)
    return sel


def _topk_sorted(v, n, payload=None):
    size = v.shape[0]
    idx = lax.broadcasted_iota(jnp.int32, v.shape, 0).astype(F32)
    vals, picks = [], []
    for _ in range(n):
        m = jnp.max(v, axis=0, keepdims=True)
        first = jnp.min(jnp.where(v == m, idx, float(size)), axis=0, keepdims=True)
        hit = idx == first
        vals.append(m)
        if payload is None:
            picks.append(first)
        else:
            picks.append(jnp.max(jnp.where(hit, payload, -1.0), axis=0, keepdims=True))
        v = jnp.where(hit, NEG_INF, v)
    return jnp.concatenate(vals, axis=0), jnp.concatenate(picks, axis=0)


def _proj_kernel(x_ref, g_ref, w_ref, o_ref, *, act, tn):
    h = _rms(x_ref[...], g_ref[...]).astype(BF16)
    for c in range(w_ref.shape[1] // tn):
        z = _dot(h, w_ref[:, c * tn:(c + 1) * tn])
        if act == "sigmoid":
            z = jax.nn.sigmoid(z)
        o_ref[:, c * tn:(c + 1) * tn] = z.astype(o_ref.dtype)


def _norm_proj(x, g, w, *, act=None, out_dtype=F32, tm=256, tn=512, wn=2048):
    m, d = x.shape
    n = w.shape[1]
    tm = min(tm, m)
    wn = min(wn, n)
    return pl.pallas_call(
        functools.partial(_proj_kernel, act=act, tn=tn),
        grid=(n // wn, m // tm),
        in_specs=[pl.BlockSpec((tm, d), lambda j, i: (i, 0)),
                  pl.BlockSpec((1, d), lambda j, i: (0, 0)),
                  pl.BlockSpec((d, wn), lambda j, i: (0, j))],
        out_specs=pl.BlockSpec((tm, wn), lambda j, i: (i, j)),
        out_shape=jax.ShapeDtypeStruct((m, n), out_dtype),
        compiler_params=_params("parallel", "parallel"),
        name="norm_proj",
    )(x, g, w)


QKV_COLS = N_HEADS * HEAD_DIM + 6 * KV_HEADS * HEAD_DIM + HEAD_DIM


def _qkv_kernel(x_ref, g_ref, w_ref, cos_ref, sin_ref, qq_ref, nsa_ref, win_ref, gate_ref, kvb_ref):
    h = _rms(x_ref[...], g_ref[...]).astype(BF16)
    cos = cos_ref[...]
    sin = sin_ref[...]

    def rope(z):
        return z * cos + pltpu.roll(z, HEAD_DIM // 2, 1) * sin

    qw = N_HEADS * HEAD_DIM
    for c in range(qw // 512):
        z = _dot(h, w_ref[:, c * 512:(c + 1) * 512])
        for j in range(4):
            zh = z[:, j * 128:(j + 1) * 128]
            col = c * 512 + j * 128
            qq_ref[:, col:col + 128] = zh.astype(BF16)
            qq_ref[:, qw + col:qw + col + 128] = rope(zh).astype(BF16)
    for c in range(3):
        z = _dot(h, w_ref[:, qw + c * 512:qw + (c + 1) * 512])
        if c == 0:
            nsa_ref[:, 0:512] = z
        else:
            for j in range(2):
                zr = rope(z[:, j * 128:(j + 1) * 128])
                zv = z[:, 256 + j * 128:256 + (j + 1) * 128]
                if c == 1:
                    nsa_ref[:, 512 + j * 128:512 + (j + 1) * 128] = zr
                    nsa_ref[:, 768 + j * 128:768 + (j + 1) * 128] = zv
                    kvb_ref[:, j * 128:(j + 1) * 128] = zr.astype(BF16)
                    kvb_ref[:, 256 + j * 128:256 + (j + 1) * 128] = zv.astype(BF16)
                else:
                    win_ref[:, j * 128:(j + 1) * 128] = zr
                    win_ref[:, 256 + j * 128:256 + (j + 1) * 128] = zv
                    kvb_ref[:, 512 + j * 128:512 + (j + 1) * 128] = zr.astype(BF16)
                    kvb_ref[:, 768 + j * 128:768 + (j + 1) * 128] = zv.astype(BF16)
    z = _dot(h, w_ref[:, qw + 1536:qw + 1536 + 128])
    gate_ref[...] = jax.nn.sigmoid(z)


def _qkv_proj(x, g, w, cos, sin, *, tm=256):
    m, d = x.shape
    tm = min(tm, m)
    row = lambda i: (i, 0)
    const = lambda i: (0, 0)
    return pl.pallas_call(
        _qkv_kernel,
        grid=(m // tm,),
        in_specs=[pl.BlockSpec((tm, d), row), pl.BlockSpec((1, d), const),
                  pl.BlockSpec((d, QKV_COLS), const),
                  pl.BlockSpec((tm, HEAD_DIM), row), pl.BlockSpec((tm, HEAD_DIM), row)],
        out_specs=[pl.BlockSpec((tm, 2048), row), pl.BlockSpec((tm, 1024), row),
                   pl.BlockSpec((tm, 512), row), pl.BlockSpec((tm, 128), row),
                   pl.BlockSpec((tm, 1024), row)],
        out_shape=[jax.ShapeDtypeStruct((m, 2048), BF16), jax.ShapeDtypeStruct((m, 1024), F32),
                   jax.ShapeDtypeStruct((m, 512), F32), jax.ShapeDtypeStruct((m, 128), F32),
                   jax.ShapeDtypeStruct((m, 1024), BF16)],
        compiler_params=_params("parallel"),
        name="qkv_proj",
    )(x, g, w, cos, sin)


ROW_W = 4 * KV_HEADS * HEAD_DIM
SUB_W = CMP_STRIDE * ROW_W


def _compress_rows(load, n_rows, wk_ref, wv_ref, y_ref):
    for kv in range(2):
        w_ref = wk_ref if kv == 0 else wv_ref
        for g in range(KV_HEADS):
            comp = kv * KV_HEADS + g
            acc = jnp.zeros((n_rows, 2 * HEAD_DIM), F32)
            for i in range(CMP_STRIDE):
                acc = acc + _dot(load(i, comp).astype(BF16), w_ref[i])
            y_ref[0, :, 2 * comp * HEAD_DIM:2 * (comp + 1) * HEAD_DIM] = acc


def _compress_prompt_kernel(x_ref, wk_ref, wv_ref, y_ref):
    load = lambda i, comp: x_ref[0, :, i * ROW_W + comp * HEAD_DIM:i * ROW_W + (comp + 1) * HEAD_DIM]
    _compress_rows(load, x_ref.shape[1], wk_ref, wv_ref, y_ref)


def _compress_prompt(nsa, wk, wv, *, ts=128):
    b, t, _ = nsa.shape
    n_sub = t // CMP_STRIDE
    ts = min(ts, n_sub)
    wspec = pl.BlockSpec((CMP_STRIDE, HEAD_DIM, 2 * HEAD_DIM), lambda bi, ci: (0, 0, 0))
    return pl.pallas_call(
        _compress_prompt_kernel,
        grid=(b, n_sub // ts),
        in_specs=[pl.BlockSpec((1, ts, SUB_W), lambda bi, ci: (bi, ci, 0)), wspec, wspec],
        out_specs=pl.BlockSpec((1, ts, 1024), lambda bi, ci: (bi, ci, 0)),
        out_shape=jax.ShapeDtypeStruct((b, n_sub, 1024), F32),
        compiler_params=_params("parallel", "parallel"),
        name="compress_prompt",
    )(nsa.reshape(b, n_sub, SUB_W), wk, wv)


PAGES_PER_STEP = 16


ROWS_PER_POS = 4 * KV_HEADS


def _compress_pages_kernel(pt_ref, *refs):
    pages = refs[:PAGES_PER_STEP]
    wk_ref, wv_ref, y_ref = refs[PAGES_PER_STEP:]
    per_page = pages[0].shape[0] // (ROWS_PER_POS * CMP_STRIDE)

    def load(i, comp):
        rows = pl.ds(i * ROWS_PER_POS + comp, per_page, stride=ROWS_PER_POS * CMP_STRIDE)
        return jnp.concatenate([p[rows, :] for p in pages], axis=0)

    _compress_rows(load, PAGES_PER_STEP * per_page, wk_ref, wv_ref, y_ref)


def _page_specs(page_rows):
    def spec(k):
        return pl.BlockSpec((page_rows, HEAD_DIM), lambda bi, ci, pt: (pt[bi, ci * PAGES_PER_STEP + k], 0))
    return [spec(k) for k in range(PAGES_PER_STEP)]


def _compress_pages(cache_rows, page_table, page_size, wk, wv):
    nb, n_pages = page_table.shape
    per_page = page_size // CMP_STRIDE
    rows = PAGES_PER_STEP * per_page
    const3 = lambda bi, ci, pt: (0, 0, 0)
    grid_spec = pltpu.PrefetchScalarGridSpec(
        num_scalar_prefetch=1,
        grid=(nb, n_pages // PAGES_PER_STEP),
        in_specs=_page_specs(page_size * ROWS_PER_POS) + [
            pl.BlockSpec((CMP_STRIDE, HEAD_DIM, 2 * HEAD_DIM), const3),
            pl.BlockSpec((CMP_STRIDE, HEAD_DIM, 2 * HEAD_DIM), const3)],
        out_specs=pl.BlockSpec((1, rows, 1024), lambda bi, ci, pt: (bi, ci, 0)),
    )
    return pl.pallas_call(
        _compress_pages_kernel,
        grid_spec=grid_spec,
        out_shape=jax.ShapeDtypeStruct((nb, n_pages * per_page, 1024), F32),
        compiler_params=_params("parallel", "parallel"),
        name="compress_pages",
    )(page_table, *([cache_rows] * PAGES_PER_STEP), wk, wv)


def _combine_compressed(y_ref, kcp_ref, tmp_ref):
    ns = y_ref.shape[1]
    nb = ns // CMP_PER_SEL
    last = lax.broadcasted_iota(jnp.int32, (ns, 1), 0) == ns - 1
    for a in range(2 * KV_HEADS):
        y1 = y_ref[0, :, a * 256:a * 256 + 128]
        y2 = y_ref[0, :, a * 256 + 128:(a + 1) * 256]
        nxt = jnp.where(last, 0.0, pltpu.roll(y2, ns - 1, 0))
        tmp_ref[...] = y1 + nxt
        for c in range(CMP_PER_SEL):
            kcp_ref[a * CMP_PER_SEL + c] = tmp_ref[pl.ds(c, nb, stride=CMP_PER_SEL), :].astype(BF16)


def _cmp_attention(q_rows, pos, kcp_ref, g, rows_per_tok):
    nb = kcp_ref.shape[1]
    blk = lax.broadcasted_iota(jnp.int32, (1, nb), 1)
    masks = [(SEL_BLOCK * blk + CMP_STRIDE * c + CMP_LEN - 1) <= pos for c in range(CMP_PER_SEL)]
    outs = []
    imp = None
    for qh in q_rows:
        s = [_dot_nt(qh, kcp_ref[g * CMP_PER_SEL + c]) * SCALE for c in range(CMP_PER_SEL)]
        p = _masked_softmax_parts(s, masks)
        o = functools.reduce(jnp.add, [
            _dot(p[c].astype(BF16), kcp_ref[(KV_HEADS + g) * CMP_PER_SEL + c]) for c in range(CMP_PER_SEL)])
        outs.append(o)
        ps = functools.reduce(jnp.add, p)
        imp = ps if imp is None else imp + ps
    return outs, imp


def _cmp_prompt_kernel(q_ref, y_ref, o_ref, sel_ref, kcp_ref, tmp_ref):
    i = pl.program_id(1)

    @pl.when(i == 0)
    def _():
        _combine_compressed(y_ref, kcp_ref, tmp_ref)

    nb = kcp_ref.shape[1]
    pos = i * Q_BLOCK + lax.broadcasted_iota(jnp.int32, (Q_BLOCK, 1), 0)
    blk = lax.broadcasted_iota(jnp.int32, (1, nb), 1)
    cur = pos // SEL_BLOCK
    forced = (blk == cur) | (blk == 0)
    causal = blk <= cur
    for g in range(KV_HEADS):
        q_rows = [q_ref[0, :, (g * GROUP + r) * HEAD_DIM:(g * GROUP + r + 1) * HEAD_DIM] for r in range(GROUP)]
        outs, imp = _cmp_attention(q_rows, pos, kcp_ref, g, 1)
        for r in range(GROUP):
            hcol = (g * GROUP + r) * HEAD_DIM
            o_ref[0, :, hcol:hcol + HEAD_DIM] = outs[r]
        v = jnp.where(forced, FORCE_SCORE, jnp.where(causal, imp, -1.0))
        sel_t = _topk_select(v.T, SEL_TOPN, 0)
        sel = jnp.where(causal, sel_t.T, 0.0)
        sel_ref[0, :, g * nb:(g + 1) * nb] = sel.astype(BF16)


def _cmp_prompt(qq3, y):
    b, t, _ = qq3.shape
    ns = y.shape[1]
    nb = ns // CMP_PER_SEL
    return pl.pallas_call(
        _cmp_prompt_kernel,
        grid=(b, t // Q_BLOCK),
        in_specs=[pl.BlockSpec((1, Q_BLOCK, 1024), lambda bi, i: (bi, i, 0)),
                  pl.BlockSpec((1, ns, 1024), lambda bi, i: (bi, 0, 0))],
        out_specs=[pl.BlockSpec((1, Q_BLOCK, 1024), lambda bi, i: (bi, i, 0)),
                   pl.BlockSpec((1, Q_BLOCK, KV_HEADS * nb), lambda bi, i: (bi, i, 0))],
        out_shape=[jax.ShapeDtypeStruct((b, t, 1024), F32),
                   jax.ShapeDtypeStruct((b, t, KV_HEADS * nb), BF16)],
        scratch_shapes=[pltpu.VMEM((2 * KV_HEADS * CMP_PER_SEL, nb, HEAD_DIM), BF16),
                        pltpu.VMEM((ns, HEAD_DIM), F32)],
        compiler_params=_params("parallel", "arbitrary"),
        name="cmp_prompt",
    )(qq3, y)


SEL_CHUNK = 512


def _expand_blocks(sel, first_block, width):
    nb = sel.shape[1]
    b = lax.broadcasted_iota(jnp.int32, (nb, width), 0)
    t = lax.broadcasted_iota(jnp.int32, (nb, width), 1)
    e = jnp.where(b == first_block + t // SEL_BLOCK, 1.0, 0.0).astype(BF16)
    return _dot(sel, e)


def _sel_prompt_kernel(q_ref, sel_ref, kv_ref, o_ref):
    i = pl.program_id(1)
    nb = sel_ref.shape[2] // KV_HEADS
    pos = i * Q_BLOCK + lax.broadcasted_iota(jnp.int32, (Q_BLOCK, 1), 0)
    n_chunks = (i * Q_BLOCK + Q_BLOCK + SEL_CHUNK - 1) // SEL_CHUNK
    lane = lax.broadcasted_iota(jnp.int32, (1, SEL_CHUNK), 1)
    rows = GROUP * Q_BLOCK
    for g in range(KV_HEADS):
        q = jnp.concatenate(
            [q_ref[0, :, (g * GROUP + r) * HEAD_DIM:(g * GROUP + r + 1) * HEAD_DIM] for r in range(GROUP)], axis=0)
        sel = sel_ref[0, :, g * nb:(g + 1) * nb]

        def body(c, carry):
            m, l, acc = carry
            start = pl.multiple_of(c * SEL_CHUNK, SEL_CHUNK)
            k = kv_ref[0, pl.ds(start, SEL_CHUNK), g * HEAD_DIM:(g + 1) * HEAD_DIM]
            v = kv_ref[0, pl.ds(start, SEL_CHUNK), (KV_HEADS + g) * HEAD_DIM:(KV_HEADS + g + 1) * HEAD_DIM]
            picked = _expand_blocks(sel, c * (SEL_CHUNK // SEL_BLOCK), SEL_CHUNK)
            bias = jnp.where(picked > 0.5, jnp.where((start + lane) <= pos, 0.0, NEG_INF), NEG_INF)
            s = _dot_nt(q, k) * (SCALE * LOG2E) + jnp.concatenate([bias] * GROUP, axis=0)
            m_new = jnp.maximum(m, jnp.max(s, axis=-1, keepdims=True))
            m_safe = jnp.where(m_new == NEG_INF, 0.0, m_new)
            p = jnp.exp2(s - m_safe)
            alpha = jnp.exp2(m - m_safe)
            l = alpha * l + jnp.sum(p, axis=-1, keepdims=True)
            acc = alpha * acc + _dot(p.astype(BF16), v)
            return m_new, l, acc

        init = (jnp.full((rows, 1), NEG_INF, F32), jnp.zeros((rows, 1), F32), jnp.zeros((rows, HEAD_DIM), F32))
        m, l, acc = lax.fori_loop(0, n_chunks, body, init)
        o = acc * (1.0 / jnp.maximum(l, 1e-30))
        for r in range(GROUP):
            hcol = (g * GROUP + r) * HEAD_DIM
            o_ref[0, :, hcol:hcol + HEAD_DIM] = o[r * Q_BLOCK:(r + 1) * Q_BLOCK]


def _sel_prompt(qq3, sel, kvb3):
    b, t, _ = qq3.shape
    nb2 = sel.shape[2]
    return pl.pallas_call(
        _sel_prompt_kernel,
        grid=(b, t // Q_BLOCK),
        in_specs=[pl.BlockSpec((1, Q_BLOCK, 1024), lambda bi, i: (bi, i, 1)),
                  pl.BlockSpec((1, Q_BLOCK, nb2), lambda bi, i: (bi, i, 0)),
                  pl.BlockSpec((1, t, 512), lambda bi, i: (bi, 0, 0))],
        out_specs=pl.BlockSpec((1, Q_BLOCK, 1024), lambda bi, i: (bi, i, 0)),
        out_shape=jax.ShapeDtypeStruct((b, t, 1024), F32),
        compiler_params=_params("parallel", "arbitrary"),
        name="sel_prompt",
    )(qq3, sel, kvb3)


def _win_prompt_kernel(q_ref, kv_ref, o_ref, *, span):
    i = pl.program_id(1)
    pos = i * Q_BLOCK + lax.broadcasted_iota(jnp.int32, (Q_BLOCK, 1), 0)
    pos = jnp.concatenate([pos] * GROUP, axis=0)
    start = pl.multiple_of(jnp.maximum(i * Q_BLOCK + Q_BLOCK - span, 0), Q_BLOCK)
    kpos = start + lax.broadcasted_iota(jnp.int32, (1, span), 1)
    diff = pos - kpos
    ok = (diff >= 0) & (diff < WINDOW)
    for g in range(KV_HEADS):
        q = jnp.concatenate(
            [q_ref[0, :, (g * GROUP + r) * HEAD_DIM:(g * GROUP + r + 1) * HEAD_DIM] for r in range(GROUP)], axis=0)
        k = kv_ref[0, pl.ds(start, span), g * HEAD_DIM:(g + 1) * HEAD_DIM]
        v = kv_ref[0, pl.ds(start, span), (KV_HEADS + g) * HEAD_DIM:(KV_HEADS + g + 1) * HEAD_DIM]
        s = _dot_nt(q, k) * SCALE
        (p,) = _masked_softmax_parts([s], [ok])
        o = _dot(p.astype(BF16), v)
        for r in range(GROUP):
            hcol = (g * GROUP + r) * HEAD_DIM
            o_ref[0, :, hcol:hcol + HEAD_DIM] = o[r * Q_BLOCK:(r + 1) * Q_BLOCK]


def _win_prompt(qq3, kvb3):
    b, t, _ = qq3.shape
    span = min(WINDOW + Q_BLOCK, t)
    return pl.pallas_call(
        functools.partial(_win_prompt_kernel, span=span),
        grid=(b, t // Q_BLOCK),
        in_specs=[pl.BlockSpec((1, Q_BLOCK, 1024), lambda bi, i: (bi, i, 1)),
                  pl.BlockSpec((1, t, 512), lambda bi, i: (bi, 0, 1))],
        out_specs=pl.BlockSpec((1, Q_BLOCK, 1024), lambda bi, i: (bi, i, 0)),
        out_shape=jax.ShapeDtypeStruct((b, t, 1024), F32),
        compiler_params=_params("parallel", "arbitrary"),
        name="win_prompt",
    )(qq3, kvb3)


def _pool_kernel(prev_ref, u_ref, w_ref, sc_ref, o_ref, *, base, zero_first_prev):
    i = pl.program_id(1)
    tq = u_ref.shape[1]
    cur = u_ref[0]
    prev = prev_ref[0]
    if zero_first_prev:
        prev = jnp.where(i == 0, 0.0, prev)
    ext = jnp.concatenate([prev, cur], axis=0)
    gpos = base + i * tq + lax.broadcasted_iota(jnp.int32, (tq, 1), 0)
    gw = ext.shape[1] // len(POOL_WINDOWS)
    for gi, w in enumerate(POOL_WINDOWS):
        s = ext[:, gi * gw:(gi + 1) * gw]
        span = 1
        while span < w:
            s = s + pltpu.roll(s, span, 0)
            span *= 2
        cnt = jnp.minimum(gpos + 1, w).astype(F32)
        d = s[POOL_PREV:] / cnt - cur[:, gi * gw:(gi + 1) * gw]
        o = _dot(d.astype(BF16), w_ref[gi]) * sc_ref[:, gi * gw:(gi + 1) * gw]
        o_ref[0, :, gi * gw:(gi + 1) * gw] = o


def _pool(prev, u3, w, scale, *, base, zero_first_prev, tq=512):
    b, t, c = u3.shape
    tq = min(tq, t)
    ratio = tq // POOL_PREV
    if zero_first_prev:
        prev_map = lambda bi, i: (bi, jnp.maximum(i * ratio - 1, 0), 0)
    else:
        prev_map = lambda bi, i: (bi, 0, 0)
    ng = len(POOL_WINDOWS)
    return pl.pallas_call(
        functools.partial(_pool_kernel, base=base, zero_first_prev=zero_first_prev),
        grid=(b, t // tq),
        in_specs=[pl.BlockSpec((1, POOL_PREV, c), prev_map),
                  pl.BlockSpec((1, tq, c), lambda bi, i: (bi, i, 0)),
                  pl.BlockSpec((ng, c // ng, c // ng), lambda bi, i: (0, 0, 0)),
                  pl.BlockSpec((1, c), lambda bi, i: (0, 0))],
        out_specs=pl.BlockSpec((1, tq, c), lambda bi, i: (bi, i, 0)),
        out_shape=jax.ShapeDtypeStruct((b, t, c), F32),
        compiler_params=_params("parallel", "parallel"),
        name="pool_mix",
    )(prev, u3, w, scale)


def _mix_kernel(oc_ref, os_ref, ow_ref, gt_ref, po_ref, ga_ref, gb_ref, wa_ref, wp_ref, mix_ref):
    gt = gt_ref[...]
    cols = []
    for h in range(N_HEADS):
        sl = slice(h * HEAD_DIM, (h + 1) * HEAD_DIM)
        o = (gt[:, h:h + 1] * oc_ref[:, sl] + gt[:, N_HEADS + h:N_HEADS + h + 1] * os_ref[:, sl]
             + gt[:, 2 * N_HEADS + h:2 * N_HEADS + h + 1] * ow_ref[:, sl])
        cols.append(o.astype(BF16))
    a = _dot(jnp.concatenate(cols, axis=1), wa_ref[...])
    p = _dot(po_ref[...].astype(BF16), wp_ref[...])
    mix_ref[...] = (ga_ref[...] * a + gb_ref[...] * p).astype(BF16)


def _mix(oc, os_, ow, gt, po, gab, wa, wp, *, tm=256):
    m = oc.shape[0]
    d = wa.shape[1]
    tm = min(tm, m)
    row = lambda i: (i, 0)
    const = lambda i: (0, 0)
    return pl.pallas_call(
        _mix_kernel,
        grid=(m // tm,),
        in_specs=[pl.BlockSpec((tm, 1024), row), pl.BlockSpec((tm, 1024), row), pl.BlockSpec((tm, 1024), row),
                  pl.BlockSpec((tm, 128), row), pl.BlockSpec((tm, 1024), row),
                  pl.BlockSpec((tm, d), lambda i: (i, 0)), pl.BlockSpec((tm, d), lambda i: (i, 1)),
                  pl.BlockSpec((1024, d), const), pl.BlockSpec((1024, d), const)],
        out_specs=pl.BlockSpec((tm, d), row),
        out_shape=jax.ShapeDtypeStruct((m, d), BF16),
        compiler_params=_params("parallel"),
        name="branch_mix",
    )(oc, os_, ow, gt, po, gab, gab, wa, wp)


def _out_kernel(x_ref, mix_ref, w_ref, o_ref):
    o_ref[...] = x_ref[...] + _dot(mix_ref[...], w_ref[...])


def _out_proj(x, mix, w, *, tm=256):
    m, d = x.shape
    tm = min(tm, m)
    return pl.pallas_call(
        _out_kernel,
        grid=(m // tm,),
        in_specs=[pl.BlockSpec((tm, d), lambda i: (i, 0)), pl.BlockSpec((tm, d), lambda i: (i, 0)),
                  pl.BlockSpec((d, d), lambda i: (0, 0))],
        out_specs=pl.BlockSpec((tm, d), lambda i: (i, 0)),
        out_shape=jax.ShapeDtypeStruct((m, d), F32),
        compiler_params=_params("parallel"),
        name="out_proj",
    )(x, mix, w)


def _pair_candidates(sv, si):
    k = PEER_TOPK
    row = lax.broadcasted_iota(jnp.int32, (8, 1), 0)
    vals, ids = [], []
    for a in range(k // 2):
        lim = k // (a + 1)
        nrow = k if a == 0 else 8
        v = sv[0][a:a + 1] + sv[1][:nrow]
        if lim < nrow:
            v = jnp.where(row < lim, v, NEG_INF)
        vals.append(v)
        ids.append(si[0][a:a + 1] * float(PEER_NKEYS) + si[1][:nrow])
    vals.append(sv[0][k // 2:] + sv[1][0:1])
    ids.append(si[0][k // 2:] * float(PEER_NKEYS) + si[1][0:1])
    return jnp.concatenate(vals, axis=0), jnp.concatenate(ids, axis=0)


def _peer_score_kernel(x_ref, g_ref, wq_ref, keys_ref, ids_ref, gw_ref):
    h = _rms(x_ref[...], g_ref[...]).astype(BF16)
    q = _dot(h, wq_ref[...]).astype(BF16)
    dk = PEER_NKEYS
    ids, gws = [], []
    for hd in range(PEER_HEADS):
        sv, si = [], []
        for c in range(2):
            qhc = q[:, (hd * 2 + c) * dk:(hd * 2 + c + 1) * dk]
            st = _dot_nt(keys_ref[hd * 2 + c], qhc)
            v, ix = _topk_sorted(st, PEER_TOPK)
            sv.append(v)
            si.append(ix)
        comb, eid = _pair_candidates(sv, si)
        cv, ce = _topk_sorted(comb, PEER_TOPK, payload=eid)
        e = jnp.exp(cv - cv[0:1])
        gws.append(e / jnp.sum(e, axis=0, keepdims=True))
        ids.append(ce)
    ids_ref[...] = jnp.concatenate(ids, axis=0).T.astype(jnp.int32)
    gw_ref[...] = jnp.concatenate(gws, axis=0)


def _peer_score(x, g, wq, keys, *, tb=128):
    m, d = x.shape
    nk = PEER_HEADS * PEER_TOPK
    return pl.pallas_call(
        _peer_score_kernel,
        grid=(m // tb,),
        in_specs=[pl.BlockSpec((tb, d), lambda i: (i, 0)), pl.BlockSpec((1, d), lambda i: (0, 0)),
                  pl.BlockSpec(wq.shape, lambda i: (0, 0)),
                  pl.BlockSpec(keys.shape, lambda i: (0, 0, 0))],
        out_specs=[pl.BlockSpec((tb, nk), lambda i: (i, 0)), pl.BlockSpec((nk, tb), lambda i: (0, i))],
        out_shape=[jax.ShapeDtypeStruct((m, nk), jnp.int32), jax.ShapeDtypeStruct((nk, m), F32)],
        compiler_params=_params("parallel"),
        name="peer_score",
    )(x, g, wq, keys)


def _gelu(x):
    return 0.5 * x * (1.0 + lax.erf(x * (2.0 ** -0.5)))


PEER_RING = 8


def _peer_apply_kernel(ids_ref, x_ref, gffn_ref, gw_ref, gfin_ref, tab_ref, y_ref, *scratch):
    bufs = scratch[:PEER_RING]
    sem, hbuf, obuf = scratch[PEER_RING:]
    tb, d = x_ref.shape
    nk = gw_ref.shape[0]
    nchunk = d // 128
    ahead = PEER_RING - 1
    hbuf[...] = _rms(x_ref[...], gffn_ref[...])

    def row_copy(t, k, slot):
        return pltpu.make_async_copy(tab_ref.at[ids_ref[t, k]], bufs[slot].at[pl.ds(k, 1), :], sem.at[slot])

    def issue(t, slot):
        for k in range(nk):
            row_copy(t, k, slot).start(priority=k % 2)

    def drain(t, slot):
        for k in range(nk):
            row_copy(t, k, slot).wait()

    for s in range(ahead):
        issue(s, s)
    lane = lax.broadcasted_iota(jnp.int32, (nk, tb), 1)

    def compute(t, slot):
        buf = bufs[slot]
        hrow = hbuf[pl.ds(t, 1), :]
        acc = jnp.zeros((nk, 128), F32)
        for j in range(nchunk):
            w = buf[:, j * 128:(j + 1) * 128]
            u = pltpu.bitcast(w << 16, F32)
            acc = acc + u * hrow[:, j * 128:(j + 1) * 128]
        act = jnp.sum(acc, axis=1, keepdims=True)
        gcol = jnp.sum(jnp.where(lane == t, gw_ref[...], 0.0), axis=1, keepdims=True)
        coef = _gelu(act) * gcol
        outs = []
        for j in range(nchunk):
            w = buf[:, j * 128:(j + 1) * 128]
            v = pltpu.bitcast(w & jnp.uint32(0xFFFF0000), F32)
            outs.append(jnp.sum(v * coef, axis=0, keepdims=True))
        obuf[pl.ds(t, 1), :] = jnp.concatenate(outs, axis=1)

    def body(p, carry):
        for s in range(PEER_RING):
            t = p * PEER_RING + s
            drain(t, s)
            issue(jnp.minimum(t + ahead, tb - 1), (s + ahead) % PEER_RING)
            compute(t, s)
        return carry

    lax.fori_loop(0, tb // PEER_RING, body, 0)
    for s in range(ahead):
        drain(tb - 1, (tb + s) % PEER_RING)
    y_ref[...] = _rms(x_ref[...] + obuf[...], gfin_ref[...])


def _peer_apply(ids, x, gffn, gw, gfin, table, *, tb=128):
    m, d = x.shape
    nk = ids.shape[1]
    return pl.pallas_call(
        _peer_apply_kernel,
        grid=(m // tb,),
        in_specs=[pl.BlockSpec((tb, nk), lambda i: (i, 0), memory_space=pltpu.SMEM),
                  pl.BlockSpec((tb, d), lambda i: (i, 0)),
                  pl.BlockSpec((1, d), lambda i: (0, 0)),
                  pl.BlockSpec((nk, tb), lambda i: (0, i)),
                  pl.BlockSpec((1, d), lambda i: (0, 0)),
                  pl.BlockSpec(memory_space=pl.ANY)],
        out_specs=pl.BlockSpec((tb, d), lambda i: (i, 0)),
        out_shape=jax.ShapeDtypeStruct((m, d), F32),
        scratch_shapes=[pltpu.VMEM((nk, d), jnp.uint32)] * PEER_RING + [
            pltpu.SemaphoreType.DMA((PEER_RING,)), pltpu.VMEM((tb, d), F32), pltpu.VMEM((tb, d), F32)],
        compiler_params=_params("arbitrary"),
        name="peer_apply",
    )(ids, x, gffn, gw, gfin, table)


def _stack_heads(q_ref, g):
    return jnp.concatenate(
        [q_ref[0, :, (g * GROUP + r) * HEAD_DIM:(g * GROUP + r + 1) * HEAD_DIM] for r in range(GROUP)], axis=0)


NEW_PAD = 128


def _pad_new(x):
    return jnp.concatenate([x, jnp.zeros((NEW_PAD - x.shape[0], x.shape[1]), F32)], axis=0).astype(BF16)


def _cmp_sample_kernel(q_ref, y_ref, o_ref, sel_ref, kcp_ref, tmp_ref, *, past):
    _combine_compressed(y_ref, kcp_ref, tmp_ref)
    t = q_ref.shape[1]
    nb = kcp_ref.shape[1]
    tpos = past + lax.broadcasted_iota(jnp.int32, (t, 1), 0)
    pos = jnp.concatenate([tpos] * GROUP, axis=0)
    blk = lax.broadcasted_iota(jnp.int32, (1, nb), 1)
    for g in range(KV_HEADS):
        q = _stack_heads(q_ref, g)
        outs, imp = _cmp_attention([q], pos, kcp_ref, g, t)
        o = outs[0]
        for r in range(GROUP):
            hcol = (g * GROUP + r) * HEAD_DIM
            o_ref[0, :, hcol:hcol + HEAD_DIM] = o[r * t:(r + 1) * t]
        imp_t = functools.reduce(jnp.add, [imp[r * t:(r + 1) * t] for r in range(GROUP)])
        v = jnp.where(blk == 0, FORCE_SCORE, imp_t)
        sel = _topk_select(v, SEL_TOPN - 1, 1)
        sel_ref[0, :, g * nb:(g + 1) * nb] = sel.astype(BF16)


def _cmp_sample(qq3, y, *, past):
    b, t, _ = qq3.shape
    ns = y.shape[1]
    nb = ns // CMP_PER_SEL
    return pl.pallas_call(
        functools.partial(_cmp_sample_kernel, past=past),
        grid=(b,),
        in_specs=[pl.BlockSpec((1, t, 1024), lambda bi: (bi, 0, 0)),
                  pl.BlockSpec((1, ns, 1024), lambda bi: (bi, 0, 0))],
        out_specs=[pl.BlockSpec((1, t, 1024), lambda bi: (bi, 0, 0)),
                   pl.BlockSpec((1, t, KV_HEADS * nb), lambda bi: (bi, 0, 0))],
        out_shape=[jax.ShapeDtypeStruct((b, t, 1024), F32),
                   jax.ShapeDtypeStruct((b, t, KV_HEADS * nb), BF16)],
        scratch_shapes=[pltpu.VMEM((2 * KV_HEADS * CMP_PER_SEL, nb, HEAD_DIM), BF16),
                        pltpu.VMEM((ns, HEAD_DIM), F32)],
        compiler_params=_params("parallel"),
        name="cmp_sample",
    )(qq3, y)


def _sel_sample_kernel(pt_ref, *refs, past):
    pages = refs[:PAGES_PER_STEP]
    q_ref, sel_ref, new_ref, o_ref, m_ref, l_ref, acc_ref = refs[PAGES_PER_STEP:]
    c = pl.program_id(1)
    nc = pl.num_programs(1)
    t = q_ref.shape[1]
    rows = GROUP * t
    nb = sel_ref.shape[2] // KV_HEADS
    page = pages[0].shape[0] // ROWS_PER_POS
    width = PAGES_PER_STEP * page

    def component(slot, g):
        rows = pl.ds(slot * KV_HEADS + g, page, stride=ROWS_PER_POS)
        return jnp.concatenate([p[rows, :] for p in pages], axis=0).astype(BF16)

    @pl.when(c == 0)
    def _():
        m_ref[...] = jnp.full(m_ref.shape, NEG_INF, F32)
        l_ref[...] = jnp.zeros(l_ref.shape, F32)
        acc_ref[...] = jnp.zeros(acc_ref.shape, F32)

    def update(g, s, ok, v):
        s = jnp.where(ok, s, NEG_INF)
        m = m_ref[g]
        m_new = jnp.maximum(m, jnp.max(s, axis=-1, keepdims=True))
        m_safe = jnp.where(m_new == NEG_INF, 0.0, m_new)
        p = jnp.exp(s - m_safe)
        alpha = jnp.exp(m - m_safe)
        l_ref[g] = alpha * l_ref[g] + jnp.sum(p, axis=-1, keepdims=True)
        acc_ref[g] = alpha * acc_ref[g] + _dot(p.astype(BF16), v)
        m_ref[g] = m_new

    for g in range(KV_HEADS):
        q = _stack_heads(q_ref, g)
        k = component(2, g)
        v = component(3, g)
        s = _dot_nt(q, k) * SCALE
        picked = _expand_blocks(sel_ref[0, :, g * nb:(g + 1) * nb], c * (width // SEL_BLOCK), width)
        picked = jnp.concatenate([picked] * GROUP, axis=0)
        update(g, s, picked > 0.5, v)

    @pl.when(c == nc - 1)
    def _():
        qi = lax.broadcasted_iota(jnp.int32, (t, 1), 0)
        qi = jnp.concatenate([qi] * GROUP, axis=0)
        ok = lax.broadcasted_iota(jnp.int32, (1, NEW_PAD), 1) <= qi
        for g in range(KV_HEADS):
            q = _stack_heads(q_ref, g)
            k = _pad_new(new_ref[0, :, 512 + g * HEAD_DIM:512 + (g + 1) * HEAD_DIM])
            v = _pad_new(new_ref[0, :, 768 + g * HEAD_DIM:768 + (g + 1) * HEAD_DIM])
            update(g, _dot_nt(q, k) * SCALE, ok, v)
            o = acc_ref[g] * (1.0 / jnp.maximum(l_ref[g], 1e-30))
            for r in range(GROUP):
                hcol = (g * GROUP + r) * HEAD_DIM
                o_ref[0, :, hcol:hcol + HEAD_DIM] = o[r * t:(r + 1) * t]


def _sel_sample(cache_rows, page_table, page_size, qq3, sel, nsa3, *, past):
    nbatch, n_pages = page_table.shape
    t = qq3.shape[1]
    nb2 = sel.shape[2]
    rows = GROUP * t
    grid_spec = pltpu.PrefetchScalarGridSpec(
        num_scalar_prefetch=1,
        grid=(nbatch, n_pages // PAGES_PER_STEP),
        in_specs=_page_specs(page_size * ROWS_PER_POS) + [
            pl.BlockSpec((1, t, 1024), lambda bi, ci, pt: (bi, 0, 1)),
            pl.BlockSpec((1, t, nb2), lambda bi, ci, pt: (bi, 0, 0)),
            pl.BlockSpec((1, t, 1024), lambda bi, ci, pt: (bi, 0, 0))],
        out_specs=pl.BlockSpec((1, t, 1024), lambda bi, ci, pt: (bi, 0, 0)),
        scratch_shapes=[pltpu.VMEM((KV_HEADS, rows, 1), F32), pltpu.VMEM((KV_HEADS, rows, 1), F32),
                        pltpu.VMEM((KV_HEADS, rows, HEAD_DIM), F32)],
    )
    return pl.pallas_call(
        functools.partial(_sel_sample_kernel, past=past),
        grid_spec=grid_spec,
        out_shape=jax.ShapeDtypeStruct((nbatch, t, 1024), F32),
        compiler_params=_params("parallel", "arbitrary"),
        name="sel_sample",
    )(page_table, *([cache_rows] * PAGES_PER_STEP), qq3, sel, nsa3)


def _win_sample_kernel(q_ref, st_ref, new_ref, o_ref):
    t = q_ref.shape[1]
    nw = st_ref.shape[1]
    qi = lax.broadcasted_iota(jnp.int32, (t, 1), 0)
    qi = jnp.concatenate([qi] * GROUP, axis=0)
    d_old = (nw + qi) - lax.broadcasted_iota(jnp.int32, (1, nw), 1)
    ok_old = (d_old >= 0) & (d_old < WINDOW)
    d_new = qi - lax.broadcasted_iota(jnp.int32, (1, NEW_PAD), 1)
    ok_new = (d_new >= 0) & (d_new < WINDOW)
    for g in range(KV_HEADS):
        q = _stack_heads(q_ref, g)
        k_old = st_ref[0, :, g * HEAD_DIM:(g + 1) * HEAD_DIM].astype(BF16)
        v_old = st_ref[0, :, (KV_HEADS + g) * HEAD_DIM:(KV_HEADS + g + 1) * HEAD_DIM].astype(BF16)
        k_new = _pad_new(new_ref[0, :, g * HEAD_DIM:(g + 1) * HEAD_DIM])
        v_new = _pad_new(new_ref[0, :, (KV_HEADS + g) * HEAD_DIM:(KV_HEADS + g + 1) * HEAD_DIM])
        p_old, p_new = _masked_softmax_parts(
            [_dot_nt(q, k_old) * SCALE, _dot_nt(q, k_new) * SCALE], [ok_old, ok_new])
        o = _dot(p_old.astype(BF16), v_old) + _dot(p_new.astype(BF16), v_new)
        for r in range(GROUP):
            hcol = (g * GROUP + r) * HEAD_DIM
            o_ref[0, :, hcol:hcol + HEAD_DIM] = o[r * t:(r + 1) * t]


def _win_sample(qq3, state_win, win3):
    b, t, _ = qq3.shape
    nw = state_win.shape[1]
    return pl.pallas_call(
        _win_sample_kernel,
        grid=(b,),
        in_specs=[pl.BlockSpec((1, t, 1024), lambda bi: (bi, 0, 1)),
                  pl.BlockSpec((1, nw, 512), lambda bi: (bi, 0, 0)),
                  pl.BlockSpec((1, t, 512), lambda bi: (bi, 0, 0))],
        out_specs=pl.BlockSpec((1, t, 1024), lambda bi: (bi, 0, 0)),
        out_shape=jax.ShapeDtypeStruct((b, t, 1024), F32),
        compiler_params=_params("parallel"),
        name="win_sample",
    )(qq3, state_win, win3)


def _rope_tables(pos):
    half = HEAD_DIM // 2
    inv = ROPE_THETA ** (-jnp.arange(half, dtype=F32) / half)
    ang = pos.astype(F32)[:, None] * inv[None, :]
    cos, sin = jnp.cos(ang), jnp.sin(ang)
    return jnp.concatenate([cos, cos], axis=-1), jnp.concatenate([-sin, sin], axis=-1)


def _prep_weights(w_in, w_phi_k, w_phi_v, w_pool_group, pool_scale, w_branch_attn, w_branch_pool, w_out,
                  peer_w_query, peer_sub_keys, peer_u, peer_v):
    d = w_in.shape[0]
    qw = N_HEADS * HEAD_DIM
    kvw = 6 * KV_HEADS * HEAD_DIM
    ngw = 3 * N_HEADS
    pw = d // 2
    o1, o2, o3, o4 = qw, qw + kvw, qw + kvw + ngw, qw + kvw + ngw + pw
    wb = w_in.astype(BF16)
    w_qkv = jnp.concatenate([wb[:, :o2], wb[:, o2:o3], jnp.zeros((d, HEAD_DIM - ngw), BF16)], axis=1)
    cat = lambda w: jnp.concatenate([w[:CMP_STRIDE], w[CMP_STRIDE:]], axis=-1).astype(BF16)
    u16 = lax.bitcast_convert_type(peer_u.astype(BF16), jnp.uint16).astype(jnp.uint32)
    v16 = lax.bitcast_convert_type(peer_v.astype(BF16), jnp.uint16).astype(jnp.uint32)
    return dict(
        w_qkv=w_qkv, w_u=wb[:, o3:o4], w_gab=wb[:, o4:],
        wk=cat(w_phi_k), wv=cat(w_phi_v),
        w_pool=w_pool_group.astype(BF16), pool_scale=pool_scale.reshape(1, -1),
        w_ba=w_branch_attn.astype(BF16), w_bp=w_branch_pool.astype(BF16), w_out=w_out.astype(BF16),
        w_query=peer_w_query.astype(BF16),
        keys=peer_sub_keys.reshape(PEER_HEADS * 2, PEER_NKEYS, -1).astype(BF16),
        table=(u16 | (v16 << 16))[:, None, :],
    )


def _token_tail(x2d, o_cmp, o_sel, o_win, gates, pool_out, gab, wp, g_ffn, g_final):
    mix = _mix(o_cmp, o_sel, o_win, gates, pool_out, gab, wp["w_ba"], wp["w_bp"])
    x2 = _out_proj(x2d, mix, wp["w_out"])
    ids, gw = _peer_score(x2, g_ffn, wp["w_query"], wp["keys"])
    return _peer_apply(ids, x2, g_ffn, gw, g_final, wp["table"])


def kernel(x_prompt, x_sample, cache_kv_nsa, state_win_kv, state_pool, page_table, g_norm_mix, w_in, w_phi_k,
           w_phi_v, w_pool_group, pool_scale, w_branch_attn, w_branch_pool, w_out, g_norm_ffn, peer_w_query,
           peer_sub_keys, peer_u, peer_v, g_norm_final):
    assert g_norm_mix.shape[0] == 1, "single-layer step"
    bp, t, d = x_prompt.shape
    bs, ts, _ = x_sample.shape
    n_pages = page_table.shape[1]
    page_size = cache_kv_nsa.shape[2]
    past = n_pages * page_size
    wp = _prep_weights(w_in[0], w_phi_k[0], w_phi_v[0], w_pool_group[0], pool_scale[0], w_branch_attn[0],
                       w_branch_pool[0], w_out[0], peer_w_query[0], peer_sub_keys[0], peer_u[0], peer_v[0])
    g_mix = g_norm_mix[0].reshape(1, d)
    g_ffn = g_norm_ffn[0].reshape(1, d)
    g_fin = g_norm_final.reshape(1, d)

    xp = x_prompt.reshape(bp * t, d)
    cos, sin = _rope_tables(jnp.arange(t))
    cos, sin = jnp.tile(cos, (bp, 1)), jnp.tile(sin, (bp, 1))
    qq, nsa, win, gates, kvb = _qkv_proj(xp, g_mix, wp["w_qkv"], cos, sin)
    u = _norm_proj(xp, g_mix, wp["w_u"])
    gab = _norm_proj(xp, g_mix, wp["w_gab"], act="sigmoid")
    qq3, nsa3, kvb3 = qq.reshape(bp, t, -1), nsa.reshape(bp, t, -1), kvb.reshape(bp, t, -1)
    y = _compress_prompt(nsa3, wp["wk"], wp["wv"])
    o_cmp, sel = _cmp_prompt(qq3, y)
    o_sel = _sel_prompt(qq3, sel, kvb3)
    o_win = _win_prompt(qq3, kvb3)
    u3 = u.reshape(bp, t, -1)
    pool_out = _pool(u3, u3, wp["w_pool"], wp["pool_scale"], base=0, zero_first_prev=True)
    y_prompt = _token_tail(xp, o_cmp.reshape(bp * t, -1), o_sel.reshape(bp * t, -1), o_win.reshape(bp * t, -1),
                           gates, pool_out.reshape(bp * t, -1), gab, wp, g_ffn, g_fin)
    wlen = min(WINDOW, t)
    new_kv_p = nsa.reshape(1, bp, t, 4, KV_HEADS, HEAD_DIM)
    new_win_p = win.reshape(bp, t, 2, KV_HEADS, HEAD_DIM)[None, :, t - wlen:]
    new_pool_p = u3[None, :, t - (POOL_PREV - 1):]

    xs = x_sample.reshape(bs * ts, d)
    cos_s, sin_s = _rope_tables(past + jnp.arange(ts))
    cos_s, sin_s = jnp.tile(cos_s, (bs, 1)), jnp.tile(sin_s, (bs, 1))
    qq_s, nsa_s, win_s, gates_s, _ = _qkv_proj(xs, g_mix, wp["w_qkv"], cos_s, sin_s)
    u_s = _norm_proj(xs, g_mix, wp["w_u"])
    gab_s = _norm_proj(xs, g_mix, wp["w_gab"], act="sigmoid")
    qq_s3, nsa_s3, win_s3 = qq_s.reshape(bs, ts, -1), nsa_s.reshape(bs, ts, -1), win_s.reshape(bs, ts, -1)
    cache_rows = cache_kv_nsa.reshape(-1, HEAD_DIM)
    y_s = _compress_pages(cache_rows, page_table, page_size, wp["wk"], wp["wv"])
    o_cmp_s, sel_s = _cmp_sample(qq_s3, y_s, past=past)
    o_sel_s = _sel_sample(cache_rows, page_table, page_size, qq_s3, sel_s, nsa_s3, past=past)
    st_win = state_win_kv[0].reshape(bs, -1, 2 * KV_HEADS * HEAD_DIM)
    o_win_s = _win_sample(qq_s3, st_win, win_s3)
    u_s3 = u_s.reshape(bs, ts, -1)
    st_pool = state_pool[0]
    prev = jnp.pad(st_pool, ((0, 0), (POOL_PREV - st_pool.shape[1], 0), (0, 0)))
    pool_out_s = _pool(prev, u_s3, wp["w_pool"], wp["pool_scale"], base=st_pool.shape[1], zero_first_prev=False)
    y_sample = _token_tail(xs, o_cmp_s.reshape(bs * ts, -1), o_sel_s.reshape(bs * ts, -1),
                           o_win_s.reshape(bs * ts, -1), gates_s, pool_out_s.reshape(bs * ts, -1), gab_s, wp,
                           g_ffn, g_fin)
    new_kv_s = nsa_s.reshape(1, bs, ts, 4, KV_HEADS, HEAD_DIM)
    win_ext = jnp.concatenate([st_win, win_s3], axis=1)
    wlen_s = min(WINDOW, win_ext.shape[1])
    new_win_s = win_ext[:, win_ext.shape[1] - wlen_s:].reshape(1, bs, wlen_s, 2, KV_HEADS, HEAD_DIM)
    pool_ext = jnp.concatenate([st_pool, u_s3], axis=1)
    new_pool_s = pool_ext[None, :, pool_ext.shape[1] - (POOL_PREV - 1):]

    return (y_prompt.reshape(bp, t, d), y_sample.reshape(bs, ts, d), new_kv_p, new_kv_s, new_win_p, new_win_s,
            new_pool_p, new_pool_s)
```

```python
import functools

import jax
import jax.numpy as jnp
import numpy as np
from jax import lax
from jax.experimental import pallas as pl
from jax.experimental.pallas import tpu as pltpu

F32 = jnp.float32
BF16 = jnp.bfloat16

HEAD_DIM = 128
N_HEADS = 8
KV_HEADS = 2
GROUP = N_HEADS // KV_HEADS
CMP_LEN = 32
CMP_STRIDE = 16
SEL_BLOCK = 64
SEL_TOPN = 16
CMP_PER_SEL = SEL_BLOCK // CMP_STRIDE
WINDOW = 512
Q_BLOCK = 128
ROPE_THETA = 10000.0
FORCE_SCORE = 1e4
POOL_WINDOWS = (2, 4, 8, 16)
POOL_PREV = 16
PEER_HEADS = 8
PEER_NKEYS = 128
PEER_TOPK = 16
EPS = 1e-6
SCALE = HEAD_DIM ** -0.5
LOG2E = 1.4426950408889634
NEG_INF = float("-inf")

ROWS_PER_POS = 4 * KV_HEADS
WIN_ROWS_PER_POS = 2 * KV_HEADS

VMEM_LIMIT = 56 * 1024 * 1024


def _params(*sem):
    return pltpu.CompilerParams(dimension_semantics=sem, vmem_limit_bytes=VMEM_LIMIT)


def _dot(a, b):
    return jnp.dot(a, b, preferred_element_type=F32)


def _dot_nt(a, b):
    return lax.dot_general(a, b, (((1,), (1,)), ((), ())), preferred_element_type=F32)


def _rms(x, g):
    return x * lax.rsqrt(jnp.mean(x * x, axis=-1, keepdims=True) + EPS) * g


def _masked_softmax_parts(parts, masks):
    parts = [jnp.where(mk, s, NEG_INF) for s, mk in zip(parts, masks)]
    m = functools.reduce(jnp.maximum, [jnp.max(s, axis=-1, keepdims=True) for s in parts])
    m = jnp.where(m == NEG_INF, 0.0, m)
    es = [jnp.exp(s - m) for s in parts]
    den = functools.reduce(jnp.add, [jnp.sum(e, axis=-1, keepdims=True) for e in es])
    inv = 1.0 / jnp.maximum(den, 1e-30)
    return [e * inv for e in es]


def _topk_select(v, n, axis):
    size = v.shape[axis]
    idx = lax.broadcasted_iota(jnp.int32, v.shape, axis).astype(F32)
    sel = jnp.zeros(v.shape, F32)
    for _ in range(n):
        m = jnp.max(v, axis=axis, keepdims=True)
        first = jnp.min(jnp.where(v == m, idx, float(size)), axis=axis, keepdims=True)
        hit = idx == first
        v = jnp.where(hit, NEG_INF, v)
        sel = jnp.where(hit, 1.0, sel)
    return sel


def _topk_sorted(v, n, payload=None):
    size = v.shape[0]
    idx = lax.broadcasted_iota(jnp.int32, v.shape, 0).astype(F32)
    vals, picks = [], []
    for _ in range(n):
        m = jnp.max(v, axis=0, keepdims=True)
        first = jnp.min(jnp.where(v == m, idx, float(size)), axis=0, keepdims=True)
        hit = idx == first
        vals.append(m)
        if payload is None:
            picks.append(first)
        else:
            picks.append(jnp.max(jnp.where(hit, payload, -1.0), axis=0, keepdims=True))
        v = jnp.where(hit, NEG_INF, v)
    return jnp.concatenate(vals, axis=0), jnp.concatenate(picks, axis=0)


def _proj_kernel(x_ref, g_ref, w_ref, o_ref, *, act, tn):
    h = _rms(x_ref[...], g_ref[...]).astype(BF16)
    for c in range(w_ref.shape[1] // tn):
        z = _dot(h, w_ref[:, c * tn:(c + 1) * tn])
        if act == "sigmoid":
            z = jax.nn.sigmoid(z)
        o_ref[:, c * tn:(c + 1) * tn] = z.astype(o_ref.dtype)


def _norm_proj(x, g, w, *, act=None, out_dtype=F32, tm=256, tn=512, wn=2048):
    m, d = x.shape
    n = w.shape[1]
    tm = min(tm, m)
    wn = min(wn, n)
    return pl.pallas_call(
        functools.partial(_proj_kernel, act=act, tn=tn),
        grid=(n // wn, m // tm),
        in_specs=[pl.BlockSpec((tm, d), lambda j, i: (i, 0)),
                  pl.BlockSpec((1, d), lambda j, i: (0, 0)),
                  pl.BlockSpec((d, wn), lambda j, i: (0, j))],
        out_specs=pl.BlockSpec((tm, wn), lambda j, i: (i, j)),
        out_shape=jax.ShapeDtypeStruct((m, n), out_dtype),
        compiler_params=_params("parallel", "parallel"),
        name="norm_proj",
    )(x, g, w)


QKV_COLS = N_HEADS * HEAD_DIM + 6 * KV_HEADS * HEAD_DIM + HEAD_DIM


def _qkv_kernel(x_ref, g_ref, w_ref, cos_ref, sin_ref, qq_ref, nsa_ref, win_ref, gate_ref, kvb_ref):
    h = _rms(x_ref[...], g_ref[...]).astype(BF16)
    cos = cos_ref[...]
    sin = sin_ref[...]

    def rope(z):
        return z * cos + pltpu.roll(z, HEAD_DIM // 2, 1) * sin

    qw = N_HEADS * HEAD_DIM
    for c in range(qw // 512):
        z = _dot(h, w_ref[:, c * 512:(c + 1) * 512])
        for j in range(4):
            zh = z[:, j * 128:(j + 1) * 128]
            col = c * 512 + j * 128
            qq_ref[:, col:col + 128] = zh.astype(BF16)
            qq_ref[:, qw + col:qw + col + 128] = rope(zh).astype(BF16)
    tm = x_ref.shape[0]

    def put(ref, comp, n_comp, val):
        ref[pl.ds(comp, tm, stride=n_comp), :] = val

    for c in range(3):
        z = _dot(h, w_ref[:, qw + c * 512:qw + (c + 1) * 512])
        for j in range(2):
            zk = z[:, j * 128:(j + 1) * 128]
            zv = z[:, 256 + j * 128:256 + (j + 1) * 128]
            if c == 0:
                put(nsa_ref, j, ROWS_PER_POS, zk)
                put(nsa_ref, KV_HEADS + j, ROWS_PER_POS, zv)
                continue
            zr = rope(zk)
            if c == 1:
                put(nsa_ref, 2 * KV_HEADS + j, ROWS_PER_POS, zr)
                put(nsa_ref, 3 * KV_HEADS + j, ROWS_PER_POS, zv)
            else:
                put(win_ref, j, WIN_ROWS_PER_POS, zr)
                put(win_ref, KV_HEADS + j, WIN_ROWS_PER_POS, zv)
            base = (c - 1) * 512
            kvb_ref[:, base + j * 128:base + (j + 1) * 128] = zr.astype(BF16)
            kvb_ref[:, base + 256 + j * 128:base + 256 + (j + 1) * 128] = zv.astype(BF16)
    z = _dot(h, w_ref[:, qw + 1536:qw + 1536 + 128])
    gate_ref[...] = jax.nn.sigmoid(z)


def _qkv_proj(x, g, w, cos, sin, *, tm=256):
    m, d = x.shape
    tm = min(tm, m)
    row = lambda i: (i, 0)
    const = lambda i: (0, 0)
    return pl.pallas_call(
        _qkv_kernel,
        grid=(m // tm,),
        in_specs=[pl.BlockSpec((tm, d), row), pl.BlockSpec((1, d), const),
                  pl.BlockSpec((d, QKV_COLS), const),
                  pl.BlockSpec((tm, HEAD_DIM), row), pl.BlockSpec((tm, HEAD_DIM), row)],
        out_specs=[pl.BlockSpec((tm, 2048), row), pl.BlockSpec((tm * ROWS_PER_POS, HEAD_DIM), row),
                   pl.BlockSpec((tm * WIN_ROWS_PER_POS, HEAD_DIM), row), pl.BlockSpec((tm, 128), row),
                   pl.BlockSpec((tm, 1024), row)],
        out_shape=[jax.ShapeDtypeStruct((m, 2048), BF16), jax.ShapeDtypeStruct((m * ROWS_PER_POS, HEAD_DIM), F32),
                   jax.ShapeDtypeStruct((m * WIN_ROWS_PER_POS, HEAD_DIM), F32), jax.ShapeDtypeStruct((m, 128), F32),
                   jax.ShapeDtypeStruct((m, 1024), BF16)],
        compiler_params=_params("parallel"),
        name="qkv_proj",
    )(x, g, w, cos, sin)


SUB_ROWS = CMP_STRIDE * ROWS_PER_POS


def _compress_rows(load, n_rows, wk_ref, wv_ref, y_ref):
    for kv in range(2):
        w_ref = wk_ref if kv == 0 else wv_ref
        for g in range(KV_HEADS):
            comp = kv * KV_HEADS + g
            acc = jnp.zeros((n_rows, 2 * HEAD_DIM), F32)
            for i in range(CMP_STRIDE):
                acc = acc + _dot(load(i, comp).astype(BF16), w_ref[i])
            y_ref[0, :, 2 * comp * HEAD_DIM:2 * (comp + 1) * HEAD_DIM] = acc


def _compress_prompt_kernel(x_ref, wk_ref, wv_ref, y_ref):
    ts = x_ref.shape[0] // SUB_ROWS
    load = lambda i, comp: x_ref[pl.ds(i * ROWS_PER_POS + comp, ts, stride=SUB_ROWS), :]
    _compress_rows(load, ts, wk_ref, wv_ref, y_ref)


def _compress_prompt(nsa_rows, b, wk, wv, *, ts=128):
    n_sub = nsa_rows.shape[0] // (b * SUB_ROWS)
    ts = min(ts, n_sub)
    steps = n_sub // ts
    wspec = pl.BlockSpec((CMP_STRIDE, HEAD_DIM, 2 * HEAD_DIM), lambda bi, ci: (0, 0, 0))
    return pl.pallas_call(
        _compress_prompt_kernel,
        grid=(b, steps),
        in_specs=[pl.BlockSpec((ts * SUB_ROWS, HEAD_DIM), lambda bi, ci: (bi * steps + ci, 0)), wspec, wspec],
        out_specs=pl.BlockSpec((1, ts, 1024), lambda bi, ci: (bi, ci, 0)),
        out_shape=jax.ShapeDtypeStruct((b, n_sub, 1024), F32),
        compiler_params=_params("parallel", "parallel"),
        name="compress_prompt",
    )(nsa_rows, wk, wv)


PAGES_PER_STEP = 16


def _compress_pages_kernel(pt_ref, *refs):
    pages = refs[:PAGES_PER_STEP]
    wk_ref, wv_ref, y_ref, kvs_ref = refs[PAGES_PER_STEP:]
    per_page = pages[0].shape[0] // SUB_ROWS
    page_size = pages[0].shape[0] // ROWS_PER_POS

    def load(i, comp):
        rows = pl.ds(i * ROWS_PER_POS + comp, per_page, stride=SUB_ROWS)
        return jnp.concatenate([p[rows, :] for p in pages], axis=0)

    _compress_rows(load, PAGES_PER_STEP * per_page, wk_ref, wv_ref, y_ref)
    for j in range(2 * KV_HEADS):
        rows = pl.ds(2 * KV_HEADS + j, page_size, stride=ROWS_PER_POS)
        kvs_ref[0, :, j * HEAD_DIM:(j + 1) * HEAD_DIM] = jnp.concatenate(
            [p[rows, :] for p in pages], axis=0).astype(BF16)


def _page_specs(page_rows):
    def spec(k):
        return pl.BlockSpec((page_rows, HEAD_DIM), lambda bi, ci, pt: (pt[bi, ci * PAGES_PER_STEP + k], 0))
    return [spec(k) for k in range(PAGES_PER_STEP)]


def _compress_pages(cache_rows, page_table, page_size, wk, wv):
    nb, n_pages = page_table.shape
    per_page = page_size // CMP_STRIDE
    rows = PAGES_PER_STEP * per_page
    const3 = lambda bi, ci, pt: (0, 0, 0)
    grid_spec = pltpu.PrefetchScalarGridSpec(
        num_scalar_prefetch=1,
        grid=(nb, n_pages // PAGES_PER_STEP),
        in_specs=_page_specs(page_size * ROWS_PER_POS) + [
            pl.BlockSpec((CMP_STRIDE, HEAD_DIM, 2 * HEAD_DIM), const3),
            pl.BlockSpec((CMP_STRIDE, HEAD_DIM, 2 * HEAD_DIM), const3)],
        out_specs=[pl.BlockSpec((1, rows, 1024), lambda bi, ci, pt: (bi, ci, 0)),
                   pl.BlockSpec((1, PAGES_PER_STEP * page_size, 512), lambda bi, ci, pt: (bi, ci, 0))],
    )
    return pl.pallas_call(
        _compress_pages_kernel,
        grid_spec=grid_spec,
        out_shape=[jax.ShapeDtypeStruct((nb, n_pages * per_page, 1024), F32),
                   jax.ShapeDtypeStruct((nb, n_pages * page_size, 512), BF16)],
        compiler_params=_params("parallel", "parallel"),
        name="compress_pages",
    )(page_table, *([cache_rows] * PAGES_PER_STEP), wk, wv)


def _combine_compressed(y_ref, kcp_ref, tmp_ref):
    ns = y_ref.shape[1]
    nb = ns // CMP_PER_SEL
    last = lax.broadcasted_iota(jnp.int32, (ns, 1), 0) == ns - 1
    for a in range(2 * KV_HEADS):
        y1 = y_ref[0, :, a * 256:a * 256 + 128]
        y2 = y_ref[0, :, a * 256 + 128:(a + 1) * 256]
        nxt = jnp.where(last, 0.0, pltpu.roll(y2, ns - 1, 0))
        tmp_ref[...] = y1 + nxt
        for c in range(CMP_PER_SEL):
            kcp_ref[a * CMP_PER_SEL + c] = tmp_ref[pl.ds(c, nb, stride=CMP_PER_SEL), :].astype(BF16)


def _cmp_attention(q_rows, pos, kcp_ref, g, rows_per_tok):
    nb = kcp_ref.shape[1]
    blk = lax.broadcasted_iota(jnp.int32, (1, nb), 1)
    masks = [(SEL_BLOCK * blk + CMP_STRIDE * c + CMP_LEN - 1) <= pos for c in range(CMP_PER_SEL)]
    outs = []
    imp = None
    for qh in q_rows:
        s = [_dot_nt(qh, kcp_ref[g * CMP_PER_SEL + c]) * SCALE for c in range(CMP_PER_SEL)]
        p = _masked_softmax_parts(s, masks)
        o = functools.reduce(jnp.add, [
            _dot(p[c].astype(BF16), kcp_ref[(KV_HEADS + g) * CMP_PER_SEL + c]) for c in range(CMP_PER_SEL)])
        outs.append(o)
        ps = functools.reduce(jnp.add, p)
        imp = ps if imp is None else imp + ps
    return outs, imp


def _cmp_prompt_kernel(q_ref, y_ref, o_ref, sel_ref, kcp_ref, tmp_ref):
    i = pl.program_id(1)

    @pl.when(i == 0)
    def _():
        _combine_compressed(y_ref, kcp_ref, tmp_ref)

    nb = kcp_ref.shape[1]
    pos = i * Q_BLOCK + lax.broadcasted_iota(jnp.int32, (Q_BLOCK, 1), 0)
    blk = lax.broadcasted_iota(jnp.int32, (1, nb), 1)
    cur = pos // SEL_BLOCK
    forced = (blk == cur) | (blk == 0)
    causal = blk <= cur
    for g in range(KV_HEADS):
        q_rows = [q_ref[0, :, (g * GROUP + r) * HEAD_DIM:(g * GROUP + r + 1) * HEAD_DIM] for r in range(GROUP)]
        outs, imp = _cmp_attention(q_rows, pos, kcp_ref, g, 1)
        for r in range(GROUP):
            hcol = (g * GROUP + r) * HEAD_DIM
            o_ref[0, :, hcol:hcol + HEAD_DIM] = outs[r]
        v = jnp.where(forced, FORCE_SCORE, jnp.where(causal, imp, -1.0))
        sel_t = _topk_select(v.T, SEL_TOPN, 0)
        sel = jnp.where(causal, sel_t.T, 0.0)
        sel_ref[0, :, g * nb:(g + 1) * nb] = sel.astype(BF16)


def _cmp_prompt(qq3, y):
    b, t, _ = qq3.shape
    ns = y.shape[1]
    nb = ns // CMP_PER_SEL
    return pl.pallas_call(
        _cmp_prompt_kernel,
        grid=(b, t // Q_BLOCK),
        in_specs=[pl.BlockSpec((1, Q_BLOCK, 1024), lambda bi, i: (bi, i, 0)),
                  pl.BlockSpec((1, ns, 1024), lambda bi, i: (bi, 0, 0))],
        out_specs=[pl.BlockSpec((1, Q_BLOCK, 1024), lambda bi, i: (bi, i, 0)),
                   pl.BlockSpec((1, Q_BLOCK, KV_HEADS * nb), lambda bi, i: (bi, i, 0))],
        out_shape=[jax.ShapeDtypeStruct((b, t, 1024), F32),
                   jax.ShapeDtypeStruct((b, t, KV_HEADS * nb), BF16)],
        scratch_shapes=[pltpu.VMEM((2 * KV_HEADS * CMP_PER_SEL, nb, HEAD_DIM), BF16),
                        pltpu.VMEM((ns, HEAD_DIM), F32)],
        compiler_params=_params("parallel", "arbitrary"),
        name="cmp_prompt",
    )(qq3, y)


SEL_CHUNK = 512


def _expand_blocks(sel, first_block, width):
    nb = sel.shape[1]
    b = lax.broadcasted_iota(jnp.int32, (nb, width), 0)
    t = lax.broadcasted_iota(jnp.int32, (nb, width), 1)
    e = jnp.where(b == first_block + t // SEL_BLOCK, 1.0, 0.0).astype(BF16)
    return _dot(sel, e)


def _sel_prompt_kernel(q_ref, sel_ref, kv_ref, o_ref):
    i = pl.program_id(1)
    nb = sel_ref.shape[2] // KV_HEADS
    pos = i * Q_BLOCK + lax.broadcasted_iota(jnp.int32, (Q_BLOCK, 1), 0)
    n_chunks = (i * Q_BLOCK + Q_BLOCK + SEL_CHUNK - 1) // SEL_CHUNK
    lane = lax.broadcasted_iota(jnp.int32, (1, SEL_CHUNK), 1)
    rows = GROUP * Q_BLOCK
    for g in range(KV_HEADS):
        q = jnp.concatenate(
            [q_ref[0, :, (g * GROUP + r) * HEAD_DIM:(g * GROUP + r + 1) * HEAD_DIM] for r in range(GROUP)], axis=0)
        sel = sel_ref[0, :, g * nb:(g + 1) * nb]

        def body(c, carry):
            m, l, acc = carry
            start = pl.multiple_of(c * SEL_CHUNK, SEL_CHUNK)
            k = kv_ref[0, pl.ds(start, SEL_CHUNK), g * HEAD_DIM:(g + 1) * HEAD_DIM]
            v = kv_ref[0, pl.ds(start, SEL_CHUNK), (KV_HEADS + g) * HEAD_DIM:(KV_HEADS + g + 1) * HEAD_DIM]
            picked = _expand_blocks(sel, c * (SEL_CHUNK // SEL_BLOCK), SEL_CHUNK)
            bias = jnp.where(picked > 0.5, jnp.where((start + lane) <= pos, 0.0, NEG_INF), NEG_INF)
            s = _dot_nt(q, k) * (SCALE * LOG2E) + jnp.concatenate([bias] * GROUP, axis=0)
            m_new = jnp.maximum(m, jnp.max(s, axis=-1, keepdims=True))
            m_safe = jnp.where(m_new == NEG_INF, 0.0, m_new)
            p = jnp.exp2(s - m_safe)
            alpha = jnp.exp2(m - m_safe)
            l = alpha * l + jnp.sum(p, axis=-1, keepdims=True)
            acc = alpha * acc + _dot(p.astype(BF16), v)
            return m_new, l, acc

        init = (jnp.full((rows, 1), NEG_INF, F32), jnp.zeros((rows, 1), F32), jnp.zeros((rows, HEAD_DIM), F32))
        m, l, acc = lax.fori_loop(0, n_chunks, body, init)
        o = acc * (1.0 / jnp.maximum(l, 1e-30))
        for r in range(GROUP):
            hcol = (g * GROUP + r) * HEAD_DIM
            o_ref[0, :, hcol:hcol + HEAD_DIM] = o[r * Q_BLOCK:(r + 1) * Q_BLOCK]


def _sel_prompt(qq3, sel, kvb3):
    b, t, _ = qq3.shape
    nb2 = sel.shape[2]
    return pl.pallas_call(
        _sel_prompt_kernel,
        grid=(b, t // Q_BLOCK),
        in_specs=[pl.BlockSpec((1, Q_BLOCK, 1024), lambda bi, i: (bi, i, 1)),
                  pl.BlockSpec((1, Q_BLOCK, nb2), lambda bi, i: (bi, i, 0)),
                  pl.BlockSpec((1, t, 512), lambda bi, i: (bi, 0, 0))],
        out_specs=pl.BlockSpec((1, Q_BLOCK, 1024), lambda bi, i: (bi, i, 0)),
        out_shape=jax.ShapeDtypeStruct((b, t, 1024), F32),
        compiler_params=_params("parallel", "arbitrary"),
        name="sel_prompt",
    )(qq3, sel, kvb3)


def _win_prompt_kernel(q_ref, kv_ref, o_ref, *, span):
    i = pl.program_id(1)
    pos = i * Q_BLOCK + lax.broadcasted_iota(jnp.int32, (Q_BLOCK, 1), 0)
    pos = jnp.concatenate([pos] * GROUP, axis=0)
    start = pl.multiple_of(jnp.maximum(i * Q_BLOCK + Q_BLOCK - span, 0), Q_BLOCK)
    kpos = start + lax.broadcasted_iota(jnp.int32, (1, span), 1)
    diff = pos - kpos
    ok = (diff >= 0) & (diff < WINDOW)
    for g in range(KV_HEADS):
        q = jnp.concatenate(
            [q_ref[0, :, (g * GROUP + r) * HEAD_DIM:(g * GROUP + r + 1) * HEAD_DIM] for r in range(GROUP)], axis=0)
        k = kv_ref[0, pl.ds(start, span), g * HEAD_DIM:(g + 1) * HEAD_DIM]
        v = kv_ref[0, pl.ds(start, span), (KV_HEADS + g) * HEAD_DIM:(KV_HEADS + g + 1) * HEAD_DIM]
        s = _dot_nt(q, k) * SCALE
        (p,) = _masked_softmax_parts([s], [ok])
        o = _dot(p.astype(BF16), v)
        for r in range(GROUP):
            hcol = (g * GROUP + r) * HEAD_DIM
            o_ref[0, :, hcol:hcol + HEAD_DIM] = o[r * Q_BLOCK:(r + 1) * Q_BLOCK]


def _win_prompt(qq3, kvb3):
    b, t, _ = qq3.shape
    span = min(WINDOW + Q_BLOCK, t)
    return pl.pallas_call(
        functools.partial(_win_prompt_kernel, span=span),
        grid=(b, t // Q_BLOCK),
        in_specs=[pl.BlockSpec((1, Q_BLOCK, 1024), lambda bi, i: (bi, i, 1)),
                  pl.BlockSpec((1, t, 512), lambda bi, i: (bi, 0, 1))],
        out_specs=pl.BlockSpec((1, Q_BLOCK, 1024), lambda bi, i: (bi, i, 0)),
        out_shape=jax.ShapeDtypeStruct((b, t, 1024), F32),
        compiler_params=_params("parallel", "arbitrary"),
        name="win_prompt",
    )(qq3, kvb3)


def _pool_kernel(prev_ref, u_ref, w_ref, sc_ref, o_ref, *, base, zero_first_prev):
    i = pl.program_id(1)
    tq = u_ref.shape[1]
    cur = u_ref[0]
    prev = prev_ref[0]
    if zero_first_prev:
        prev = jnp.where(i == 0, 0.0, prev)
    ext = jnp.concatenate([prev, cur], axis=0)
    gpos = base + i * tq + lax.broadcasted_iota(jnp.int32, (tq, 1), 0)
    gw = ext.shape[1] // len(POOL_WINDOWS)
    for gi, w in enumerate(POOL_WINDOWS):
        s = ext[:, gi * gw:(gi + 1) * gw]
        span = 1
        while span < w:
            s = s + pltpu.roll(s, span, 0)
            span *= 2
        cnt = jnp.minimum(gpos + 1, w).astype(F32)
        d = s[POOL_PREV:] / cnt - cur[:, gi * gw:(gi + 1) * gw]
        o = _dot(d.astype(BF16), w_ref[gi]) * sc_ref[:, gi * gw:(gi + 1) * gw]
        o_ref[0, :, gi * gw:(gi + 1) * gw] = o


def _pool(prev, u3, w, scale, *, base, zero_first_prev, tq=512):
    b, t, c = u3.shape
    tq = min(tq, t)
    ratio = tq // POOL_PREV
    if zero_first_prev:
        prev_map = lambda bi, i: (bi, jnp.maximum(i * ratio - 1, 0), 0)
    else:
        prev_map = lambda bi, i: (bi, 0, 0)
    ng = len(POOL_WINDOWS)
    return pl.pallas_call(
        functools.partial(_pool_kernel, base=base, zero_first_prev=zero_first_prev),
        grid=(b, t // tq),
        in_specs=[pl.BlockSpec((1, POOL_PREV, c), prev_map),
                  pl.BlockSpec((1, tq, c), lambda bi, i: (bi, i, 0)),
                  pl.BlockSpec((ng, c // ng, c // ng), lambda bi, i: (0, 0, 0)),
                  pl.BlockSpec((1, c), lambda bi, i: (0, 0))],
        out_specs=pl.BlockSpec((1, tq, c), lambda bi, i: (bi, i, 0)),
        out_shape=jax.ShapeDtypeStruct((b, t, c), F32),
        compiler_params=_params("parallel", "parallel"),
        name="pool_mix",
    )(prev, u3, w, scale)


def _mix_kernel(oc_ref, os_ref, ow_ref, gt_ref, po_ref, ga_ref, gb_ref, wa_ref, wp_ref, mix_ref):
    gt = gt_ref[...]
    cols = []
    for h in range(N_HEADS):
        sl = slice(h * HEAD_DIM, (h + 1) * HEAD_DIM)
        o = (gt[:, h:h + 1] * oc_ref[:, sl] + gt[:, N_HEADS + h:N_HEADS + h + 1] * os_ref[:, sl]
             + gt[:, 2 * N_HEADS + h:2 * N_HEADS + h + 1] * ow_ref[:, sl])
        cols.append(o.astype(BF16))
    a = _dot(jnp.concatenate(cols, axis=1), wa_ref[...])
    p = _dot(po_ref[...].astype(BF16), wp_ref[...])
    mix_ref[...] = (ga_ref[...] * a + gb_ref[...] * p).astype(BF16)


def _mix(oc, os_, ow, gt, po, gab, wa, wp, *, tm=256):
    m = oc.shape[0]
    d = wa.shape[1]
    tm = min(tm, m)
    row = lambda i: (i, 0)
    const = lambda i: (0, 0)
    return pl.pallas_call(
        _mix_kernel,
        grid=(m // tm,),
        in_specs=[pl.BlockSpec((tm, 1024), row), pl.BlockSpec((tm, 1024), row), pl.BlockSpec((tm, 1024), row),
                  pl.BlockSpec((tm, 128), row), pl.BlockSpec((tm, 1024), row),
                  pl.BlockSpec((tm, d), lambda i: (i, 0)), pl.BlockSpec((tm, d), lambda i: (i, 1)),
                  pl.BlockSpec((1024, d), const), pl.BlockSpec((1024, d), const)],
        out_specs=pl.BlockSpec((tm, d), row),
        out_shape=jax.ShapeDtypeStruct((m, d), BF16),
        compiler_params=_params("parallel"),
        name="branch_mix",
    )(oc, os_, ow, gt, po, gab, gab, wa, wp)


def _out_kernel(x_ref, mix_ref, w_ref, o_ref):
    o_ref[...] = x_ref[...] + _dot(mix_ref[...], w_ref[...])


def _out_proj(x, mix, w, *, tm=256):
    m, d = x.shape
    tm = min(tm, m)
    return pl.pallas_call(
        _out_kernel,
        grid=(m // tm,),
        in_specs=[pl.BlockSpec((tm, d), lambda i: (i, 0)), pl.BlockSpec((tm, d), lambda i: (i, 0)),
                  pl.BlockSpec((d, d), lambda i: (0, 0))],
        out_specs=pl.BlockSpec((tm, d), lambda i: (i, 0)),
        out_shape=jax.ShapeDtypeStruct((m, d), F32),
        compiler_params=_params("parallel"),
        name="out_proj",
    )(x, mix, w)


def _pair_candidates(sv, si):
    k = PEER_TOPK
    row = lax.broadcasted_iota(jnp.int32, (8, 1), 0)
    vals, ids = [], []
    for a in range(k // 2):
        lim = k // (a + 1)
        nrow = k if a == 0 else 8
        v = sv[0][a:a + 1] + sv[1][:nrow]
        if lim < nrow:
            v = jnp.where(row < lim, v, NEG_INF)
        vals.append(v)
        ids.append(si[0][a:a + 1] * float(PEER_NKEYS) + si[1][:nrow])
    vals.append(sv[0][k // 2:] + sv[1][0:1])
    ids.append(si[0][k // 2:] * float(PEER_NKEYS) + si[1][0:1])
    return jnp.concatenate(vals, axis=0), jnp.concatenate(ids, axis=0)


def _peer_score_kernel(x_ref, g_ref, wq_ref, keys_ref, ids_ref, gw_ref):
    h = _rms(x_ref[...], g_ref[...]).astype(BF16)
    q = _dot(h, wq_ref[...]).astype(BF16)
    dk = PEER_NKEYS
    ids, gws = [], []
    for hd in range(PEER_HEADS):
        sv, si = [], []
        for c in range(2):
            qhc = q[:, (hd * 2 + c) * dk:(hd * 2 + c + 1) * dk]
            st = _dot_nt(keys_ref[hd * 2 + c], qhc)
            v, ix = _topk_sorted(st, PEER_TOPK)
            sv.append(v)
            si.append(ix)
        comb, eid = _pair_candidates(sv, si)
        cv, ce = _topk_sorted(comb, PEER_TOPK, payload=eid)
        e = jnp.exp(cv - cv[0:1])
        gws.append(e / jnp.sum(e, axis=0, keepdims=True))
        ids.append(ce)
    ids_ref[...] = jnp.concatenate(ids, axis=0).T.astype(jnp.int32)
    gw_ref[...] = jnp.concatenate(gws, axis=0)


def _peer_score(x, g, wq, keys, *, tb=128):
    m, d = x.shape
    nk = PEER_HEADS * PEER_TOPK
    return pl.pallas_call(
        _peer_score_kernel,
        grid=(m // tb,),
        in_specs=[pl.BlockSpec((tb, d), lambda i: (i, 0)), pl.BlockSpec((1, d), lambda i: (0, 0)),
                  pl.BlockSpec(wq.shape, lambda i: (0, 0)),
                  pl.BlockSpec(keys.shape, lambda i: (0, 0, 0))],
        out_specs=[pl.BlockSpec((tb, nk), lambda i: (i, 0)), pl.BlockSpec((nk, tb), lambda i: (0, i))],
        out_shape=[jax.ShapeDtypeStruct((m, nk), jnp.int32), jax.ShapeDtypeStruct((nk, m), F32)],
        compiler_params=_params("parallel"),
        name="peer_score",
    )(x, g, wq, keys)


def _gelu(x):
    return 0.5 * x * (1.0 + lax.erf(x * (2.0 ** -0.5)))


PEER_RING = 8


def _peer_apply_kernel(ids_ref, nxt_ref, x_ref, gffn_ref, gw_ref, gfin_ref, tab_ref, y_ref, *scratch):
    bufs = scratch[:PEER_RING]
    sem, hbuf, obuf = scratch[PEER_RING:]
    step = pl.program_id(0)
    tb, d = x_ref.shape
    nk = gw_ref.shape[0]
    nchunk = d // 128
    ahead = PEER_RING - 1
    n_groups = tb // PEER_RING
    hbuf[...] = _rms(x_ref[...], gffn_ref[...])

    def row_copy(ids, t, k, slot):
        return pltpu.make_async_copy(tab_ref.at[ids[t, k]], bufs[slot].at[pl.ds(k, 1), :], sem.at[slot])

    def issue(ids, t, slot):
        for k in range(nk):
            row_copy(ids, t, k, slot).start(priority=k % 2)

    def drain(slot):
        for k in range(nk):
            row_copy(ids_ref, 0, k, slot).wait()

    @pl.when(step == 0)
    def _():
        for s in range(ahead):
            issue(ids_ref, s, s)

    lane = lax.broadcasted_iota(jnp.int32, (nk, tb), 1)

    def compute(t, slot):
        buf = bufs[slot]
        hrow = hbuf[pl.ds(t, 1), :]
        acc = jnp.zeros((nk, 128), F32)
        for j in range(nchunk):
            w = buf[:, j * 128:(j + 1) * 128]
            u = pltpu.bitcast(w << 16, F32)
            acc = acc + u * hrow[:, j * 128:(j + 1) * 128]
        act = jnp.sum(acc, axis=1, keepdims=True)
        gcol = jnp.sum(jnp.where(lane == t, gw_ref[...], 0.0), axis=1, keepdims=True)
        coef = _gelu(act) * gcol
        outs = []
        for j in range(nchunk):
            w = buf[:, j * 128:(j + 1) * 128]
            v = pltpu.bitcast(w & jnp.uint32(0xFFFF0000), F32)
            outs.append(jnp.sum(v * coef, axis=0, keepdims=True))
        obuf[pl.ds(t, 1), :] = jnp.concatenate(outs, axis=1)

    def group(p, last):
        for s in range(PEER_RING):
            t = p * PEER_RING + s
            drain(s)
            if last and s > 0:
                issue(nxt_ref, s - 1, (s + ahead) % PEER_RING)
            else:
                issue(ids_ref, t + ahead, (s + ahead) % PEER_RING)
            compute(t, s)

    def body(p, carry):
        group(p, False)
        return carry

    lax.fori_loop(0, n_groups - 1, body, 0)
    group(n_groups - 1, True)

    @pl.when(step == pl.num_programs(0) - 1)
    def _():
        for s in range(ahead):
            drain(s)

    y_ref[...] = _rms(x_ref[...] + obuf[...], gfin_ref[...])


def _peer_apply(ids, x, gffn, gw, gfin, table, *, tb=128):
    m, d = x.shape
    nk = ids.shape[1]
    steps = m // tb
    return pl.pallas_call(
        _peer_apply_kernel,
        grid=(steps,),
        in_specs=[pl.BlockSpec((tb, nk), lambda i: (i, 0), memory_space=pltpu.SMEM),
                  pl.BlockSpec((tb, nk), lambda i: (jnp.minimum(i + 1, steps - 1), 0), memory_space=pltpu.SMEM),
                  pl.BlockSpec((tb, d), lambda i: (i, 0)),
                  pl.BlockSpec((1, d), lambda i: (0, 0)),
                  pl.BlockSpec((nk, tb), lambda i: (0, i)),
                  pl.BlockSpec((1, d), lambda i: (0, 0)),
                  pl.BlockSpec(memory_space=pl.ANY)],
        out_specs=pl.BlockSpec((tb, d), lambda i: (i, 0)),
        out_shape=jax.ShapeDtypeStruct((m, d), F32),
        scratch_shapes=[pltpu.VMEM((nk, d), jnp.uint32)] * PEER_RING + [
            pltpu.SemaphoreType.DMA((PEER_RING,)), pltpu.VMEM((tb, d), F32), pltpu.VMEM((tb, d), F32)],
        compiler_params=_params("arbitrary"),
        name="peer_apply",
    )(ids, ids, x, gffn, gw, gfin, table)


def _stack_heads(q_ref, g):
    return jnp.concatenate(
        [q_ref[0, :, (g * GROUP + r) * HEAD_DIM:(g * GROUP + r + 1) * HEAD_DIM] for r in range(GROUP)], axis=0)


NEW_PAD = 128


def _pad_new(x):
    return jnp.concatenate([x, jnp.zeros((NEW_PAD - x.shape[0], x.shape[1]), F32)], axis=0).astype(BF16)


def _cmp_sample_kernel(q_ref, y_ref, o_ref, sel_ref, kcp_ref, tmp_ref, *, past):
    _combine_compressed(y_ref, kcp_ref, tmp_ref)
    t = q_ref.shape[1]
    nb = kcp_ref.shape[1]
    tpos = past + lax.broadcasted_iota(jnp.int32, (t, 1), 0)
    pos = jnp.concatenate([tpos] * GROUP, axis=0)
    blk = lax.broadcasted_iota(jnp.int32, (1, nb), 1)
    for g in range(KV_HEADS):
        q = _stack_heads(q_ref, g)
        outs, imp = _cmp_attention([q], pos, kcp_ref, g, t)
        o = outs[0]
        for r in range(GROUP):
            hcol = (g * GROUP + r) * HEAD_DIM
            o_ref[0, :, hcol:hcol + HEAD_DIM] = o[r * t:(r + 1) * t]
        imp_t = functools.reduce(jnp.add, [imp[r * t:(r + 1) * t] for r in range(GROUP)])
        v = jnp.where(blk == 0, FORCE_SCORE, imp_t)
        sel = _topk_select(v, SEL_TOPN - 1, 1)
        sel_ref[0, :, g * nb:(g + 1) * nb] = sel.astype(BF16)


def _cmp_sample(qq3, y, *, past):
    b, t, _ = qq3.shape
    ns = y.shape[1]
    nb = ns // CMP_PER_SEL
    return pl.pallas_call(
        functools.partial(_cmp_sample_kernel, past=past),
        grid=(b,),
        in_specs=[pl.BlockSpec((1, t, 1024), lambda bi: (bi, 0, 0)),
                  pl.BlockSpec((1, ns, 1024), lambda bi: (bi, 0, 0))],
        out_specs=[pl.BlockSpec((1, t, 1024), lambda bi: (bi, 0, 0)),
                   pl.BlockSpec((1, t, KV_HEADS * nb), lambda bi: (bi, 0, 0))],
        out_shape=[jax.ShapeDtypeStruct((b, t, 1024), F32),
                   jax.ShapeDtypeStruct((b, t, KV_HEADS * nb), BF16)],
        scratch_shapes=[pltpu.VMEM((2 * KV_HEADS * CMP_PER_SEL, nb, HEAD_DIM), BF16),
                        pltpu.VMEM((ns, HEAD_DIM), F32)],
        compiler_params=_params("parallel"),
        name="cmp_sample",
    )(qq3, y)


def _sel_sample_kernel(q_ref, sel_ref, kv_ref, new_ref, o_ref, m_ref, l_ref, acc_ref):
    c = pl.program_id(1)
    nc = pl.num_programs(1)
    t = q_ref.shape[1]
    nb = sel_ref.shape[2] // KV_HEADS
    width = kv_ref.shape[1]

    @pl.when(c == 0)
    def _():
        m_ref[...] = jnp.full(m_ref.shape, NEG_INF, F32)
        l_ref[...] = jnp.zeros(l_ref.shape, F32)
        acc_ref[...] = jnp.zeros(acc_ref.shape, F32)

    def update(g, s, ok, v):
        s = jnp.where(ok, s, NEG_INF)
        m = m_ref[g]
        m_new = jnp.maximum(m, jnp.max(s, axis=-1, keepdims=True))
        m_safe = jnp.where(m_new == NEG_INF, 0.0, m_new)
        p = jnp.exp(s - m_safe)
        alpha = jnp.exp(m - m_safe)
        l_ref[g] = alpha * l_ref[g] + jnp.sum(p, axis=-1, keepdims=True)
        acc_ref[g] = alpha * acc_ref[g] + _dot(p.astype(BF16), v)
        m_ref[g] = m_new

    for g in range(KV_HEADS):
        q = _stack_heads(q_ref, g)
        k = kv_ref[0, :, g * HEAD_DIM:(g + 1) * HEAD_DIM]
        v = kv_ref[0, :, (KV_HEADS + g) * HEAD_DIM:(KV_HEADS + g + 1) * HEAD_DIM]
        s = _dot_nt(q, k) * SCALE
        picked = _expand_blocks(sel_ref[0, :, g * nb:(g + 1) * nb], c * (width // SEL_BLOCK), width)
        picked = jnp.concatenate([picked] * GROUP, axis=0)
        update(g, s, picked > 0.5, v)

    @pl.when(c == nc - 1)
    def _():
        qi = lax.broadcasted_iota(jnp.int32, (t, 1), 0)
        qi = jnp.concatenate([qi] * GROUP, axis=0)
        ok = lax.broadcasted_iota(jnp.int32, (1, NEW_PAD), 1) <= qi
        for g in range(KV_HEADS):
            q = _stack_heads(q_ref, g)
            k = _pad_new(new_ref[pl.ds(2 * KV_HEADS + g, t, stride=ROWS_PER_POS), :])
            v = _pad_new(new_ref[pl.ds(3 * KV_HEADS + g, t, stride=ROWS_PER_POS), :])
            update(g, _dot_nt(q, k) * SCALE, ok, v)
            o = acc_ref[g] * (1.0 / jnp.maximum(l_ref[g], 1e-30))
            for r in range(GROUP):
                hcol = (g * GROUP + r) * HEAD_DIM
                o_ref[0, :, hcol:hcol + HEAD_DIM] = o[r * t:(r + 1) * t]


SEL_SAMPLE_CHUNK = 2048


def _sel_sample(kvs, qq3, sel, nsa_rows):
    nbatch, past, _ = kvs.shape
    t = qq3.shape[1]
    nb2 = sel.shape[2]
    rows = GROUP * t
    width = min(SEL_SAMPLE_CHUNK, past)
    return pl.pallas_call(
        _sel_sample_kernel,
        grid=(nbatch, past // width),
        in_specs=[pl.BlockSpec((1, t, 1024), lambda bi, ci: (bi, 0, 1)),
                  pl.BlockSpec((1, t, nb2), lambda bi, ci: (bi, 0, 0)),
                  pl.BlockSpec((1, width, 512), lambda bi, ci: (bi, ci, 0)),
                  pl.BlockSpec((t * ROWS_PER_POS, HEAD_DIM), lambda bi, ci: (bi, 0))],
        out_specs=pl.BlockSpec((1, t, 1024), lambda bi, ci: (bi, 0, 0)),
        out_shape=jax.ShapeDtypeStruct((nbatch, t, 1024), F32),
        scratch_shapes=[pltpu.VMEM((KV_HEADS, rows, 1), F32), pltpu.VMEM((KV_HEADS, rows, 1), F32),
                        pltpu.VMEM((KV_HEADS, rows, HEAD_DIM), F32)],
        compiler_params=_params("parallel", "arbitrary"),
        name="sel_sample",
    )(qq3, sel, kvs, nsa_rows)


def _win_sample_kernel(q_ref, st_ref, new_ref, o_ref):
    t = q_ref.shape[1]
    nw = st_ref.shape[0] // WIN_ROWS_PER_POS
    old = lambda comp: st_ref[pl.ds(comp, nw, stride=WIN_ROWS_PER_POS), :].astype(BF16)
    new = lambda comp: _pad_new(new_ref[pl.ds(comp, t, stride=WIN_ROWS_PER_POS), :])
    qi = lax.broadcasted_iota(jnp.int32, (t, 1), 0)
    qi = jnp.concatenate([qi] * GROUP, axis=0)
    d_old = (nw + qi) - lax.broadcasted_iota(jnp.int32, (1, nw), 1)
    ok_old = (d_old >= 0) & (d_old < WINDOW)
    d_new = qi - lax.broadcasted_iota(jnp.int32, (1, NEW_PAD), 1)
    ok_new = (d_new >= 0) & (d_new < WINDOW)
    for g in range(KV_HEADS):
        q = _stack_heads(q_ref, g)
        k_old, v_old = old(g), old(KV_HEADS + g)
        k_new, v_new = new(g), new(KV_HEADS + g)
        p_old, p_new = _masked_softmax_parts(
            [_dot_nt(q, k_old) * SCALE, _dot_nt(q, k_new) * SCALE], [ok_old, ok_new])
        o = _dot(p_old.astype(BF16), v_old) + _dot(p_new.astype(BF16), v_new)
        for r in range(GROUP):
            hcol = (g * GROUP + r) * HEAD_DIM
            o_ref[0, :, hcol:hcol + HEAD_DIM] = o[r * t:(r + 1) * t]


def _win_sample(qq3, state_rows, win_rows):
    b, t, _ = qq3.shape
    nw = state_rows.shape[0] // (b * WIN_ROWS_PER_POS)
    return pl.pallas_call(
        _win_sample_kernel,
        grid=(b,),
        in_specs=[pl.BlockSpec((1, t, 1024), lambda bi: (bi, 0, 1)),
                  pl.BlockSpec((nw * WIN_ROWS_PER_POS, HEAD_DIM), lambda bi: (bi, 0)),
                  pl.BlockSpec((t * WIN_ROWS_PER_POS, HEAD_DIM), lambda bi: (bi, 0))],
        out_specs=pl.BlockSpec((1, t, 1024), lambda bi: (bi, 0, 0)),
        out_shape=jax.ShapeDtypeStruct((b, t, 1024), F32),
        compiler_params=_params("parallel"),
        name="win_sample",
    )(qq3, state_rows, win_rows)


def _rope_tables(pos):
    half = HEAD_DIM // 2
    inv = ROPE_THETA ** (-jnp.arange(half, dtype=F32) / half)
    ang = pos.astype(F32)[:, None] * inv[None, :]
    cos, sin = jnp.cos(ang), jnp.sin(ang)
    return jnp.concatenate([cos, cos], axis=-1), jnp.concatenate([-sin, sin], axis=-1)


def _prep_weights(w_in, w_phi_k, w_phi_v, w_pool_group, pool_scale, w_branch_attn, w_branch_pool, w_out,
                  peer_w_query, peer_sub_keys, peer_u, peer_v):
    d = w_in.shape[0]
    qw = N_HEADS * HEAD_DIM
    kvw = 6 * KV_HEADS * HEAD_DIM
    ngw = 3 * N_HEADS
    pw = d // 2
    o1, o2, o3, o4 = qw, qw + kvw, qw + kvw + ngw, qw + kvw + ngw + pw
    wb = w_in.astype(BF16)
    w_qkv = jnp.concatenate([wb[:, :o2], wb[:, o2:o3], jnp.zeros((d, HEAD_DIM - ngw), BF16)], axis=1)
    cat = lambda w: jnp.concatenate([w[:CMP_STRIDE], w[CMP_STRIDE:]], axis=-1).astype(BF16)
    u16 = lax.bitcast_convert_type(peer_u.astype(BF16), jnp.uint16).astype(jnp.uint32)
    v16 = lax.bitcast_convert_type(peer_v.astype(BF16), jnp.uint16).astype(jnp.uint32)
    return dict(
        w_qkv=w_qkv, w_u=wb[:, o3:o4], w_gab=wb[:, o4:],
        wk=cat(w_phi_k), wv=cat(w_phi_v),
        w_pool=w_pool_group.astype(BF16), pool_scale=pool_scale.reshape(1, -1),
        w_ba=w_branch_attn.astype(BF16), w_bp=w_branch_pool.astype(BF16), w_out=w_out.astype(BF16),
        w_query=peer_w_query.astype(BF16),
        keys=peer_sub_keys.reshape(PEER_HEADS * 2, PEER_NKEYS, -1).astype(BF16),
        table=(u16 | (v16 << 16))[:, None, :],
    )


def _token_tail(x2d, o_cmp, o_sel, o_win, gates, pool_out, gab, wp, g_ffn, g_final):
    mix = _mix(o_cmp, o_sel, o_win, gates, pool_out, gab, wp["w_ba"], wp["w_bp"])
    x2 = _out_proj(x2d, mix, wp["w_out"])
    ids, gw = _peer_score(x2, g_ffn, wp["w_query"], wp["keys"])
    return _peer_apply(ids, x2, g_ffn, gw, g_final, wp["table"])


def kernel(x_prompt, x_sample, cache_kv_nsa, state_win_kv, state_pool, page_table, g_norm_mix, w_in, w_phi_k,
           w_phi_v, w_pool_group, pool_scale, w_branch_attn, w_branch_pool, w_out, g_norm_ffn, peer_w_query,
           peer_sub_keys, peer_u, peer_v, g_norm_final):
    assert g_norm_mix.shape[0] == 1, "single-layer step"
    bp, t, d = x_prompt.shape
    bs, ts, _ = x_sample.shape
    n_pages = page_table.shape[1]
    page_size = cache_kv_nsa.shape[2]
    past = n_pages * page_size
    wp = _prep_weights(w_in[0], w_phi_k[0], w_phi_v[0], w_pool_group[0], pool_scale[0], w_branch_attn[0],
                       w_branch_pool[0], w_out[0], peer_w_query[0], peer_sub_keys[0], peer_u[0], peer_v[0])
    g_mix = g_norm_mix[0].reshape(1, d)
    g_ffn = g_norm_ffn[0].reshape(1, d)
    g_fin = g_norm_final.reshape(1, d)

    xp = x_prompt.reshape(bp * t, d)
    cos, sin = _rope_tables(jnp.arange(t))
    cos, sin = jnp.tile(cos, (bp, 1)), jnp.tile(sin, (bp, 1))
    qq, nsa, win, gates, kvb = _qkv_proj(xp, g_mix, wp["w_qkv"], cos, sin)
    u = _norm_proj(xp, g_mix, wp["w_u"])
    gab = _norm_proj(xp, g_mix, wp["w_gab"], act="sigmoid", out_dtype=BF16)
    qq3, kvb3 = qq.reshape(bp, t, -1), kvb.reshape(bp, t, -1)
    y = _compress_prompt(nsa, bp, wp["wk"], wp["wv"])
    o_cmp, sel = _cmp_prompt(qq3, y)
    o_sel = _sel_prompt(qq3, sel, kvb3)
    o_win = _win_prompt(qq3, kvb3)
    u3 = u.reshape(bp, t, -1)
    pool_out = _pool(u3, u3, wp["w_pool"], wp["pool_scale"], base=0, zero_first_prev=True)
    y_prompt = _token_tail(xp, o_cmp.reshape(bp * t, -1), o_sel.reshape(bp * t, -1), o_win.reshape(bp * t, -1),
                           gates, pool_out.reshape(bp * t, -1), gab, wp, g_ffn, g_fin)
    wlen = min(WINDOW, t)
    new_kv_p = nsa.reshape(1, bp, t, 4, KV_HEADS, HEAD_DIM)
    new_win_p = win.reshape(bp, t, 2, KV_HEADS, HEAD_DIM)[None, :, t - wlen:]
    new_pool_p = u3[None, :, t - (POOL_PREV - 1):]

    xs = x_sample.reshape(bs * ts, d)
    cos_s, sin_s = _rope_tables(past + jnp.arange(ts))
    cos_s, sin_s = jnp.tile(cos_s, (bs, 1)), jnp.tile(sin_s, (bs, 1))
    qq_s, nsa_s, win_s, gates_s, _ = _qkv_proj(xs, g_mix, wp["w_qkv"], cos_s, sin_s)
    u_s = _norm_proj(xs, g_mix, wp["w_u"])
    gab_s = _norm_proj(xs, g_mix, wp["w_gab"], act="sigmoid", out_dtype=BF16)
    qq_s3 = qq_s.reshape(bs, ts, -1)
    cache_rows = cache_kv_nsa.reshape(-1, HEAD_DIM)
    y_s, kvs = _compress_pages(cache_rows, page_table, page_size, wp["wk"], wp["wv"])
    o_cmp_s, sel_s = _cmp_sample(qq_s3, y_s, past=past)
    o_sel_s = _sel_sample(kvs, qq_s3, sel_s, nsa_s)
    o_win_s = _win_sample(qq_s3, state_win_kv.reshape(-1, HEAD_DIM), win_s)
    u_s3 = u_s.reshape(bs, ts, -1)
    st_pool = state_pool[0]
    prev = jnp.pad(st_pool, ((0, 0), (POOL_PREV - st_pool.shape[1], 0), (0, 0)))
    pool_out_s = _pool(prev, u_s3, wp["w_pool"], wp["pool_scale"], base=st_pool.shape[1], zero_first_prev=False)
    y_sample = _token_tail(xs, o_cmp_s.reshape(bs * ts, -1), o_sel_s.reshape(bs * ts, -1),
                           o_win_s.reshape(bs * ts, -1), gates_s, pool_out_s.reshape(bs * ts, -1), gab_s, wp,
                           g_ffn, g_fin)
    new_kv_s = nsa_s.reshape(1, bs, ts, 4, KV_HEADS, HEAD_DIM)
    win_ext = jnp.concatenate([state_win_kv, win_s.reshape(1, bs, ts, 2, KV_HEADS, HEAD_DIM)], axis=2)
    new_win_s = win_ext[:, :, win_ext.shape[2] - min(WINDOW, win_ext.shape[2]):]
    pool_ext = jnp.concatenate([st_pool, u_s3], axis=1)
    new_pool_s = pool_ext[None, :, pool_ext.shape[1] - (POOL_PREV - 1):]

    return (y_prompt.reshape(bp, t, d), y_sample.reshape(bs, ts, d), new_kv_p, new_kv_s, new_win_p, new_win_s,
            new_pool_p, new_pool_s)
```

```python
import functools

import jax
import jax.numpy as jnp
from jax import lax
from jax.experimental import pallas as pl
from jax.experimental.pallas import tpu as pltpu

F32 = jnp.float32
BF16 = jnp.bfloat16

HEAD_DIM = 128
N_HEADS = 8
KV_HEADS = 2
GROUP = N_HEADS // KV_HEADS
CMP_LEN = 32
CMP_STRIDE = 16
SEL_BLOCK = 64
SEL_TOPN = 16
CMP_PER_SEL = SEL_BLOCK // CMP_STRIDE
WINDOW = 512
Q_BLOCK = 128
ROPE_THETA = 10000.0
FORCE_SCORE = 1e4
POOL_WINDOWS = (2, 4, 8, 16)
POOL_PREV = 16
PEER_HEADS = 8
PEER_NKEYS = 128
PEER_TOPK = 16
EPS = 1e-6
SCALE = HEAD_DIM ** -0.5
LOG2E = 1.4426950408889634
NEG_INF = float("-inf")

ROWS_PER_POS = 4 * KV_HEADS
WIN_ROWS_PER_POS = 2 * KV_HEADS

ROW_TILE = 512
VMEM_LIMIT = 56 * 1024 * 1024


def _params(*sem):
    return pltpu.CompilerParams(dimension_semantics=sem, vmem_limit_bytes=VMEM_LIMIT)


def _dot(a, b):
    return jnp.dot(a, b, preferred_element_type=F32)


def _dot_nt(a, b):
    return lax.dot_general(a, b, (((1,), (1,)), ((), ())), preferred_element_type=F32)


def _rms(x, g):
    return x * lax.rsqrt(jnp.mean(x * x, axis=-1, keepdims=True) + EPS) * g


def _masked_softmax_parts(parts, masks):
    parts = [jnp.where(mk, s, NEG_INF) for s, mk in zip(parts, masks)]
    same = all(s.shape == parts[0].shape for s in parts)

    def across(op, red, xs):
        if same:
            return red(functools.reduce(op, xs), axis=-1, keepdims=True)
        return functools.reduce(op, [red(x, axis=-1, keepdims=True) for x in xs])

    m = across(jnp.maximum, jnp.max, parts)
    m = jnp.where(m == NEG_INF, 0.0, m)
    es = [jnp.exp(s - m) for s in parts]
    den = across(jnp.add, jnp.sum, es)
    inv = 1.0 / jnp.maximum(den, 1e-30)
    return [e * inv for e in es]


def _topk_select(v, n, axis):
    size = v.shape[axis]
    idx = lax.broadcasted_iota(jnp.int32, v.shape, axis).astype(F32)
    sel = jnp.zeros(v.shape, F32)
    for _ in range(n):
        m = jnp.max(v, axis=axis, keepdims=True)
        first = jnp.min(jnp.where(v == m, idx, float(size)), axis=axis, keepdims=True)
        hit = idx == first
        v = jnp.where(hit, NEG_INF, v)
        sel = jnp.where(hit, 1.0, sel)
    return sel


def _topk_sorted(v, n, payload=None):
    size = v.shape[0]
    idx = lax.broadcasted_iota(jnp.int32, v.shape, 0).astype(F32)
    vals, picks = [], []
    for _ in range(n):
        m = jnp.max(v, axis=0, keepdims=True)
        first = jnp.min(jnp.where(v == m, idx, float(size)), axis=0, keepdims=True)
        hit = idx == first
        vals.append(m)
        if payload is None:
            picks.append(first)
        else:
            picks.append(jnp.max(jnp.where(hit, payload, -1.0), axis=0, keepdims=True))
        v = jnp.where(hit, NEG_INF, v)
    return jnp.concatenate(vals, axis=0), jnp.concatenate(picks, axis=0)


def _proj_kernel(x_ref, g_ref, w_ref, o_ref, *, act, tn):
    h = _rms(x_ref[...], g_ref[...]).astype(BF16)
    for c in range(w_ref.shape[1] // tn):
        z = _dot(h, w_ref[:, c * tn:(c + 1) * tn])
        if act == "sigmoid":
            z = jax.nn.sigmoid(z)
        o_ref[:, c * tn:(c + 1) * tn] = z.astype(o_ref.dtype)


def _norm_proj(x, g, w, *, act=None, out_dtype=F32, tm=ROW_TILE, tn=512, wn=2048):
    m, d = x.shape
    n = w.shape[1]
    tm = min(tm, m)
    wn = min(wn, n)
    return pl.pallas_call(
        functools.partial(_proj_kernel, act=act, tn=tn),
        grid=(n // wn, m // tm),
        in_specs=[pl.BlockSpec((tm, d), lambda j, i: (i, 0)),
                  pl.BlockSpec((1, d), lambda j, i: (0, 0)),
                  pl.BlockSpec((d, wn), lambda j, i: (0, j))],
        out_specs=pl.BlockSpec((tm, wn), lambda j, i: (i, j)),
        out_shape=jax.ShapeDtypeStruct((m, n), out_dtype),
        compiler_params=_params("parallel", "parallel"),
        name="norm_proj",
    )(x, g, w)


QKV_COLS = N_HEADS * HEAD_DIM + 6 * KV_HEADS * HEAD_DIM + HEAD_DIM


def _qkv_kernel(x_ref, g_ref, w_ref, cos_ref, sin_ref, qq_ref, nsa_ref, win_ref, gate_ref, kvb_ref):
    h = _rms(x_ref[...], g_ref[...]).astype(BF16)
    cos = cos_ref[...]
    sin = sin_ref[...]

    def rope(z):
        return z * cos + pltpu.roll(z, HEAD_DIM // 2, 1) * sin

    qw = N_HEADS * HEAD_DIM
    for c in range(qw // 512):
        z = _dot(h, w_ref[:, c * 512:(c + 1) * 512])
        for j in range(4):
            zh = z[:, j * 128:(j + 1) * 128]
            col = c * 512 + j * 128
            qq_ref[:, col:col + 128] = zh.astype(BF16)
            qq_ref[:, qw + col:qw + col + 128] = rope(zh).astype(BF16)
    tm = x_ref.shape[0]

    def put(ref, comp, n_comp, val):
        ref[pl.ds(comp, tm, stride=n_comp), :] = val

    for c in range(3):
        z = _dot(h, w_ref[:, qw + c * 512:qw + (c + 1) * 512])
        for j in range(2):
            zk = z[:, j * 128:(j + 1) * 128]
            zv = z[:, 256 + j * 128:256 + (j + 1) * 128]
            if c == 0:
                put(nsa_ref, j, ROWS_PER_POS, zk)
                put(nsa_ref, KV_HEADS + j, ROWS_PER_POS, zv)
                continue
            zr = rope(zk)
            if c == 1:
                put(nsa_ref, 2 * KV_HEADS + j, ROWS_PER_POS, zr)
                put(nsa_ref, 3 * KV_HEADS + j, ROWS_PER_POS, zv)
            else:
                put(win_ref, j, WIN_ROWS_PER_POS, zr)
                put(win_ref, KV_HEADS + j, WIN_ROWS_PER_POS, zv)
            base = (c - 1) * 512
            kvb_ref[:, base + j * 128:base + (j + 1) * 128] = zr.astype(BF16)
            kvb_ref[:, base + 256 + j * 128:base + 256 + (j + 1) * 128] = zv.astype(BF16)
    z = _dot(h, w_ref[:, qw + 1536:qw + 1536 + 128])
    gate_ref[...] = jax.nn.sigmoid(z)


def _qkv_proj(x, g, w, cos, sin, *, tm=ROW_TILE):
    m, d = x.shape
    tm = min(tm, m)
    row = lambda i: (i, 0)
    const = lambda i: (0, 0)
    return pl.pallas_call(
        _qkv_kernel,
        grid=(m // tm,),
        in_specs=[pl.BlockSpec((tm, d), row), pl.BlockSpec((1, d), const),
                  pl.BlockSpec((d, QKV_COLS), const),
                  pl.BlockSpec((tm, HEAD_DIM), row), pl.BlockSpec((tm, HEAD_DIM), row)],
        out_specs=[pl.BlockSpec((tm, 2048), row), pl.BlockSpec((tm * ROWS_PER_POS, HEAD_DIM), row),
                   pl.BlockSpec((tm * WIN_ROWS_PER_POS, HEAD_DIM), row), pl.BlockSpec((tm, 128), row),
                   pl.BlockSpec((tm, 1024), row)],
        out_shape=[jax.ShapeDtypeStruct((m, 2048), BF16), jax.ShapeDtypeStruct((m * ROWS_PER_POS, HEAD_DIM), F32),
                   jax.ShapeDtypeStruct((m * WIN_ROWS_PER_POS, HEAD_DIM), F32), jax.ShapeDtypeStruct((m, 128), F32),
                   jax.ShapeDtypeStruct((m, 1024), BF16)],
        compiler_params=_params("parallel"),
        name="qkv_proj",
    )(x, g, w, cos, sin)


SUB_ROWS = CMP_STRIDE * ROWS_PER_POS


def _compress_rows(load, n_rows, wk_ref, wv_ref, y_ref):
    for kv in range(2):
        w_ref = wk_ref if kv == 0 else wv_ref
        for g in range(KV_HEADS):
            comp = kv * KV_HEADS + g
            acc = jnp.zeros((n_rows, 2 * HEAD_DIM), F32)
            for i in range(CMP_STRIDE):
                acc = acc + _dot(load(i, comp).astype(BF16), w_ref[i])
            y_ref[0, :, 2 * comp * HEAD_DIM:2 * (comp + 1) * HEAD_DIM] = acc


def _compress_prompt_kernel(x_ref, wk_ref, wv_ref, y_ref):
    ts = x_ref.shape[0] // SUB_ROWS
    load = lambda i, comp: x_ref[pl.ds(i * ROWS_PER_POS + comp, ts, stride=SUB_ROWS), :]
    _compress_rows(load, ts, wk_ref, wv_ref, y_ref)


def _compress_prompt(nsa_rows, b, wk, wv, *, ts=128):
    n_sub = nsa_rows.shape[0] // (b * SUB_ROWS)
    ts = min(ts, n_sub)
    steps = n_sub // ts
    wspec = pl.BlockSpec((CMP_STRIDE, HEAD_DIM, 2 * HEAD_DIM), lambda bi, ci: (0, 0, 0))
    return pl.pallas_call(
        _compress_prompt_kernel,
        grid=(b, steps),
        in_specs=[pl.BlockSpec((ts * SUB_ROWS, HEAD_DIM), lambda bi, ci: (bi * steps + ci, 0)), wspec, wspec],
        out_specs=pl.BlockSpec((1, ts, 1024), lambda bi, ci: (bi, ci, 0)),
        out_shape=jax.ShapeDtypeStruct((b, n_sub, 1024), F32),
        compiler_params=_params("parallel", "parallel"),
        name="compress_prompt",
    )(nsa_rows, wk, wv)


PAGES_PER_STEP = 16


def _compress_pages_kernel(pt_ref, *refs):
    pages = refs[:PAGES_PER_STEP]
    wk_ref, wv_ref, y_ref, kvs_ref = refs[PAGES_PER_STEP:]
    per_page = pages[0].shape[0] // SUB_ROWS
    page_size = pages[0].shape[0] // ROWS_PER_POS

    def load(i, comp):
        rows = pl.ds(i * ROWS_PER_POS + comp, per_page, stride=SUB_ROWS)
        return jnp.concatenate([p[rows, :] for p in pages], axis=0)

    _compress_rows(load, PAGES_PER_STEP * per_page, wk_ref, wv_ref, y_ref)
    for j in range(2 * KV_HEADS):
        rows = pl.ds(2 * KV_HEADS + j, page_size, stride=ROWS_PER_POS)
        kvs_ref[0, :, j * HEAD_DIM:(j + 1) * HEAD_DIM] = jnp.concatenate(
            [p[rows, :] for p in pages], axis=0).astype(BF16)


def _page_specs(page_rows):
    def spec(k):
        return pl.BlockSpec((page_rows, HEAD_DIM), lambda bi, ci, pt: (pt[bi, ci * PAGES_PER_STEP + k], 0))
    return [spec(k) for k in range(PAGES_PER_STEP)]


def _compress_pages(cache_rows, page_table, page_size, wk, wv):
    nb, n_pages = page_table.shape
    per_page = page_size // CMP_STRIDE
    rows = PAGES_PER_STEP * per_page
    const3 = lambda bi, ci, pt: (0, 0, 0)
    grid_spec = pltpu.PrefetchScalarGridSpec(
        num_scalar_prefetch=1,
        grid=(nb, n_pages // PAGES_PER_STEP),
        in_specs=_page_specs(page_size * ROWS_PER_POS) + [
            pl.BlockSpec((CMP_STRIDE, HEAD_DIM, 2 * HEAD_DIM), const3),
            pl.BlockSpec((CMP_STRIDE, HEAD_DIM, 2 * HEAD_DIM), const3)],
        out_specs=[pl.BlockSpec((1, rows, 1024), lambda bi, ci, pt: (bi, ci, 0)),
                   pl.BlockSpec((1, PAGES_PER_STEP * page_size, 512), lambda bi, ci, pt: (bi, ci, 0))],
    )
    return pl.pallas_call(
        _compress_pages_kernel,
        grid_spec=grid_spec,
        out_shape=[jax.ShapeDtypeStruct((nb, n_pages * per_page, 1024), F32),
                   jax.ShapeDtypeStruct((nb, n_pages * page_size, 512), BF16)],
        compiler_params=_params("parallel", "parallel"),
        name="compress_pages",
    )(page_table, *([cache_rows] * PAGES_PER_STEP), wk, wv)


def _combine_compressed(y_ref, kcp_ref, tmp_ref):
    ns = y_ref.shape[1]
    nb = ns // CMP_PER_SEL
    last = lax.broadcasted_iota(jnp.int32, (ns, 1), 0) == ns - 1
    for a in range(2 * KV_HEADS):
        y1 = y_ref[0, :, a * 256:a * 256 + 128]
        y2 = y_ref[0, :, a * 256 + 128:(a + 1) * 256]
        nxt = jnp.where(last, 0.0, pltpu.roll(y2, ns - 1, 0))
        tmp_ref[...] = y1 + nxt
        for c in range(CMP_PER_SEL):
            kcp_ref[a * CMP_PER_SEL + c] = tmp_ref[pl.ds(c, nb, stride=CMP_PER_SEL), :].astype(BF16)


def _cmp_attention(q_rows, pos, kcp_ref, g):
    nb = kcp_ref.shape[1]
    blk = lax.broadcasted_iota(jnp.int32, (1, nb), 1)
    masks = [(SEL_BLOCK * blk + CMP_STRIDE * c + CMP_LEN - 1) <= pos for c in range(CMP_PER_SEL)]
    outs = []
    imp = None
    for qh in q_rows:
        s = [_dot_nt(qh, kcp_ref[g * CMP_PER_SEL + c]) * SCALE for c in range(CMP_PER_SEL)]
        p = _masked_softmax_parts(s, masks)
        o = functools.reduce(jnp.add, [
            _dot(p[c].astype(BF16), kcp_ref[(KV_HEADS + g) * CMP_PER_SEL + c]) for c in range(CMP_PER_SEL)])
        outs.append(o)
        ps = functools.reduce(jnp.add, p)
        imp = ps if imp is None else imp + ps
    return outs, imp


def _cmp_prompt_kernel(q_ref, y_ref, o_ref, sel_ref, kcp_ref, tmp_ref):
    i = pl.program_id(1)

    @pl.when(i == 0)
    def _():
        _combine_compressed(y_ref, kcp_ref, tmp_ref)

    nb = kcp_ref.shape[1]
    pos = i * Q_BLOCK + lax.broadcasted_iota(jnp.int32, (Q_BLOCK, 1), 0)
    blk = lax.broadcasted_iota(jnp.int32, (1, nb), 1)
    cur = pos // SEL_BLOCK
    forced = (blk == cur) | (blk == 0)
    causal = blk <= cur
    for g in range(KV_HEADS):
        q_rows = [q_ref[0, :, (g * GROUP + r) * HEAD_DIM:(g * GROUP + r + 1) * HEAD_DIM] for r in range(GROUP)]
        outs, imp = _cmp_attention(q_rows, pos, kcp_ref, g)
        for r in range(GROUP):
            hcol = (g * GROUP + r) * HEAD_DIM
            o_ref[0, :, hcol:hcol + HEAD_DIM] = outs[r]
        v = jnp.where(forced, FORCE_SCORE, jnp.where(causal, imp, -1.0))
        sel_t = _topk_select(v.T, SEL_TOPN, 0)
        sel = jnp.where(causal, sel_t.T, 0.0)
        sel_ref[0, :, g * nb:(g + 1) * nb] = sel.astype(BF16)


def _cmp_prompt(qq3, y):
    b, t, _ = qq3.shape
    ns = y.shape[1]
    nb = ns // CMP_PER_SEL
    return pl.pallas_call(
        _cmp_prompt_kernel,
        grid=(b, t // Q_BLOCK),
        in_specs=[pl.BlockSpec((1, Q_BLOCK, 1024), lambda bi, i: (bi, i, 0)),
                  pl.BlockSpec((1, ns, 1024), lambda bi, i: (bi, 0, 0))],
        out_specs=[pl.BlockSpec((1, Q_BLOCK, 1024), lambda bi, i: (bi, i, 0)),
                   pl.BlockSpec((1, Q_BLOCK, KV_HEADS * nb), lambda bi, i: (bi, i, 0))],
        out_shape=[jax.ShapeDtypeStruct((b, t, 1024), F32),
                   jax.ShapeDtypeStruct((b, t, KV_HEADS * nb), BF16)],
        scratch_shapes=[pltpu.VMEM((2 * KV_HEADS * CMP_PER_SEL, nb, HEAD_DIM), BF16),
                        pltpu.VMEM((ns, HEAD_DIM), F32)],
        compiler_params=_params("parallel", "arbitrary"),
        name="cmp_prompt",
    )(qq3, y)


SEL_CHUNK = 512


def _expand_blocks(sel, first_block, width):
    nb = sel.shape[1]
    b = lax.broadcasted_iota(jnp.int32, (nb, width), 0)
    t = lax.broadcasted_iota(jnp.int32, (nb, width), 1)
    e = jnp.where(b == first_block + t // SEL_BLOCK, 1.0, 0.0).astype(BF16)
    return _dot(sel, e)


def _sel_prompt_kernel(q_ref, sel_ref, kv_ref, o_ref):
    i = pl.program_id(1)
    nb = sel_ref.shape[2] // KV_HEADS
    pos = i * Q_BLOCK + lax.broadcasted_iota(jnp.int32, (Q_BLOCK, 1), 0)
    n_chunks = (i * Q_BLOCK + Q_BLOCK + SEL_CHUNK - 1) // SEL_CHUNK
    lane = lax.broadcasted_iota(jnp.int32, (1, SEL_CHUNK), 1)
    rows = GROUP * Q_BLOCK
    for g in range(KV_HEADS):
        q = jnp.concatenate(
            [q_ref[0, :, (g * GROUP + r) * HEAD_DIM:(g * GROUP + r + 1) * HEAD_DIM] for r in range(GROUP)], axis=0)
        sel = sel_ref[0, :, g * nb:(g + 1) * nb]

        def body(c, carry):
            m, l, acc = carry
            start = pl.multiple_of(c * SEL_CHUNK, SEL_CHUNK)
            k = kv_ref[0, pl.ds(start, SEL_CHUNK), g * HEAD_DIM:(g + 1) * HEAD_DIM]
            v = kv_ref[0, pl.ds(start, SEL_CHUNK), (KV_HEADS + g) * HEAD_DIM:(KV_HEADS + g + 1) * HEAD_DIM]
            picked = _expand_blocks(sel, c * (SEL_CHUNK // SEL_BLOCK), SEL_CHUNK)
            bias = jnp.where(picked > 0.5, jnp.where((start + lane) <= pos, 0.0, NEG_INF), NEG_INF)
            s = _dot_nt(q, k) * (SCALE * LOG2E) + jnp.concatenate([bias] * GROUP, axis=0)
            m_new = jnp.maximum(m, jnp.max(s, axis=-1, keepdims=True))
            m_safe = jnp.where(m_new == NEG_INF, 0.0, m_new)
            p = jnp.exp2(s - m_safe)
            alpha = jnp.exp2(m - m_safe)
            l = alpha * l + jnp.sum(p, axis=-1, keepdims=True)
            acc = alpha * acc + _dot(p.astype(BF16), v)
            return m_new, l, acc

        init = (jnp.full((rows, 1), NEG_INF, F32), jnp.zeros((rows, 1), F32), jnp.zeros((rows, HEAD_DIM), F32))
        m, l, acc = lax.fori_loop(0, n_chunks, body, init)
        o = acc * (1.0 / jnp.maximum(l, 1e-30))
        for r in range(GROUP):
            hcol = (g * GROUP + r) * HEAD_DIM
            o_ref[0, :, hcol:hcol + HEAD_DIM] = o[r * Q_BLOCK:(r + 1) * Q_BLOCK]


def _sel_prompt(qq3, sel, kvb3):
    b, t, _ = qq3.shape
    nb2 = sel.shape[2]
    return pl.pallas_call(
        _sel_prompt_kernel,
        grid=(b, t // Q_BLOCK),
        in_specs=[pl.BlockSpec((1, Q_BLOCK, 1024), lambda bi, i: (bi, i, 1)),
                  pl.BlockSpec((1, Q_BLOCK, nb2), lambda bi, i: (bi, i, 0)),
                  pl.BlockSpec((1, t, 512), lambda bi, i: (bi, 0, 0))],
        out_specs=pl.BlockSpec((1, Q_BLOCK, 1024), lambda bi, i: (bi, i, 0)),
        out_shape=jax.ShapeDtypeStruct((b, t, 1024), F32),
        compiler_params=_params("parallel", "arbitrary"),
        name="sel_prompt",
    )(qq3, sel, kvb3)


def _win_prompt_kernel(q_ref, kv_ref, o_ref, *, span):
    i = pl.program_id(1)
    pos = i * Q_BLOCK + lax.broadcasted_iota(jnp.int32, (Q_BLOCK, 1), 0)
    pos = jnp.concatenate([pos] * GROUP, axis=0)
    start = pl.multiple_of(jnp.maximum(i * Q_BLOCK + Q_BLOCK - span, 0), Q_BLOCK)
    kpos = start + lax.broadcasted_iota(jnp.int32, (1, span), 1)
    diff = pos - kpos
    ok = (diff >= 0) & (diff < WINDOW)
    for g in range(KV_HEADS):
        q = jnp.concatenate(
            [q_ref[0, :, (g * GROUP + r) * HEAD_DIM:(g * GROUP + r + 1) * HEAD_DIM] for r in range(GROUP)], axis=0)
        k = kv_ref[0, pl.ds(start, span), g * HEAD_DIM:(g + 1) * HEAD_DIM]
        v = kv_ref[0, pl.ds(start, span), (KV_HEADS + g) * HEAD_DIM:(KV_HEADS + g + 1) * HEAD_DIM]
        s = _dot_nt(q, k) * SCALE
        (p,) = _masked_softmax_parts([s], [ok])
        o = _dot(p.astype(BF16), v)
        for r in range(GROUP):
            hcol = (g * GROUP + r) * HEAD_DIM
            o_ref[0, :, hcol:hcol + HEAD_DIM] = o[r * Q_BLOCK:(r + 1) * Q_BLOCK]


def _win_prompt(qq3, kvb3):
    b, t, _ = qq3.shape
    span = min(WINDOW + Q_BLOCK, t)
    return pl.pallas_call(
        functools.partial(_win_prompt_kernel, span=span),
        grid=(b, t // Q_BLOCK),
        in_specs=[pl.BlockSpec((1, Q_BLOCK, 1024), lambda bi, i: (bi, i, 1)),
                  pl.BlockSpec((1, t, 512), lambda bi, i: (bi, 0, 1))],
        out_specs=pl.BlockSpec((1, Q_BLOCK, 1024), lambda bi, i: (bi, i, 0)),
        out_shape=jax.ShapeDtypeStruct((b, t, 1024), F32),
        compiler_params=_params("parallel", "arbitrary"),
        name="win_prompt",
    )(qq3, kvb3)


def _pool_kernel(prev_ref, u_ref, w_ref, sc_ref, o_ref, *, base, zero_first_prev):
    i = pl.program_id(1)
    tq = u_ref.shape[1]
    cur = u_ref[0]
    prev = prev_ref[0]
    if zero_first_prev:
        prev = jnp.where(i == 0, 0.0, prev)
    ext = jnp.concatenate([prev, cur], axis=0)
    gpos = base + i * tq + lax.broadcasted_iota(jnp.int32, (tq, 1), 0)
    gw = ext.shape[1] // len(POOL_WINDOWS)
    for gi, w in enumerate(POOL_WINDOWS):
        s = ext[:, gi * gw:(gi + 1) * gw]
        span = 1
        while span < w:
            s = s + pltpu.roll(s, span, 0)
            span *= 2
        cnt = jnp.minimum(gpos + 1, w).astype(F32)
        d = s[POOL_PREV:] / cnt - cur[:, gi * gw:(gi + 1) * gw]
        o = _dot(d.astype(BF16), w_ref[gi]) * sc_ref[:, gi * gw:(gi + 1) * gw]
        o_ref[0, :, gi * gw:(gi + 1) * gw] = o


def _pool(prev, u3, w, scale, *, base, zero_first_prev, tq=512):
    b, t, c = u3.shape
    tq = min(tq, t)
    ratio = tq // POOL_PREV
    if zero_first_prev:
        prev_map = lambda bi, i: (bi, jnp.maximum(i * ratio - 1, 0), 0)
    else:
        prev_map = lambda bi, i: (bi, 0, 0)
    ng = len(POOL_WINDOWS)
    return pl.pallas_call(
        functools.partial(_pool_kernel, base=base, zero_first_prev=zero_first_prev),
        grid=(b, t // tq),
        in_specs=[pl.BlockSpec((1, POOL_PREV, c), prev_map),
                  pl.BlockSpec((1, tq, c), lambda bi, i: (bi, i, 0)),
                  pl.BlockSpec((ng, c // ng, c // ng), lambda bi, i: (0, 0, 0)),
                  pl.BlockSpec((1, c), lambda bi, i: (0, 0))],
        out_specs=pl.BlockSpec((1, tq, c), lambda bi, i: (bi, i, 0)),
        out_shape=jax.ShapeDtypeStruct((b, t, c), F32),
        compiler_params=_params("parallel", "parallel"),
        name="pool_mix",
    )(prev, u3, w, scale)


def _mix_kernel(oc_ref, os_ref, ow_ref, gt_ref, po_ref, ga_ref, gb_ref, wa_ref, wp_ref, mix_ref):
    gt = gt_ref[...]
    cols = []
    for h in range(N_HEADS):
        sl = slice(h * HEAD_DIM, (h + 1) * HEAD_DIM)
        o = (gt[:, h:h + 1] * oc_ref[:, sl] + gt[:, N_HEADS + h:N_HEADS + h + 1] * os_ref[:, sl]
             + gt[:, 2 * N_HEADS + h:2 * N_HEADS + h + 1] * ow_ref[:, sl])
        cols.append(o.astype(BF16))
    a = _dot(jnp.concatenate(cols, axis=1), wa_ref[...])
    p = _dot(po_ref[...].astype(BF16), wp_ref[...])
    mix_ref[...] = (ga_ref[...] * a + gb_ref[...] * p).astype(BF16)


def _mix(oc, os_, ow, gt, po, gab, wa, wp, *, tm=ROW_TILE):
    m = oc.shape[0]
    d = wa.shape[1]
    tm = min(tm, m)
    row = lambda i: (i, 0)
    const = lambda i: (0, 0)
    return pl.pallas_call(
        _mix_kernel,
        grid=(m // tm,),
        in_specs=[pl.BlockSpec((tm, 1024), row), pl.BlockSpec((tm, 1024), row), pl.BlockSpec((tm, 1024), row),
                  pl.BlockSpec((tm, 128), row), pl.BlockSpec((tm, 1024), row),
                  pl.BlockSpec((tm, d), lambda i: (i, 0)), pl.BlockSpec((tm, d), lambda i: (i, 1)),
                  pl.BlockSpec((1024, d), const), pl.BlockSpec((1024, d), const)],
        out_specs=pl.BlockSpec((tm, d), row),
        out_shape=jax.ShapeDtypeStruct((m, d), BF16),
        compiler_params=_params("parallel"),
        name="branch_mix",
    )(oc, os_, ow, gt, po, gab, gab, wa, wp)


def _out_kernel(x_ref, mix_ref, w_ref, o_ref):
    o_ref[...] = x_ref[...] + _dot(mix_ref[...], w_ref[...])


def _out_proj(x, mix, w, *, tm=ROW_TILE):
    m, d = x.shape
    tm = min(tm, m)
    return pl.pallas_call(
        _out_kernel,
        grid=(m // tm,),
        in_specs=[pl.BlockSpec((tm, d), lambda i: (i, 0)), pl.BlockSpec((tm, d), lambda i: (i, 0)),
                  pl.BlockSpec((d, d), lambda i: (0, 0))],
        out_specs=pl.BlockSpec((tm, d), lambda i: (i, 0)),
        out_shape=jax.ShapeDtypeStruct((m, d), F32),
        compiler_params=_params("parallel"),
        name="out_proj",
    )(x, mix, w)


def _pair_candidates(sv, si):
    k = PEER_TOPK
    row = lax.broadcasted_iota(jnp.int32, (8, 1), 0)
    vals, ids = [], []
    for a in range(k // 2):
        lim = k // (a + 1)
        nrow = k if a == 0 else 8
        v = sv[0][a:a + 1] + sv[1][:nrow]
        if lim < nrow:
            v = jnp.where(row < lim, v, NEG_INF)
        vals.append(v)
        ids.append(si[0][a:a + 1] * float(PEER_NKEYS) + si[1][:nrow])
    vals.append(sv[0][k // 2:] + sv[1][0:1])
    ids.append(si[0][k // 2:] * float(PEER_NKEYS) + si[1][0:1])
    return jnp.concatenate(vals, axis=0), jnp.concatenate(ids, axis=0)


def _peer_score_kernel(x_ref, g_ref, wq_ref, keys_ref, ids_ref, gw_ref):
    h = _rms(x_ref[...], g_ref[...]).astype(BF16)
    q = _dot(h, wq_ref[...]).astype(BF16)
    dk = PEER_NKEYS
    ids, gws = [], []
    for hd in range(PEER_HEADS):
        sv, si = [], []
        for c in range(2):
            qhc = q[:, (hd * 2 + c) * dk:(hd * 2 + c + 1) * dk]
            st = _dot_nt(keys_ref[hd * 2 + c], qhc)
            v, ix = _topk_sorted(st, PEER_TOPK)
            sv.append(v)
            si.append(ix)
        comb, eid = _pair_candidates(sv, si)
        cv, ce = _topk_sorted(comb, PEER_TOPK, payload=eid)
        e = jnp.exp(cv - cv[0:1])
        gws.append(e / jnp.sum(e, axis=0, keepdims=True))
        ids.append(ce)
    ids_ref[...] = jnp.concatenate(ids, axis=0).T.astype(jnp.int32)
    gw_ref[...] = jnp.concatenate(gws, axis=0)


def _peer_score(x, g, wq, keys, *, tb=128):
    m, d = x.shape
    nk = PEER_HEADS * PEER_TOPK
    return pl.pallas_call(
        _peer_score_kernel,
        grid=(m // tb,),
        in_specs=[pl.BlockSpec((tb, d), lambda i: (i, 0)), pl.BlockSpec((1, d), lambda i: (0, 0)),
                  pl.BlockSpec(wq.shape, lambda i: (0, 0)),
                  pl.BlockSpec(keys.shape, lambda i: (0, 0, 0))],
        out_specs=[pl.BlockSpec((tb, nk), lambda i: (i, 0)), pl.BlockSpec((nk, tb), lambda i: (0, i))],
        out_shape=[jax.ShapeDtypeStruct((m, nk), jnp.int32), jax.ShapeDtypeStruct((nk, m), F32)],
        compiler_params=_params("parallel"),
        name="peer_score",
    )(x, g, wq, keys)


def _gelu(x):
    return 0.5 * x * (1.0 + lax.erf(x * (2.0 ** -0.5)))


PACK_ROWS = 256


def _pack_kernel(u_ref, v_ref, o_ref):
    lo = pltpu.bitcast(u_ref[...].astype(BF16).astype(F32), jnp.uint32) >> 16
    hi = pltpu.bitcast(v_ref[...].astype(BF16).astype(F32), jnp.uint32) & jnp.uint32(0xFFFF0000)
    o_ref[...] = (lo | hi).reshape(o_ref.shape)


def _pack_tables(u, v):
    e, d = u.shape
    return pl.pallas_call(
        _pack_kernel,
        grid=(e // PACK_ROWS,),
        in_specs=[pl.BlockSpec((PACK_ROWS, d), lambda i: (i, 0)), pl.BlockSpec((PACK_ROWS, d), lambda i: (i, 0))],
        out_specs=pl.BlockSpec((PACK_ROWS, 1, d), lambda i: (i, 0, 0)),
        out_shape=jax.ShapeDtypeStruct((e, 1, d), jnp.uint32),
        compiler_params=_params("parallel"),
        name="peer_pack",
    )(u, v)


PEER_RING = 8


def _peer_apply_kernel(ids_ref, nxt_ref, x_ref, gffn_ref, gw_ref, gfin_ref, tab_ref, y_ref, *scratch):
    bufs = scratch[:PEER_RING]
    sem, hbuf, obuf = scratch[PEER_RING:]
    step = pl.program_id(0)
    tb, d = x_ref.shape
    nk = gw_ref.shape[0]
    nchunk = d // 128
    ahead = PEER_RING - 1
    n_groups = tb // PEER_RING
    hbuf[...] = _rms(x_ref[...], gffn_ref[...])

    def row_copy(ids, t, k, slot):
        return pltpu.make_async_copy(tab_ref.at[ids[t, k]], bufs[slot].at[pl.ds(k, 1), :], sem.at[slot])

    def issue(ids, t, slot):
        for k in range(nk):
            row_copy(ids, t, k, slot).start(priority=k % 2)

    def drain(slot):
        for k in range(nk):
            row_copy(ids_ref, 0, k, slot).wait()

    @pl.when(step == 0)
    def _():
        for s in range(ahead):
            issue(ids_ref, s, s)

    lane = lax.broadcasted_iota(jnp.int32, (nk, tb), 1)

    def compute(t, slot):
        buf = bufs[slot]
        hrow = hbuf[pl.ds(t, 1), :]
        acc = jnp.zeros((nk, 128), F32)
        for j in range(nchunk):
            w = buf[:, j * 128:(j + 1) * 128]
            u = pltpu.bitcast(w << 16, F32)
            acc = acc + u * hrow[:, j * 128:(j + 1) * 128]
        act = jnp.sum(acc, axis=1, keepdims=True)
        gcol = jnp.sum(jnp.where(lane == t, gw_ref[...], 0.0), axis=1, keepdims=True)
        coef = _gelu(act) * gcol
        outs = []
        for j in range(nchunk):
            w = buf[:, j * 128:(j + 1) * 128]
            v = pltpu.bitcast(w & jnp.uint32(0xFFFF0000), F32)
            outs.append(jnp.sum(v * coef, axis=0, keepdims=True))
        obuf[pl.ds(t, 1), :] = jnp.concatenate(outs, axis=1)

    def group(p, last):
        for s in range(PEER_RING):
            t = p * PEER_RING + s
            drain(s)
            if last and s > 0:
                issue(nxt_ref, s - 1, (s + ahead) % PEER_RING)
            else:
                issue(ids_ref, t + ahead, (s + ahead) % PEER_RING)
            compute(t, s)

    def body(p, carry):
        group(p, False)
        return carry

    lax.fori_loop(0, n_groups - 1, body, 0)
    group(n_groups - 1, True)

    @pl.when(step == pl.num_programs(0) - 1)
    def _():
        for s in range(ahead):
            drain(s)

    y_ref[...] = _rms(x_ref[...] + obuf[...], gfin_ref[...])


def _peer_apply(ids, x, gffn, gw, gfin, table, *, tb=128):
    m, d = x.shape
    nk = ids.shape[1]
    steps = m // tb
    return pl.pallas_call(
        _peer_apply_kernel,
        grid=(steps,),
        in_specs=[pl.BlockSpec((tb, nk), lambda i: (i, 0), memory_space=pltpu.SMEM),
                  pl.BlockSpec((tb, nk), lambda i: (jnp.minimum(i + 1, steps - 1), 0), memory_space=pltpu.SMEM),
                  pl.BlockSpec((tb, d), lambda i: (i, 0)),
                  pl.BlockSpec((1, d), lambda i: (0, 0)),
                  pl.BlockSpec((nk, tb), lambda i: (0, i)),
                  pl.BlockSpec((1, d), lambda i: (0, 0)),
                  pl.BlockSpec(memory_space=pl.ANY)],
        out_specs=pl.BlockSpec((tb, d), lambda i: (i, 0)),
        out_shape=jax.ShapeDtypeStruct((m, d), F32),
        scratch_shapes=[pltpu.VMEM((nk, d), jnp.uint32)] * PEER_RING + [
            pltpu.SemaphoreType.DMA((PEER_RING,)), pltpu.VMEM((tb, d), F32), pltpu.VMEM((tb, d), F32)],
        compiler_params=_params("arbitrary"),
        name="peer_apply",
    )(ids, ids, x, gffn, gw, gfin, table)


def _stack_heads(q_ref, g):
    return jnp.concatenate(
        [q_ref[0, :, (g * GROUP + r) * HEAD_DIM:(g * GROUP + r + 1) * HEAD_DIM] for r in range(GROUP)], axis=0)


NEW_PAD = 128


def _pad_new(x):
    return jnp.concatenate([x, jnp.zeros((NEW_PAD - x.shape[0], x.shape[1]), F32)], axis=0).astype(BF16)


def _cmp_sample_kernel(q_ref, y_ref, o_ref, sel_ref, kcp_ref, tmp_ref, *, past):
    _combine_compressed(y_ref, kcp_ref, tmp_ref)
    t = q_ref.shape[1]
    nb = kcp_ref.shape[1]
    tpos = past + lax.broadcasted_iota(jnp.int32, (t, 1), 0)
    pos = jnp.concatenate([tpos] * GROUP, axis=0)
    blk = lax.broadcasted_iota(jnp.int32, (1, nb), 1)
    for g in range(KV_HEADS):
        q = _stack_heads(q_ref, g)
        outs, imp = _cmp_attention([q], pos, kcp_ref, g)
        o = outs[0]
        for r in range(GROUP):
            hcol = (g * GROUP + r) * HEAD_DIM
            o_ref[0, :, hcol:hcol + HEAD_DIM] = o[r * t:(r + 1) * t]
        imp_t = functools.reduce(jnp.add, [imp[r * t:(r + 1) * t] for r in range(GROUP)])
        v = jnp.where(blk == 0, FORCE_SCORE, imp_t)
        sel = _topk_select(v, SEL_TOPN - 1, 1)
        sel_ref[0, :, g * nb:(g + 1) * nb] = sel.astype(BF16)


def _cmp_sample(qq3, y, *, past):
    b, t, _ = qq3.shape
    ns = y.shape[1]
    nb = ns // CMP_PER_SEL
    return pl.pallas_call(
        functools.partial(_cmp_sample_kernel, past=past),
        grid=(b,),
        in_specs=[pl.BlockSpec((1, t, 1024), lambda bi: (bi, 0, 0)),
                  pl.BlockSpec((1, ns, 1024), lambda bi: (bi, 0, 0))],
        out_specs=[pl.BlockSpec((1, t, 1024), lambda bi: (bi, 0, 0)),
                   pl.BlockSpec((1, t, KV_HEADS * nb), lambda bi: (bi, 0, 0))],
        out_shape=[jax.ShapeDtypeStruct((b, t, 1024), F32),
                   jax.ShapeDtypeStruct((b, t, KV_HEADS * nb), BF16)],
        scratch_shapes=[pltpu.VMEM((2 * KV_HEADS * CMP_PER_SEL, nb, HEAD_DIM), BF16),
                        pltpu.VMEM((ns, HEAD_DIM), F32)],
        compiler_params=_params("parallel"),
        name="cmp_sample",
    )(qq3, y)


def _sel_sample_kernel(q_ref, sel_ref, kv_ref, new_ref, o_ref, m_ref, l_ref, acc_ref):
    c = pl.program_id(1)
    nc = pl.num_programs(1)
    t = q_ref.shape[1]
    nb = sel_ref.shape[2] // KV_HEADS
    width = kv_ref.shape[1]

    @pl.when(c == 0)
    def _():
        m_ref[...] = jnp.full(m_ref.shape, NEG_INF, F32)
        l_ref[...] = jnp.zeros(l_ref.shape, F32)
        acc_ref[...] = jnp.zeros(acc_ref.shape, F32)

    def update(g, s, ok, v):
        s = jnp.where(ok, s, NEG_INF)
        m = m_ref[g]
        m_new = jnp.maximum(m, jnp.max(s, axis=-1, keepdims=True))
        m_safe = jnp.where(m_new == NEG_INF, 0.0, m_new)
        p = jnp.exp(s - m_safe)
        alpha = jnp.exp(m - m_safe)
        l_ref[g] = alpha * l_ref[g] + jnp.sum(p, axis=-1, keepdims=True)
        acc_ref[g] = alpha * acc_ref[g] + _dot(p.astype(BF16), v)
        m_ref[g] = m_new

    for g in range(KV_HEADS):
        q = _stack_heads(q_ref, g)
        k = kv_ref[0, :, g * HEAD_DIM:(g + 1) * HEAD_DIM]
        v = kv_ref[0, :, (KV_HEADS + g) * HEAD_DIM:(KV_HEADS + g + 1) * HEAD_DIM]
        s = _dot_nt(q, k) * SCALE
        picked = _expand_blocks(sel_ref[0, :, g * nb:(g + 1) * nb], c * (width // SEL_BLOCK), width)
        picked = jnp.concatenate([picked] * GROUP, axis=0)
        update(g, s, picked > 0.5, v)

    @pl.when(c == nc - 1)
    def _():
        qi = lax.broadcasted_iota(jnp.int32, (t, 1), 0)
        qi = jnp.concatenate([qi] * GROUP, axis=0)
        ok = lax.broadcasted_iota(jnp.int32, (1, NEW_PAD), 1) <= qi
        for g in range(KV_HEADS):
            q = _stack_heads(q_ref, g)
            k = _pad_new(new_ref[pl.ds(2 * KV_HEADS + g, t, stride=ROWS_PER_POS), :])
            v = _pad_new(new_ref[pl.ds(3 * KV_HEADS + g, t, stride=ROWS_PER_POS), :])
            update(g, _dot_nt(q, k) * SCALE, ok, v)
            o = acc_ref[g] * (1.0 / jnp.maximum(l_ref[g], 1e-30))
            for r in range(GROUP):
                hcol = (g * GROUP + r) * HEAD_DIM
                o_ref[0, :, hcol:hcol + HEAD_DIM] = o[r * t:(r + 1) * t]


SEL_SAMPLE_CHUNK = 2048


def _sel_sample(kvs, qq3, sel, nsa_rows):
    nbatch, past, _ = kvs.shape
    t = qq3.shape[1]
    nb2 = sel.shape[2]
    rows = GROUP * t
    width = min(SEL_SAMPLE_CHUNK, past)
    return pl.pallas_call(
        _sel_sample_kernel,
        grid=(nbatch, past // width),
        in_specs=[pl.BlockSpec((1, t, 1024), lambda bi, ci: (bi, 0, 1)),
                  pl.BlockSpec((1, t, nb2), lambda bi, ci: (bi, 0, 0)),
                  pl.BlockSpec((1, width, 512), lambda bi, ci: (bi, ci, 0)),
                  pl.BlockSpec((t * ROWS_PER_POS, HEAD_DIM), lambda bi, ci: (bi, 0))],
        out_specs=pl.BlockSpec((1, t, 1024), lambda bi, ci: (bi, 0, 0)),
        out_shape=jax.ShapeDtypeStruct((nbatch, t, 1024), F32),
        scratch_shapes=[pltpu.VMEM((KV_HEADS, rows, 1), F32), pltpu.VMEM((KV_HEADS, rows, 1), F32),
                        pltpu.VMEM((KV_HEADS, rows, HEAD_DIM), F32)],
        compiler_params=_params("parallel", "arbitrary"),
        name="sel_sample",
    )(qq3, sel, kvs, nsa_rows)


def _win_sample_kernel(q_ref, st_ref, new_ref, o_ref):
    t = q_ref.shape[1]
    nw = st_ref.shape[0] // WIN_ROWS_PER_POS
    old = lambda comp: st_ref[pl.ds(comp, nw, stride=WIN_ROWS_PER_POS), :].astype(BF16)
    new = lambda comp: _pad_new(new_ref[pl.ds(comp, t, stride=WIN_ROWS_PER_POS), :])
    qi = lax.broadcasted_iota(jnp.int32, (t, 1), 0)
    qi = jnp.concatenate([qi] * GROUP, axis=0)
    d_old = (nw + qi) - lax.broadcasted_iota(jnp.int32, (1, nw), 1)
    ok_old = (d_old >= 0) & (d_old < WINDOW)
    d_new = qi - lax.broadcasted_iota(jnp.int32, (1, NEW_PAD), 1)
    ok_new = (d_new >= 0) & (d_new < WINDOW)
    for g in range(KV_HEADS):
        q = _stack_heads(q_ref, g)
        k_old, v_old = old(g), old(KV_HEADS + g)
        k_new, v_new = new(g), new(KV_HEADS + g)
        p_old, p_new = _masked_softmax_parts(
            [_dot_nt(q, k_old) * SCALE, _dot_nt(q, k_new) * SCALE], [ok_old, ok_new])
        o = _dot(p_old.astype(BF16), v_old) + _dot(p_new.astype(BF16), v_new)
        for r in range(GROUP):
            hcol = (g * GROUP + r) * HEAD_DIM
            o_ref[0, :, hcol:hcol + HEAD_DIM] = o[r * t:(r + 1) * t]


def _win_sample(qq3, state_rows, win_rows):
    b, t, _ = qq3.shape
    nw = state_rows.shape[0] // (b * WIN_ROWS_PER_POS)
    return pl.pallas_call(
        _win_sample_kernel,
        grid=(b,),
        in_specs=[pl.BlockSpec((1, t, 1024), lambda bi: (bi, 0, 1)),
                  pl.BlockSpec((nw * WIN_ROWS_PER_POS, HEAD_DIM), lambda bi: (bi, 0)),
                  pl.BlockSpec((t * WIN_ROWS_PER_POS, HEAD_DIM), lambda bi: (bi, 0))],
        out_specs=pl.BlockSpec((1, t, 1024), lambda bi: (bi, 0, 0)),
        out_shape=jax.ShapeDtypeStruct((b, t, 1024), F32),
        compiler_params=_params("parallel"),
        name="win_sample",
    )(qq3, state_rows, win_rows)


def _rope_tables(pos):
    half = HEAD_DIM // 2
    inv = ROPE_THETA ** (-jnp.arange(half, dtype=F32) / half)
    ang = pos.astype(F32)[:, None] * inv[None, :]
    cos, sin = jnp.cos(ang), jnp.sin(ang)
    return jnp.concatenate([cos, cos], axis=-1), jnp.concatenate([-sin, sin], axis=-1)


def _prep_weights(w_in, w_phi_k, w_phi_v, w_pool_group, pool_scale, w_branch_attn, w_branch_pool, w_out,
                  peer_w_query, peer_sub_keys, peer_u, peer_v):
    d = w_in.shape[0]
    qw = N_HEADS * HEAD_DIM
    kvw = 6 * KV_HEADS * HEAD_DIM
    ngw = 3 * N_HEADS
    pw = d // 2
    o1, o2, o3, o4 = qw, qw + kvw, qw + kvw + ngw, qw + kvw + ngw + pw
    wb = w_in.astype(BF16)
    w_qkv = jnp.concatenate([wb[:, :o2], wb[:, o2:o3], jnp.zeros((d, HEAD_DIM - ngw), BF16)], axis=1)
    cat = lambda w: jnp.concatenate([w[:CMP_STRIDE], w[CMP_STRIDE:]], axis=-1).astype(BF16)
    return dict(
        w_qkv=w_qkv, w_u=wb[:, o3:o4], w_gab=wb[:, o4:],
        wk=cat(w_phi_k), wv=cat(w_phi_v),
        w_pool=w_pool_group.astype(BF16), pool_scale=pool_scale.reshape(1, -1),
        w_ba=w_branch_attn.astype(BF16), w_bp=w_branch_pool.astype(BF16), w_out=w_out.astype(BF16),
        w_query=peer_w_query.astype(BF16),
        keys=peer_sub_keys.reshape(PEER_HEADS * 2, PEER_NKEYS, -1).astype(BF16),
        table=_pack_tables(peer_u, peer_v),
    )


def _token_tail(x2d, o_cmp, o_sel, o_win, gates, pool_out, gab, wp, g_ffn, g_final):
    mix = _mix(o_cmp, o_sel, o_win, gates, pool_out, gab, wp["w_ba"], wp["w_bp"])
    x2 = _out_proj(x2d, mix, wp["w_out"])
    ids, gw = _peer_score(x2, g_ffn, wp["w_query"], wp["keys"])
    return _peer_apply(ids, x2, g_ffn, gw, g_final, wp["table"])


def kernel(x_prompt, x_sample, cache_kv_nsa, state_win_kv, state_pool, page_table, g_norm_mix, w_in, w_phi_k,
           w_phi_v, w_pool_group, pool_scale, w_branch_attn, w_branch_pool, w_out, g_norm_ffn, peer_w_query,
           peer_sub_keys, peer_u, peer_v, g_norm_final):
    assert g_norm_mix.shape[0] == 1, "single-layer step"
    bp, t, d = x_prompt.shape
    bs, ts, _ = x_sample.shape
    n_pages = page_table.shape[1]
    page_size = cache_kv_nsa.shape[2]
    past = n_pages * page_size
    wp = _prep_weights(w_in[0], w_phi_k[0], w_phi_v[0], w_pool_group[0], pool_scale[0], w_branch_attn[0],
                       w_branch_pool[0], w_out[0], peer_w_query[0], peer_sub_keys[0], peer_u[0], peer_v[0])
    g_mix = g_norm_mix[0].reshape(1, d)
    g_ffn = g_norm_ffn[0].reshape(1, d)
    g_fin = g_norm_final.reshape(1, d)

    xp = x_prompt.reshape(bp * t, d)
    cos, sin = _rope_tables(jnp.arange(t))
    cos, sin = jnp.tile(cos, (bp, 1)), jnp.tile(sin, (bp, 1))
    qq, nsa, win, gates, kvb = _qkv_proj(xp, g_mix, wp["w_qkv"], cos, sin)
    u = _norm_proj(xp, g_mix, wp["w_u"])
    gab = _norm_proj(xp, g_mix, wp["w_gab"], act="sigmoid", out_dtype=BF16)
    qq3, kvb3 = qq.reshape(bp, t, -1), kvb.reshape(bp, t, -1)
    y = _compress_prompt(nsa, bp, wp["wk"], wp["wv"])
    o_cmp, sel = _cmp_prompt(qq3, y)
    o_sel = _sel_prompt(qq3, sel, kvb3)
    o_win = _win_prompt(qq3, kvb3)
    u3 = u.reshape(bp, t, -1)
    pool_out = _pool(u3, u3, wp["w_pool"], wp["pool_scale"], base=0, zero_first_prev=True)
    y_prompt = _token_tail(xp, o_cmp.reshape(bp * t, -1), o_sel.reshape(bp * t, -1), o_win.reshape(bp * t, -1),
                           gates, pool_out.reshape(bp * t, -1), gab, wp, g_ffn, g_fin)
    wlen = min(WINDOW, t)
    new_kv_p = nsa.reshape(1, bp, t, 4, KV_HEADS, HEAD_DIM)
    new_win_p = win.reshape(bp, t, 2, KV_HEADS, HEAD_DIM)[None, :, t - wlen:]
    new_pool_p = u3[None, :, t - (POOL_PREV - 1):]

    xs = x_sample.reshape(bs * ts, d)
    cos_s, sin_s = _rope_tables(past + jnp.arange(ts))
    cos_s, sin_s = jnp.tile(cos_s, (bs, 1)), jnp.tile(sin_s, (bs, 1))
    qq_s, nsa_s, win_s, gates_s, _ = _qkv_proj(xs, g_mix, wp["w_qkv"], cos_s, sin_s)
    u_s = _norm_proj(xs, g_mix, wp["w_u"])
    gab_s = _norm_proj(xs, g_mix, wp["w_gab"], act="sigmoid", out_dtype=BF16)
    qq_s3 = qq_s.reshape(bs, ts, -1)
    cache_rows = cache_kv_nsa.reshape(-1, HEAD_DIM)
    y_s, kvs = _compress_pages(cache_rows, page_table, page_size, wp["wk"], wp["wv"])
    o_cmp_s, sel_s = _cmp_sample(qq_s3, y_s, past=past)
    o_sel_s = _sel_sample(kvs, qq_s3, sel_s, nsa_s)
    o_win_s = _win_sample(qq_s3, state_win_kv.reshape(-1, HEAD_DIM), win_s)
    u_s3 = u_s.reshape(bs, ts, -1)
    st_pool = state_pool[0]
    prev = jnp.pad(st_pool, ((0, 0), (POOL_PREV - st_pool.shape[1], 0), (0, 0)))
    pool_out_s = _pool(prev, u_s3, wp["w_pool"], wp["pool_scale"], base=st_pool.shape[1], zero_first_prev=False)
    y_sample = _token_tail(xs, o_cmp_s.reshape(bs * ts, -1), o_sel_s.reshape(bs * ts, -1),
                           o_win_s.reshape(bs * ts, -1), gates_s, pool_out_s.reshape(bs * ts, -1), gab_s, wp,
                           g_ffn, g_fin)
    new_kv_s = nsa_s.reshape(1, bs, ts, 4, KV_HEADS, HEAD_DIM)
    win_ext = jnp.concatenate([state_win_kv, win_s.reshape(1, bs, ts, 2, KV_HEADS, HEAD_DIM)], axis=2)
    new_win_s = win_ext[:, :, win_ext.shape[2] - min(WINDOW, win_ext.shape[2]):]
    pool_ext = jnp.concatenate([st_pool, u_s3], axis=1)
    new_pool_s = pool_ext[None, :, pool_ext.shape[1] - (POOL_PREV - 1):]

    return (y_prompt.reshape(bp, t, d), y_sample.reshape(bs, ts, d), new_kv_p, new_kv_s, new_win_p, new_win_s,
            new_pool_p, new_pool_s)
```

```python
import functools

import jax
import jax.numpy as jnp
from jax import lax
from jax.experimental import pallas as pl
from jax.experimental.pallas import tpu as pltpu

F32 = jnp.float32
BF16 = jnp.bfloat16

HEAD_DIM = 128
N_HEADS = 8
KV_HEADS = 2
GROUP = N_HEADS // KV_HEADS
CMP_LEN = 32
CMP_STRIDE = 16
SEL_BLOCK = 64
SEL_TOPN = 16
CMP_PER_SEL = SEL_BLOCK // CMP_STRIDE
WINDOW = 512
Q_BLOCK = 128
ROPE_THETA = 10000.0
FORCE_SCORE = 1e4
POOL_WINDOWS = (2, 4, 8, 16)
POOL_PREV = 16
PEER_HEADS = 8
PEER_NKEYS = 128
PEER_TOPK = 16
EPS = 1e-6
SCALE = HEAD_DIM ** -0.5
LOG2E = 1.4426950408889634
NEG_INF = float("-inf")

ROWS_PER_POS = 4 * KV_HEADS
WIN_ROWS_PER_POS = 2 * KV_HEADS

ROW_TILE = 512
VMEM_LIMIT = 56 * 1024 * 1024


def _params(*sem):
    return pltpu.CompilerParams(dimension_semantics=sem, vmem_limit_bytes=VMEM_LIMIT)


def _dot(a, b):
    return jnp.dot(a, b, preferred_element_type=F32)


def _dot_nt(a, b):
    return lax.dot_general(a, b, (((1,), (1,)), ((), ())), preferred_element_type=F32)


def _rms(x, g):
    return x * lax.rsqrt(jnp.mean(x * x, axis=-1, keepdims=True) + EPS) * g


def _masked_softmax_parts(parts, masks):
    parts = [jnp.where(mk, s, NEG_INF) for s, mk in zip(parts, masks)]
    same = all(s.shape == parts[0].shape for s in parts)

    def across(op, red, xs):
        if same:
            return red(functools.reduce(op, xs), axis=-1, keepdims=True)
        return functools.reduce(op, [red(x, axis=-1, keepdims=True) for x in xs])

    m = across(jnp.maximum, jnp.max, parts)
    m = jnp.where(m == NEG_INF, 0.0, m)
    es = [jnp.exp(s - m) for s in parts]
    den = across(jnp.add, jnp.sum, es)
    inv = 1.0 / jnp.maximum(den, 1e-30)
    return [e * inv for e in es]


def _topk_select(v, n, axis):
    size = v.shape[axis]
    idx = lax.broadcasted_iota(jnp.int32, v.shape, axis).astype(F32)
    sel = jnp.zeros(v.shape, F32)
    for _ in range(n):
        m = jnp.max(v, axis=axis, keepdims=True)
        first = jnp.min(jnp.where(v == m, idx, float(size)), axis=axis, keepdims=True)
        hit = idx == first
        v = jnp.where(hit, NEG_INF, v)
        sel = jnp.where(hit, 1.0, sel)
    return sel


def _topk_sorted(v, n, payload=None):
    size = v.shape[0]
    idx = lax.broadcasted_iota(jnp.int32, v.shape, 0).astype(F32)
    vals, picks = [], []
    for _ in range(n):
        m = jnp.max(v, axis=0, keepdims=True)
        first = jnp.min(jnp.where(v == m, idx, float(size)), axis=0, keepdims=True)
        hit = idx == first
        vals.append(m)
        if payload is None:
            picks.append(first)
        else:
            picks.append(jnp.max(jnp.where(hit, payload, -1.0), axis=0, keepdims=True))
        v = jnp.where(hit, NEG_INF, v)
    return jnp.concatenate(vals, axis=0), jnp.concatenate(picks, axis=0)


def _proj_kernel(h_ref, w_ref, o_ref, *, act, tn):
    h = h_ref[...]
    for c in range(w_ref.shape[1] // tn):
        z = _dot(h, w_ref[:, c * tn:(c + 1) * tn])
        if act == "sigmoid":
            z = jax.nn.sigmoid(z)
        o_ref[:, c * tn:(c + 1) * tn] = z.astype(o_ref.dtype)


def _proj(h, w, *, act=None, out_dtype=F32, tm=ROW_TILE, tn=512, wn=2048):
    m, d = h.shape
    n = w.shape[1]
    tm = min(tm, m)
    wn = min(wn, n)
    return pl.pallas_call(
        functools.partial(_proj_kernel, act=act, tn=tn),
        grid=(n // wn, m // tm),
        in_specs=[pl.BlockSpec((tm, d), lambda j, i: (i, 0)),
                  pl.BlockSpec((d, wn), lambda j, i: (0, j))],
        out_specs=pl.BlockSpec((tm, wn), lambda j, i: (i, j)),
        out_shape=jax.ShapeDtypeStruct((m, n), out_dtype),
        compiler_params=_params("parallel", "parallel"),
        name="mixer_proj",
    )(h, w)


QKV_COLS = N_HEADS * HEAD_DIM + 6 * KV_HEADS * HEAD_DIM + HEAD_DIM


def _qkv_kernel(x_ref, g_ref, w_ref, cos_ref, sin_ref, qq_ref, nsa_ref, win_ref, gate_ref, kvb_ref, h_ref):
    h = _rms(x_ref[...], g_ref[...]).astype(BF16)
    h_ref[...] = h
    cos = cos_ref[...]
    sin = sin_ref[...]

    def rope(z):
        return z * cos + pltpu.roll(z, HEAD_DIM // 2, 1) * sin

    qw = N_HEADS * HEAD_DIM
    for c in range(qw // 512):
        z = _dot(h, w_ref[:, c * 512:(c + 1) * 512])
        for j in range(4):
            zh = z[:, j * 128:(j + 1) * 128]
            col = c * 512 + j * 128
            qq_ref[:, col:col + 128] = zh.astype(BF16)
            qq_ref[:, qw + col:qw + col + 128] = rope(zh).astype(BF16)
    tm = x_ref.shape[0]

    def put(ref, comp, n_comp, val):
        ref[pl.ds(comp, tm, stride=n_comp), :] = val

    for c in range(3):
        z = _dot(h, w_ref[:, qw + c * 512:qw + (c + 1) * 512])
        for j in range(2):
            zk = z[:, j * 128:(j + 1) * 128]
            zv = z[:, 256 + j * 128:256 + (j + 1) * 128]
            if c == 0:
                put(nsa_ref, j, ROWS_PER_POS, zk)
                put(nsa_ref, KV_HEADS + j, ROWS_PER_POS, zv)
                continue
            zr = rope(zk)
            if c == 1:
                put(nsa_ref, 2 * KV_HEADS + j, ROWS_PER_POS, zr)
                put(nsa_ref, 3 * KV_HEADS + j, ROWS_PER_POS, zv)
            else:
                put(win_ref, j, WIN_ROWS_PER_POS, zr)
                put(win_ref, KV_HEADS + j, WIN_ROWS_PER_POS, zv)
            base = (c - 1) * 512
            kvb_ref[:, base + j * 128:base + (j + 1) * 128] = zr.astype(BF16)
            kvb_ref[:, base + 256 + j * 128:base + 256 + (j + 1) * 128] = zv.astype(BF16)
    z = _dot(h, w_ref[:, qw + 1536:qw + 1536 + 128])
    gate_ref[...] = jax.nn.sigmoid(z)


def _qkv_proj(x, g, w, cos, sin, *, tm=ROW_TILE):
    m, d = x.shape
    tm = min(tm, m)
    row = lambda i: (i, 0)
    const = lambda i: (0, 0)
    return pl.pallas_call(
        _qkv_kernel,
        grid=(m // tm,),
        in_specs=[pl.BlockSpec((tm, d), row), pl.BlockSpec((1, d), const),
                  pl.BlockSpec((d, QKV_COLS), const),
                  pl.BlockSpec((tm, HEAD_DIM), row), pl.BlockSpec((tm, HEAD_DIM), row)],
        out_specs=[pl.BlockSpec((tm, 2048), row), pl.BlockSpec((tm * ROWS_PER_POS, HEAD_DIM), row),
                   pl.BlockSpec((tm * WIN_ROWS_PER_POS, HEAD_DIM), row), pl.BlockSpec((tm, 128), row),
                   pl.BlockSpec((tm, 1024), row), pl.BlockSpec((tm, d), row)],
        out_shape=[jax.ShapeDtypeStruct((m, 2048), BF16), jax.ShapeDtypeStruct((m * ROWS_PER_POS, HEAD_DIM), F32),
                   jax.ShapeDtypeStruct((m * WIN_ROWS_PER_POS, HEAD_DIM), F32), jax.ShapeDtypeStruct((m, 128), F32),
                   jax.ShapeDtypeStruct((m, 1024), BF16), jax.ShapeDtypeStruct((m, d), BF16)],
        compiler_params=_params("parallel"),
        name="qkv_proj",
    )(x, g, w, cos, sin)


SUB_ROWS = CMP_STRIDE * ROWS_PER_POS


def _compress_rows(load, n_rows, wk_ref, wv_ref, y_ref):
    for kv in range(2):
        w_ref = wk_ref if kv == 0 else wv_ref
        for g in range(KV_HEADS):
            comp = kv * KV_HEADS + g
            acc = jnp.zeros((n_rows, 2 * HEAD_DIM), F32)
            for i in range(CMP_STRIDE):
                acc = acc + _dot(load(i, comp).astype(BF16), w_ref[i])
            y_ref[0, :, 2 * comp * HEAD_DIM:2 * (comp + 1) * HEAD_DIM] = acc


def _compress_prompt_kernel(x_ref, wk_ref, wv_ref, y_ref):
    ts = x_ref.shape[0] // SUB_ROWS
    load = lambda i, comp: x_ref[pl.ds(i * ROWS_PER_POS + comp, ts, stride=SUB_ROWS), :]
    _compress_rows(load, ts, wk_ref, wv_ref, y_ref)


def _compress_prompt(nsa_rows, b, wk, wv, *, ts=128):
    n_sub = nsa_rows.shape[0] // (b * SUB_ROWS)
    ts = min(ts, n_sub)
    steps = n_sub // ts
    wspec = pl.BlockSpec((CMP_STRIDE, HEAD_DIM, 2 * HEAD_DIM), lambda bi, ci: (0, 0, 0))
    return pl.pallas_call(
        _compress_prompt_kernel,
        grid=(b, steps),
        in_specs=[pl.BlockSpec((ts * SUB_ROWS, HEAD_DIM), lambda bi, ci: (bi * steps + ci, 0)), wspec, wspec],
        out_specs=pl.BlockSpec((1, ts, 1024), lambda bi, ci: (bi, ci, 0)),
        out_shape=jax.ShapeDtypeStruct((b, n_sub, 1024), F32),
        compiler_params=_params("parallel", "parallel"),
        name="compress_prompt",
    )(nsa_rows, wk, wv)


PAGES_PER_STEP = 16


def _compress_pages_kernel(pt_ref, *refs):
    pages = refs[:PAGES_PER_STEP]
    wk_ref, wv_ref, y_ref, kvs_ref = refs[PAGES_PER_STEP:]
    per_page = pages[0].shape[0] // SUB_ROWS
    page_size = pages[0].shape[0] // ROWS_PER_POS

    def load(i, comp):
        rows = pl.ds(i * ROWS_PER_POS + comp, per_page, stride=SUB_ROWS)
        return jnp.concatenate([p[rows, :] for p in pages], axis=0)

    _compress_rows(load, PAGES_PER_STEP * per_page, wk_ref, wv_ref, y_ref)
    for j in range(2 * KV_HEADS):
        rows = pl.ds(2 * KV_HEADS + j, page_size, stride=ROWS_PER_POS)
        kvs_ref[0, :, j * HEAD_DIM:(j + 1) * HEAD_DIM] = jnp.concatenate(
            [p[rows, :] for p in pages], axis=0).astype(BF16)


def _page_specs(page_rows):
    def spec(k):
        return pl.BlockSpec((page_rows, HEAD_DIM), lambda bi, ci, pt: (pt[bi, ci * PAGES_PER_STEP + k], 0))
    return [spec(k) for k in range(PAGES_PER_STEP)]


def _compress_pages(cache_rows, page_table, page_size, wk, wv):
    nb, n_pages = page_table.shape
    per_page = page_size // CMP_STRIDE
    rows = PAGES_PER_STEP * per_page
    const3 = lambda bi, ci, pt: (0, 0, 0)
    grid_spec = pltpu.PrefetchScalarGridSpec(
        num_scalar_prefetch=1,
        grid=(nb, n_pages // PAGES_PER_STEP),
        in_specs=_page_specs(page_size * ROWS_PER_POS) + [
            pl.BlockSpec((CMP_STRIDE, HEAD_DIM, 2 * HEAD_DIM), const3),
            pl.BlockSpec((CMP_STRIDE, HEAD_DIM, 2 * HEAD_DIM), const3)],
        out_specs=[pl.BlockSpec((1, rows, 1024), lambda bi, ci, pt: (bi, ci, 0)),
                   pl.BlockSpec((1, PAGES_PER_STEP * page_size, 512), lambda bi, ci, pt: (bi, ci, 0))],
    )
    return pl.pallas_call(
        _compress_pages_kernel,
        grid_spec=grid_spec,
        out_shape=[jax.ShapeDtypeStruct((nb, n_pages * per_page, 1024), F32),
                   jax.ShapeDtypeStruct((nb, n_pages * page_size, 512), BF16)],
        compiler_params=_params("parallel", "parallel"),
        name="compress_pages",
    )(page_table, *([cache_rows] * PAGES_PER_STEP), wk, wv)


def _combine_compressed(y_ref, kcp_ref, tmp_ref):
    ns = y_ref.shape[1]
    nb = ns // CMP_PER_SEL
    last = lax.broadcasted_iota(jnp.int32, (ns, 1), 0) == ns - 1
    for a in range(2 * KV_HEADS):
        y1 = y_ref[0, :, a * 256:a * 256 + 128]
        y2 = y_ref[0, :, a * 256 + 128:(a + 1) * 256]
        nxt = jnp.where(last, 0.0, pltpu.roll(y2, ns - 1, 0))
        tmp_ref[...] = y1 + nxt
        for c in range(CMP_PER_SEL):
            kcp_ref[a * CMP_PER_SEL + c] = tmp_ref[pl.ds(c, nb, stride=CMP_PER_SEL), :].astype(BF16)


def _cmp_attention(q_rows, pos, kcp_ref, g):
    nb = kcp_ref.shape[1]
    blk = lax.broadcasted_iota(jnp.int32, (1, nb), 1)
    masks = [(SEL_BLOCK * blk + CMP_STRIDE * c + CMP_LEN - 1) <= pos for c in range(CMP_PER_SEL)]
    outs = []
    imp = None
    for qh in q_rows:
        s = [_dot_nt(qh, kcp_ref[g * CMP_PER_SEL + c]) * SCALE for c in range(CMP_PER_SEL)]
        p = _masked_softmax_parts(s, masks)
        o = functools.reduce(jnp.add, [
            _dot(p[c].astype(BF16), kcp_ref[(KV_HEADS + g) * CMP_PER_SEL + c]) for c in range(CMP_PER_SEL)])
        outs.append(o)
        ps = functools.reduce(jnp.add, p)
        imp = ps if imp is None else imp + ps
    return outs, imp


def _cmp_prompt_kernel(q_ref, y_ref, o_ref, sel_ref, kcp_ref, tmp_ref):
    i = pl.program_id(1)

    @pl.when(i == 0)
    def _():
        _combine_compressed(y_ref, kcp_ref, tmp_ref)

    nb = kcp_ref.shape[1]
    pos = i * Q_BLOCK + lax.broadcasted_iota(jnp.int32, (Q_BLOCK, 1), 0)
    blk = lax.broadcasted_iota(jnp.int32, (1, nb), 1)
    cur = pos // SEL_BLOCK
    forced = (blk == cur) | (blk == 0)
    causal = blk <= cur
    for g in range(KV_HEADS):
        q_rows = [q_ref[0, :, (g * GROUP + r) * HEAD_DIM:(g * GROUP + r + 1) * HEAD_DIM] for r in range(GROUP)]
        outs, imp = _cmp_attention(q_rows, pos, kcp_ref, g)
        for r in range(GROUP):
            hcol = (g * GROUP + r) * HEAD_DIM
            o_ref[0, :, hcol:hcol + HEAD_DIM] = outs[r]
        v = jnp.where(forced, FORCE_SCORE, jnp.where(causal, imp, -1.0))
        sel_t = _topk_select(v.T, SEL_TOPN, 0)
        sel = jnp.where(causal, sel_t.T, 0.0)
        sel_ref[0, :, g * nb:(g + 1) * nb] = sel.astype(BF16)


def _cmp_prompt(qq3, y):
    b, t, _ = qq3.shape
    ns = y.shape[1]
    nb = ns // CMP_PER_SEL
    return pl.pallas_call(
        _cmp_prompt_kernel,
        grid=(b, t // Q_BLOCK),
        in_specs=[pl.BlockSpec((1, Q_BLOCK, 1024), lambda bi, i: (bi, i, 0)),
                  pl.BlockSpec((1, ns, 1024), lambda bi, i: (bi, 0, 0))],
        out_specs=[pl.BlockSpec((1, Q_BLOCK, 1024), lambda bi, i: (bi, i, 0)),
                   pl.BlockSpec((1, Q_BLOCK, KV_HEADS * nb), lambda bi, i: (bi, i, 0))],
        out_shape=[jax.ShapeDtypeStruct((b, t, 1024), F32),
                   jax.ShapeDtypeStruct((b, t, KV_HEADS * nb), BF16)],
        scratch_shapes=[pltpu.VMEM((2 * KV_HEADS * CMP_PER_SEL, nb, HEAD_DIM), BF16),
                        pltpu.VMEM((ns, HEAD_DIM), F32)],
        compiler_params=_params("parallel", "arbitrary"),
        name="cmp_prompt",
    )(qq3, y)


SEL_CHUNK = 512


def _expand_blocks(sel, first_block, width):
    nb = sel.shape[1]
    b = lax.broadcasted_iota(jnp.int32, (nb, width), 0)
    t = lax.broadcasted_iota(jnp.int32, (nb, width), 1)
    e = jnp.where(b == first_block + t // SEL_BLOCK, 1.0, 0.0).astype(BF16)
    return _dot(sel, e)


def _sel_prompt_kernel(q_ref, sel_ref, kv_ref, o_ref):
    i = pl.program_id(1)
    nb = sel_ref.shape[2] // KV_HEADS
    pos = i * Q_BLOCK + lax.broadcasted_iota(jnp.int32, (Q_BLOCK, 1), 0)
    n_chunks = (i * Q_BLOCK + Q_BLOCK + SEL_CHUNK - 1) // SEL_CHUNK
    lane = lax.broadcasted_iota(jnp.int32, (1, SEL_CHUNK), 1)
    rows = GROUP * Q_BLOCK
    for g in range(KV_HEADS):
        q = jnp.concatenate(
            [q_ref[0, :, (g * GROUP + r) * HEAD_DIM:(g * GROUP + r + 1) * HEAD_DIM] for r in range(GROUP)], axis=0)
        sel = sel_ref[0, :, g * nb:(g + 1) * nb]

        def body(c, carry):
            m, l, acc = carry
            start = pl.multiple_of(c * SEL_CHUNK, SEL_CHUNK)
            k = kv_ref[0, pl.ds(start, SEL_CHUNK), g * HEAD_DIM:(g + 1) * HEAD_DIM]
            v = kv_ref[0, pl.ds(start, SEL_CHUNK), (KV_HEADS + g) * HEAD_DIM:(KV_HEADS + g + 1) * HEAD_DIM]
            picked = _expand_blocks(sel, c * (SEL_CHUNK // SEL_BLOCK), SEL_CHUNK)
            bias = jnp.where(picked > 0.5, jnp.where((start + lane) <= pos, 0.0, NEG_INF), NEG_INF)
            s = _dot_nt(q, k) * (SCALE * LOG2E) + jnp.concatenate([bias] * GROUP, axis=0)
            m_new = jnp.maximum(m, jnp.max(s, axis=-1, keepdims=True))
            m_safe = jnp.where(m_new == NEG_INF, 0.0, m_new)
            p = jnp.exp2(s - m_safe)
            alpha = jnp.exp2(m - m_safe)
            l = alpha * l + jnp.sum(p, axis=-1, keepdims=True)
            acc = alpha * acc + _dot(p.astype(BF16), v)
            return m_new, l, acc

        init = (jnp.full((rows, 1), NEG_INF, F32), jnp.zeros((rows, 1), F32), jnp.zeros((rows, HEAD_DIM), F32))
        m, l, acc = lax.fori_loop(0, n_chunks, body, init)
        o = acc * (1.0 / jnp.maximum(l, 1e-30))
        for r in range(GROUP):
            hcol = (g * GROUP + r) * HEAD_DIM
            o_ref[0, :, hcol:hcol + HEAD_DIM] = o[r * Q_BLOCK:(r + 1) * Q_BLOCK]


def _sel_prompt(qq3, sel, kvb3):
    b, t, _ = qq3.shape
    nb2 = sel.shape[2]
    return pl.pallas_call(
        _sel_prompt_kernel,
        grid=(b, t // Q_BLOCK),
        in_specs=[pl.BlockSpec((1, Q_BLOCK, 1024), lambda bi, i: (bi, i, 1)),
                  pl.BlockSpec((1, Q_BLOCK, nb2), lambda bi, i: (bi, i, 0)),
                  pl.BlockSpec((1, t, 512), lambda bi, i: (bi, 0, 0))],
        out_specs=pl.BlockSpec((1, Q_BLOCK, 1024), lambda bi, i: (bi, i, 0)),
        out_shape=jax.ShapeDtypeStruct((b, t, 1024), F32),
        compiler_params=_params("parallel", "arbitrary"),
        name="sel_prompt",
    )(qq3, sel, kvb3)


def _win_prompt_kernel(q_ref, kv_ref, o_ref, *, span):
    i = pl.program_id(1)
    pos = i * Q_BLOCK + lax.broadcasted_iota(jnp.int32, (Q_BLOCK, 1), 0)
    pos = jnp.concatenate([pos] * GROUP, axis=0)
    start = pl.multiple_of(jnp.maximum(i * Q_BLOCK + Q_BLOCK - span, 0), Q_BLOCK)
    kpos = start + lax.broadcasted_iota(jnp.int32, (1, span), 1)
    diff = pos - kpos
    ok = (diff >= 0) & (diff < WINDOW)
    for g in range(KV_HEADS):
        q = jnp.concatenate(
            [q_ref[0, :, (g * GROUP + r) * HEAD_DIM:(g * GROUP + r + 1) * HEAD_DIM] for r in range(GROUP)], axis=0)
        k = kv_ref[0, pl.ds(start, span), g * HEAD_DIM:(g + 1) * HEAD_DIM]
        v = kv_ref[0, pl.ds(start, span), (KV_HEADS + g) * HEAD_DIM:(KV_HEADS + g + 1) * HEAD_DIM]
        s = _dot_nt(q, k) * SCALE
        (p,) = _masked_softmax_parts([s], [ok])
        o = _dot(p.astype(BF16), v)
        for r in range(GROUP):
            hcol = (g * GROUP + r) * HEAD_DIM
            o_ref[0, :, hcol:hcol + HEAD_DIM] = o[r * Q_BLOCK:(r + 1) * Q_BLOCK]


def _win_prompt(qq3, kvb3):
    b, t, _ = qq3.shape
    span = min(WINDOW + Q_BLOCK, t)
    return pl.pallas_call(
        functools.partial(_win_prompt_kernel, span=span),
        grid=(b, t // Q_BLOCK),
        in_specs=[pl.BlockSpec((1, Q_BLOCK, 1024), lambda bi, i: (bi, i, 1)),
                  pl.BlockSpec((1, t, 512), lambda bi, i: (bi, 0, 1))],
        out_specs=pl.BlockSpec((1, Q_BLOCK, 1024), lambda bi, i: (bi, i, 0)),
        out_shape=jax.ShapeDtypeStruct((b, t, 1024), F32),
        compiler_params=_params("parallel", "arbitrary"),
        name="win_prompt",
    )(qq3, kvb3)


def _pool_kernel(prev_ref, u_ref, w_ref, sc_ref, o_ref, *, base, zero_first_prev):
    i = pl.program_id(1)
    tq = u_ref.shape[1]
    cur = u_ref[0]
    prev = prev_ref[0]
    if zero_first_prev:
        prev = jnp.where(i == 0, 0.0, prev)
    ext = jnp.concatenate([prev, cur], axis=0)
    gpos = base + i * tq + lax.broadcasted_iota(jnp.int32, (tq, 1), 0)
    gw = ext.shape[1] // len(POOL_WINDOWS)
    for gi, w in enumerate(POOL_WINDOWS):
        s = ext[:, gi * gw:(gi + 1) * gw]
        span = 1
        while span < w:
            s = s + pltpu.roll(s, span, 0)
            span *= 2
        cnt = jnp.minimum(gpos + 1, w).astype(F32)
        d = s[POOL_PREV:] / cnt - cur[:, gi * gw:(gi + 1) * gw]
        o = _dot(d.astype(BF16), w_ref[gi]) * sc_ref[:, gi * gw:(gi + 1) * gw]
        o_ref[0, :, gi * gw:(gi + 1) * gw] = o


def _pool(prev, u3, w, scale, *, base, zero_first_prev, tq=512):
    b, t, c = u3.shape
    tq = min(tq, t)
    ratio = tq // POOL_PREV
    if zero_first_prev:
        prev_map = lambda bi, i: (bi, jnp.maximum(i * ratio - 1, 0), 0)
    else:
        prev_map = lambda bi, i: (bi, 0, 0)
    ng = len(POOL_WINDOWS)
    return pl.pallas_call(
        functools.partial(_pool_kernel, base=base, zero_first_prev=zero_first_prev),
        grid=(b, t // tq),
        in_specs=[pl.BlockSpec((1, POOL_PREV, c), prev_map),
                  pl.BlockSpec((1, tq, c), lambda bi, i: (bi, i, 0)),
                  pl.BlockSpec((ng, c // ng, c // ng), lambda bi, i: (0, 0, 0)),
                  pl.BlockSpec((1, c), lambda bi, i: (0, 0))],
        out_specs=pl.BlockSpec((1, tq, c), lambda bi, i: (bi, i, 0)),
        out_shape=jax.ShapeDtypeStruct((b, t, c), F32),
        compiler_params=_params("parallel", "parallel"),
        name="pool_mix",
    )(prev, u3, w, scale)


def _mix_kernel(oc_ref, os_ref, ow_ref, gt_ref, po_ref, ga_ref, gb_ref, wa_ref, wp_ref, mix_ref):
    gt = gt_ref[...]
    cols = []
    for h in range(N_HEADS):
        sl = slice(h * HEAD_DIM, (h + 1) * HEAD_DIM)
        o = (gt[:, h:h + 1] * oc_ref[:, sl] + gt[:, N_HEADS + h:N_HEADS + h + 1] * os_ref[:, sl]
             + gt[:, 2 * N_HEADS + h:2 * N_HEADS + h + 1] * ow_ref[:, sl])
        cols.append(o.astype(BF16))
    a = _dot(jnp.concatenate(cols, axis=1), wa_ref[...])
    p = _dot(po_ref[...].astype(BF16), wp_ref[...])
    mix_ref[...] = (ga_ref[...] * a + gb_ref[...] * p).astype(BF16)


def _mix(oc, os_, ow, gt, po, gab, wa, wp, *, tm=ROW_TILE):
    m = oc.shape[0]
    d = wa.shape[1]
    tm = min(tm, m)
    row = lambda i: (i, 0)
    const = lambda i: (0, 0)
    return pl.pallas_call(
        _mix_kernel,
        grid=(m // tm,),
        in_specs=[pl.BlockSpec((tm, 1024), row), pl.BlockSpec((tm, 1024), row), pl.BlockSpec((tm, 1024), row),
                  pl.BlockSpec((tm, 128), row), pl.BlockSpec((tm, 1024), row),
                  pl.BlockSpec((tm, d), lambda i: (i, 0)), pl.BlockSpec((tm, d), lambda i: (i, 1)),
                  pl.BlockSpec((1024, d), const), pl.BlockSpec((1024, d), const)],
        out_specs=pl.BlockSpec((tm, d), row),
        out_shape=jax.ShapeDtypeStruct((m, d), BF16),
        compiler_params=_params("parallel"),
        name="branch_mix",
    )(oc, os_, ow, gt, po, gab, gab, wa, wp)


def _out_kernel(x_ref, mix_ref, w_ref, o_ref):
    o_ref[...] = x_ref[...] + _dot(mix_ref[...], w_ref[...])


def _out_proj(x, mix, w, *, tm=ROW_TILE):
    m, d = x.shape
    tm = min(tm, m)
    return pl.pallas_call(
        _out_kernel,
        grid=(m // tm,),
        in_specs=[pl.BlockSpec((tm, d), lambda i: (i, 0)), pl.BlockSpec((tm, d), lambda i: (i, 0)),
                  pl.BlockSpec((d, d), lambda i: (0, 0))],
        out_specs=pl.BlockSpec((tm, d), lambda i: (i, 0)),
        out_shape=jax.ShapeDtypeStruct((m, d), F32),
        compiler_params=_params("parallel"),
        name="out_proj",
    )(x, mix, w)


def _pair_candidates(sv, si):
    k = PEER_TOPK
    row = lax.broadcasted_iota(jnp.int32, (8, 1), 0)
    vals, ids = [], []
    for a in range(k // 2):
        lim = k // (a + 1)
        nrow = k if a == 0 else 8
        v = sv[0][a:a + 1] + sv[1][:nrow]
        if lim < nrow:
            v = jnp.where(row < lim, v, NEG_INF)
        vals.append(v)
        ids.append(si[0][a:a + 1] * float(PEER_NKEYS) + si[1][:nrow])
    vals.append(sv[0][k // 2:] + sv[1][0:1])
    ids.append(si[0][k // 2:] * float(PEER_NKEYS) + si[1][0:1])
    return jnp.concatenate(vals, axis=0), jnp.concatenate(ids, axis=0)


def _peer_score_kernel(x_ref, g_ref, wq_ref, keys_ref, ids_ref, gw_ref):
    h = _rms(x_ref[...], g_ref[...]).astype(BF16)
    q = _dot(h, wq_ref[...]).astype(BF16)
    dk = PEER_NKEYS
    ids, gws = [], []
    for hd in range(PEER_HEADS):
        sv, si = [], []
        for c in range(2):
            qhc = q[:, (hd * 2 + c) * dk:(hd * 2 + c + 1) * dk]
            st = _dot_nt(keys_ref[hd * 2 + c], qhc)
            v, ix = _topk_sorted(st, PEER_TOPK)
            sv.append(v)
            si.append(ix)
        comb, eid = _pair_candidates(sv, si)
        cv, ce = _topk_sorted(comb, PEER_TOPK, payload=eid)
        e = jnp.exp(cv - cv[0:1])
        gws.append(e / jnp.sum(e, axis=0, keepdims=True))
        ids.append(ce)
    ids_ref[...] = jnp.concatenate(ids, axis=0).T.astype(jnp.int32)
    gw_ref[...] = jnp.concatenate(gws, axis=0)


def _peer_score(x, g, wq, keys, *, tb=128):
    m, d = x.shape
    nk = PEER_HEADS * PEER_TOPK
    return pl.pallas_call(
        _peer_score_kernel,
        grid=(m // tb,),
        in_specs=[pl.BlockSpec((tb, d), lambda i: (i, 0)), pl.BlockSpec((1, d), lambda i: (0, 0)),
                  pl.BlockSpec(wq.shape, lambda i: (0, 0)),
                  pl.BlockSpec(keys.shape, lambda i: (0, 0, 0))],
        out_specs=[pl.BlockSpec((tb, nk), lambda i: (i, 0)), pl.BlockSpec((nk, tb), lambda i: (0, i))],
        out_shape=[jax.ShapeDtypeStruct((m, nk), jnp.int32), jax.ShapeDtypeStruct((nk, m), F32)],
        compiler_params=_params("parallel"),
        name="peer_score",
    )(x, g, wq, keys)


def _gelu(x):
    return 0.5 * x * (1.0 + lax.erf(x * (2.0 ** -0.5)))


PACK_ROWS = 256


def _pack_kernel(u_ref, v_ref, o_ref):
    lo = pltpu.bitcast(u_ref[...].astype(BF16).astype(F32), jnp.uint32) >> 16
    hi = pltpu.bitcast(v_ref[...].astype(BF16).astype(F32), jnp.uint32) & jnp.uint32(0xFFFF0000)
    o_ref[...] = (lo | hi).reshape(o_ref.shape)


def _pack_tables(u, v):
    e, d = u.shape
    return pl.pallas_call(
        _pack_kernel,
        grid=(e // PACK_ROWS,),
        in_specs=[pl.BlockSpec((PACK_ROWS, d), lambda i: (i, 0)), pl.BlockSpec((PACK_ROWS, d), lambda i: (i, 0))],
        out_specs=pl.BlockSpec((PACK_ROWS, 1, d), lambda i: (i, 0, 0)),
        out_shape=jax.ShapeDtypeStruct((e, 1, d), jnp.uint32),
        compiler_params=_params("parallel"),
        name="peer_pack",
    )(u, v)


PEER_RING = 8


def _peer_apply_kernel(ids_ref, nxt_ref, x_ref, gffn_ref, gw_ref, gfin_ref, tab_ref, y_ref, *scratch):
    bufs = scratch[:PEER_RING]
    sem, hbuf, obuf = scratch[PEER_RING:]
    step = pl.program_id(0)
    tb, d = x_ref.shape
    nk = gw_ref.shape[0]
    nchunk = d // 128
    ahead = PEER_RING - 1
    n_groups = tb // PEER_RING
    hbuf[...] = _rms(x_ref[...], gffn_ref[...])

    def row_copy(ids, t, k, slot):
        return pltpu.make_async_copy(tab_ref.at[ids[t, k]], bufs[slot].at[pl.ds(k, 1), :], sem.at[slot])

    def issue(ids, t, slot):
        for k in range(nk):
            row_copy(ids, t, k, slot).start(priority=k % 2)

    def drain(slot):
        for k in range(nk):
            row_copy(ids_ref, 0, k, slot).wait()

    @pl.when(step == 0)
    def _():
        for s in range(ahead):
            issue(ids_ref, s, s)

    lane = lax.broadcasted_iota(jnp.int32, (nk, tb), 1)

    def compute(t, slot):
        buf = bufs[slot]
        hrow = hbuf[pl.ds(t, 1), :]
        acc = jnp.zeros((nk, 128), F32)
        for j in range(nchunk):
            w = buf[:, j * 128:(j + 1) * 128]
            u = pltpu.bitcast(w << 16, F32)
            acc = acc + u * hrow[:, j * 128:(j + 1) * 128]
        act = jnp.sum(acc, axis=1, keepdims=True)
        gcol = jnp.sum(jnp.where(lane == t, gw_ref[...], 0.0), axis=1, keepdims=True)
        coef = _gelu(act) * gcol
        outs = []
        for j in range(nchunk):
            w = buf[:, j * 128:(j + 1) * 128]
            v = pltpu.bitcast(w & jnp.uint32(0xFFFF0000), F32)
            outs.append(jnp.sum(v * coef, axis=0, keepdims=True))
        obuf[pl.ds(t, 1), :] = jnp.concatenate(outs, axis=1)

    def group(p, last):
        for s in range(PEER_RING):
            t = p * PEER_RING + s
            drain(s)
            if last and s > 0:
                issue(nxt_ref, s - 1, (s + ahead) % PEER_RING)
            else:
                issue(ids_ref, t + ahead, (s + ahead) % PEER_RING)
            compute(t, s)

    def body(p, carry):
        group(p, False)
        return carry

    lax.fori_loop(0, n_groups - 1, body, 0)
    group(n_groups - 1, True)

    @pl.when(step == pl.num_programs(0) - 1)
    def _():
        for s in range(ahead):
            drain(s)

    y_ref[...] = _rms(x_ref[...] + obuf[...], gfin_ref[...])


def _peer_apply(ids, x, gffn, gw, gfin, table, *, tb=128):
    m, d = x.shape
    nk = ids.shape[1]
    steps = m // tb
    return pl.pallas_call(
        _peer_apply_kernel,
        grid=(steps,),
        in_specs=[pl.BlockSpec((tb, nk), lambda i: (i, 0), memory_space=pltpu.SMEM),
                  pl.BlockSpec((tb, nk), lambda i: (jnp.minimum(i + 1, steps - 1), 0), memory_space=pltpu.SMEM),
                  pl.BlockSpec((tb, d), lambda i: (i, 0)),
                  pl.BlockSpec((1, d), lambda i: (0, 0)),
                  pl.BlockSpec((nk, tb), lambda i: (0, i)),
                  pl.BlockSpec((1, d), lambda i: (0, 0)),
                  pl.BlockSpec(memory_space=pl.ANY)],
        out_specs=pl.BlockSpec((tb, d), lambda i: (i, 0)),
        out_shape=jax.ShapeDtypeStruct((m, d), F32),
        scratch_shapes=[pltpu.VMEM((nk, d), jnp.uint32)] * PEER_RING + [
            pltpu.SemaphoreType.DMA((PEER_RING,)), pltpu.VMEM((tb, d), F32), pltpu.VMEM((tb, d), F32)],
        compiler_params=_params("arbitrary"),
        name="peer_apply",
    )(ids, ids, x, gffn, gw, gfin, table)


def _stack_heads(q_ref, g):
    return jnp.concatenate(
        [q_ref[0, :, (g * GROUP + r) * HEAD_DIM:(g * GROUP + r + 1) * HEAD_DIM] for r in range(GROUP)], axis=0)


NEW_PAD = 128


def _pad_new(x):
    return jnp.concatenate([x, jnp.zeros((NEW_PAD - x.shape[0], x.shape[1]), F32)], axis=0).astype(BF16)


def _cmp_sample_kernel(q_ref, y_ref, o_ref, sel_ref, kcp_ref, tmp_ref, *, past):
    _combine_compressed(y_ref, kcp_ref, tmp_ref)
    t = q_ref.shape[1]
    nb = kcp_ref.shape[1]
    tpos = past + lax.broadcasted_iota(jnp.int32, (t, 1), 0)
    pos = jnp.concatenate([tpos] * GROUP, axis=0)
    blk = lax.broadcasted_iota(jnp.int32, (1, nb), 1)
    for g in range(KV_HEADS):
        q = _stack_heads(q_ref, g)
        outs, imp = _cmp_attention([q], pos, kcp_ref, g)
        o = outs[0]
        for r in range(GROUP):
            hcol = (g * GROUP + r) * HEAD_DIM
            o_ref[0, :, hcol:hcol + HEAD_DIM] = o[r * t:(r + 1) * t]
        imp_t = functools.reduce(jnp.add, [imp[r * t:(r + 1) * t] for r in range(GROUP)])
        v = jnp.where(blk == 0, FORCE_SCORE, imp_t)
        sel = _topk_select(v, SEL_TOPN - 1, 1)
        sel_ref[0, :, g * nb:(g + 1) * nb] = sel.astype(BF16)


def _cmp_sample(qq3, y, *, past):
    b, t, _ = qq3.shape
    ns = y.shape[1]
    nb = ns // CMP_PER_SEL
    return pl.pallas_call(
        functools.partial(_cmp_sample_kernel, past=past),
        grid=(b,),
        in_specs=[pl.BlockSpec((1, t, 1024), lambda bi: (bi, 0, 0)),
                  pl.BlockSpec((1, ns, 1024), lambda bi: (bi, 0, 0))],
        out_specs=[pl.BlockSpec((1, t, 1024), lambda bi: (bi, 0, 0)),
                   pl.BlockSpec((1, t, KV_HEADS * nb), lambda bi: (bi, 0, 0))],
        out_shape=[jax.ShapeDtypeStruct((b, t, 1024), F32),
                   jax.ShapeDtypeStruct((b, t, KV_HEADS * nb), BF16)],
        scratch_shapes=[pltpu.VMEM((2 * KV_HEADS * CMP_PER_SEL, nb, HEAD_DIM), BF16),
                        pltpu.VMEM((ns, HEAD_DIM), F32)],
        compiler_params=_params("parallel"),
        name="cmp_sample",
    )(qq3, y)


def _sel_sample_kernel(q_ref, sel_ref, kv_ref, new_ref, o_ref, m_ref, l_ref, acc_ref):
    c = pl.program_id(1)
    nc = pl.num_programs(1)
    t = q_ref.shape[1]
    nb = sel_ref.shape[2] // KV_HEADS
    width = kv_ref.shape[1]

    @pl.when(c == 0)
    def _():
        m_ref[...] = jnp.full(m_ref.shape, NEG_INF, F32)
        l_ref[...] = jnp.zeros(l_ref.shape, F32)
        acc_ref[...] = jnp.zeros(acc_ref.shape, F32)

    def update(g, s, ok, v):
        s = jnp.where(ok, s, NEG_INF)
        m = m_ref[g]
        m_new = jnp.maximum(m, jnp.max(s, axis=-1, keepdims=True))
        m_safe = jnp.where(m_new == NEG_INF, 0.0, m_new)
        p = jnp.exp(s - m_safe)
        alpha = jnp.exp(m - m_safe)
        l_ref[g] = alpha * l_ref[g] + jnp.sum(p, axis=-1, keepdims=True)
        acc_ref[g] = alpha * acc_ref[g] + _dot(p.astype(BF16), v)
        m_ref[g] = m_new

    for g in range(KV_HEADS):
        q = _stack_heads(q_ref, g)
        k = kv_ref[0, :, g * HEAD_DIM:(g + 1) * HEAD_DIM]
        v = kv_ref[0, :, (KV_HEADS + g) * HEAD_DIM:(KV_HEADS + g + 1) * HEAD_DIM]
        s = _dot_nt(q, k) * SCALE
        picked = _expand_blocks(sel_ref[0, :, g * nb:(g + 1) * nb], c * (width // SEL_BLOCK), width)
        picked = jnp.concatenate([picked] * GROUP, axis=0)
        update(g, s, picked > 0.5, v)

    @pl.when(c == nc - 1)
    def _():
        qi = lax.broadcasted_iota(jnp.int32, (t, 1), 0)
        qi = jnp.concatenate([qi] * GROUP, axis=0)
        ok = lax.broadcasted_iota(jnp.int32, (1, NEW_PAD), 1) <= qi
        for g in range(KV_HEADS):
            q = _stack_heads(q_ref, g)
            k = _pad_new(new_ref[pl.ds(2 * KV_HEADS + g, t, stride=ROWS_PER_POS), :])
            v = _pad_new(new_ref[pl.ds(3 * KV_HEADS + g, t, stride=ROWS_PER_POS), :])
            update(g, _dot_nt(q, k) * SCALE, ok, v)
            o = acc_ref[g] * (1.0 / jnp.maximum(l_ref[g], 1e-30))
            for r in range(GROUP):
                hcol = (g * GROUP + r) * HEAD_DIM
                o_ref[0, :, hcol:hcol + HEAD_DIM] = o[r * t:(r + 1) * t]


SEL_SAMPLE_CHUNK = 2048


def _sel_sample(kvs, qq3, sel, nsa_rows):
    nbatch, past, _ = kvs.shape
    t = qq3.shape[1]
    nb2 = sel.shape[2]
    rows = GROUP * t
    width = min(SEL_SAMPLE_CHUNK, past)
    return pl.pallas_call(
        _sel_sample_kernel,
        grid=(nbatch, past // width),
        in_specs=[pl.BlockSpec((1, t, 1024), lambda bi, ci: (bi, 0, 1)),
                  pl.BlockSpec((1, t, nb2), lambda bi, ci: (bi, 0, 0)),
                  pl.BlockSpec((1, width, 512), lambda bi, ci: (bi, ci, 0)),
                  pl.BlockSpec((t * ROWS_PER_POS, HEAD_DIM), lambda bi, ci: (bi, 0))],
        out_specs=pl.BlockSpec((1, t, 1024), lambda bi, ci: (bi, 0, 0)),
        out_shape=jax.ShapeDtypeStruct((nbatch, t, 1024), F32),
        scratch_shapes=[pltpu.VMEM((KV_HEADS, rows, 1), F32), pltpu.VMEM((KV_HEADS, rows, 1), F32),
                        pltpu.VMEM((KV_HEADS, rows, HEAD_DIM), F32)],
        compiler_params=_params("parallel", "arbitrary"),
        name="sel_sample",
    )(qq3, sel, kvs, nsa_rows)


def _win_sample_kernel(q_ref, st_ref, new_ref, o_ref):
    t = q_ref.shape[1]
    nw = st_ref.shape[0] // WIN_ROWS_PER_POS
    old = lambda comp: st_ref[pl.ds(comp, nw, stride=WIN_ROWS_PER_POS), :].astype(BF16)
    new = lambda comp: _pad_new(new_ref[pl.ds(comp, t, stride=WIN_ROWS_PER_POS), :])
    qi = lax.broadcasted_iota(jnp.int32, (t, 1), 0)
    qi = jnp.concatenate([qi] * GROUP, axis=0)
    d_old = (nw + qi) - lax.broadcasted_iota(jnp.int32, (1, nw), 1)
    ok_old = (d_old >= 0) & (d_old < WINDOW)
    d_new = qi - lax.broadcasted_iota(jnp.int32, (1, NEW_PAD), 1)
    ok_new = (d_new >= 0) & (d_new < WINDOW)
    for g in range(KV_HEADS):
        q = _stack_heads(q_ref, g)
        k_old, v_old = old(g), old(KV_HEADS + g)
        k_new, v_new = new(g), new(KV_HEADS + g)
        p_old, p_new = _masked_softmax_parts(
            [_dot_nt(q, k_old) * SCALE, _dot_nt(q, k_new) * SCALE], [ok_old, ok_new])
        o = _dot(p_old.astype(BF16), v_old) + _dot(p_new.astype(BF16), v_new)
        for r in range(GROUP):
            hcol = (g * GROUP + r) * HEAD_DIM
            o_ref[0, :, hcol:hcol + HEAD_DIM] = o[r * t:(r + 1) * t]


def _win_sample(qq3, state_rows, win_rows):
    b, t, _ = qq3.shape
    nw = state_rows.shape[0] // (b * WIN_ROWS_PER_POS)
    return pl.pallas_call(
        _win_sample_kernel,
        grid=(b,),
        in_specs=[pl.BlockSpec((1, t, 1024), lambda bi: (bi, 0, 1)),
                  pl.BlockSpec((nw * WIN_ROWS_PER_POS, HEAD_DIM), lambda bi: (bi, 0)),
                  pl.BlockSpec((t * WIN_ROWS_PER_POS, HEAD_DIM), lambda bi: (bi, 0))],
        out_specs=pl.BlockSpec((1, t, 1024), lambda bi: (bi, 0, 0)),
        out_shape=jax.ShapeDtypeStruct((b, t, 1024), F32),
        compiler_params=_params("parallel"),
        name="win_sample",
    )(qq3, state_rows, win_rows)


def _rope_tables(pos):
    half = HEAD_DIM // 2
    inv = ROPE_THETA ** (-jnp.arange(half, dtype=F32) / half)
    ang = pos.astype(F32)[:, None] * inv[None, :]
    cos, sin = jnp.cos(ang), jnp.sin(ang)
    return jnp.concatenate([cos, cos], axis=-1), jnp.concatenate([-sin, sin], axis=-1)


def _prep_weights(w_in, w_phi_k, w_phi_v, w_pool_group, pool_scale, w_branch_attn, w_branch_pool, w_out,
                  peer_w_query, peer_sub_keys, peer_u, peer_v):
    d = w_in.shape[0]
    qw = N_HEADS * HEAD_DIM
    kvw = 6 * KV_HEADS * HEAD_DIM
    ngw = 3 * N_HEADS
    pw = d // 2
    o1, o2, o3, o4 = qw, qw + kvw, qw + kvw + ngw, qw + kvw + ngw + pw
    wb = w_in.astype(BF16)
    w_qkv = jnp.concatenate([wb[:, :o2], wb[:, o2:o3], jnp.zeros((d, HEAD_DIM - ngw), BF16)], axis=1)
    cat = lambda w: jnp.concatenate([w[:CMP_STRIDE], w[CMP_STRIDE:]], axis=-1).astype(BF16)
    return dict(
        w_qkv=w_qkv, w_u=wb[:, o3:o4], w_gab=wb[:, o4:],
        wk=cat(w_phi_k), wv=cat(w_phi_v),
        w_pool=w_pool_group.astype(BF16), pool_scale=pool_scale.reshape(1, -1),
        w_ba=w_branch_attn.astype(BF16), w_bp=w_branch_pool.astype(BF16), w_out=w_out.astype(BF16),
        w_query=peer_w_query.astype(BF16),
        keys=peer_sub_keys.reshape(PEER_HEADS * 2, PEER_NKEYS, -1).astype(BF16),
        table=_pack_tables(peer_u, peer_v),
    )


def _token_tail(x2d, o_cmp, o_sel, o_win, gates, pool_out, gab, wp, g_ffn, g_final):
    mix = _mix(o_cmp, o_sel, o_win, gates, pool_out, gab, wp["w_ba"], wp["w_bp"])
    x2 = _out_proj(x2d, mix, wp["w_out"])
    ids, gw = _peer_score(x2, g_ffn, wp["w_query"], wp["keys"])
    return _peer_apply(ids, x2, g_ffn, gw, g_final, wp["table"])


def kernel(x_prompt, x_sample, cache_kv_nsa, state_win_kv, state_pool, page_table, g_norm_mix, w_in, w_phi_k,
           w_phi_v, w_pool_group, pool_scale, w_branch_attn, w_branch_pool, w_out, g_norm_ffn, peer_w_query,
           peer_sub_keys, peer_u, peer_v, g_norm_final):
    assert g_norm_mix.shape[0] == 1, "single-layer step"
    bp, t, d = x_prompt.shape
    bs, ts, _ = x_sample.shape
    n_pages = page_table.shape[1]
    page_size = cache_kv_nsa.shape[2]
    past = n_pages * page_size
    wp = _prep_weights(w_in[0], w_phi_k[0], w_phi_v[0], w_pool_group[0], pool_scale[0], w_branch_attn[0],
                       w_branch_pool[0], w_out[0], peer_w_query[0], peer_sub_keys[0], peer_u[0], peer_v[0])
    g_mix = g_norm_mix[0].reshape(1, d)
    g_ffn = g_norm_ffn[0].reshape(1, d)
    g_fin = g_norm_final.reshape(1, d)

    xp = x_prompt.reshape(bp * t, d)
    cos, sin = _rope_tables(jnp.arange(t))
    cos, sin = jnp.tile(cos, (bp, 1)), jnp.tile(sin, (bp, 1))
    qq, nsa, win, gates, kvb, hn = _qkv_proj(xp, g_mix, wp["w_qkv"], cos, sin)
    u = _proj(hn, wp["w_u"])
    gab = _proj(hn, wp["w_gab"], act="sigmoid", out_dtype=BF16)
    qq3, kvb3 = qq.reshape(bp, t, -1), kvb.reshape(bp, t, -1)
    y = _compress_prompt(nsa, bp, wp["wk"], wp["wv"])
    o_cmp, sel = _cmp_prompt(qq3, y)
    o_sel = _sel_prompt(qq3, sel, kvb3)
    o_win = _win_prompt(qq3, kvb3)
    u3 = u.reshape(bp, t, -1)
    pool_out = _pool(u3, u3, wp["w_pool"], wp["pool_scale"], base=0, zero_first_prev=True)
    y_prompt = _token_tail(xp, o_cmp.reshape(bp * t, -1), o_sel.reshape(bp * t, -1), o_win.reshape(bp * t, -1),
                           gates, pool_out.reshape(bp * t, -1), gab, wp, g_ffn, g_fin)
    wlen = min(WINDOW, t)
    new_kv_p = nsa.reshape(1, bp, t, 4, KV_HEADS, HEAD_DIM)
    new_win_p = win.reshape(bp, t, 2, KV_HEADS, HEAD_DIM)[None, :, t - wlen:]
    new_pool_p = u3[None, :, t - (POOL_PREV - 1):]

    xs = x_sample.reshape(bs * ts, d)
    cos_s, sin_s = _rope_tables(past + jnp.arange(ts))
    cos_s, sin_s = jnp.tile(cos_s, (bs, 1)), jnp.tile(sin_s, (bs, 1))
    qq_s, nsa_s, win_s, gates_s, _, hn_s = _qkv_proj(xs, g_mix, wp["w_qkv"], cos_s, sin_s)
    u_s = _proj(hn_s, wp["w_u"])
    gab_s = _proj(hn_s, wp["w_gab"], act="sigmoid", out_dtype=BF16)
    qq_s3 = qq_s.reshape(bs, ts, -1)
    cache_rows = cache_kv_nsa.reshape(-1, HEAD_DIM)
    y_s, kvs = _compress_pages(cache_rows, page_table, page_size, wp["wk"], wp["wv"])
    o_cmp_s, sel_s = _cmp_sample(qq_s3, y_s, past=past)
    o_sel_s = _sel_sample(kvs, qq_s3, sel_s, nsa_s)
    o_win_s = _win_sample(qq_s3, state_win_kv.reshape(-1, HEAD_DIM), win_s)
    u_s3 = u_s.reshape(bs, ts, -1)
    st_pool = state_pool[0]
    prev = jnp.pad(st_pool, ((0, 0), (POOL_PREV - st_pool.shape[1], 0), (0, 0)))
    pool_out_s = _pool(prev, u_s3, wp["w_pool"], wp["pool_scale"], base=st_pool.shape[1], zero_first_prev=False)
    y_sample = _token_tail(xs, o_cmp_s.reshape(bs * ts, -1), o_sel_s.reshape(bs * ts, -1),
                           o_win_s.reshape(bs * ts, -1), gates_s, pool_out_s.reshape(bs * ts, -1), gab_s, wp,
                           g_ffn, g_fin)
    new_kv_s = nsa_s.reshape(1, bs, ts, 4, KV_HEADS, HEAD_DIM)
    win_ext = jnp.concatenate([state_win_kv, win_s.reshape(1, bs, ts, 2, KV_HEADS, HEAD_DIM)], axis=2)
    new_win_s = win_ext[:, :, win_ext.shape[2] - min(WINDOW, win_ext.shape[2]):]
    pool_ext = jnp.concatenate([st_pool, u_s3], axis=1)
    new_pool_s = pool_ext[None, :, pool_ext.shape[1] - (POOL_PREV - 1):]

    return (y_prompt.reshape(bp, t, d), y_sample.reshape(bs, ts, d), new_kv_p, new_kv_s, new_win_p, new_win_s,
            new_pool_p, new_pool_s)
```

```python
import functools

import jax
import jax.numpy as jnp
from jax import lax
from jax.experimental import pallas as pl
from jax.experimental.pallas import tpu as pltpu

F32 = jnp.float32
BF16 = jnp.bfloat16

HEAD_DIM = 128
N_HEADS = 8
KV_HEADS = 2
GROUP = N_HEADS // KV_HEADS
CMP_LEN = 32
CMP_STRIDE = 16
SEL_BLOCK = 64
SEL_TOPN = 16
CMP_PER_SEL = SEL_BLOCK // CMP_STRIDE
WINDOW = 512
Q_BLOCK = 128
ROPE_THETA = 10000.0
FORCE_SCORE = 1e4
POOL_WINDOWS = (2, 4, 8, 16)
POOL_PREV = 16
PEER_HEADS = 8
PEER_NKEYS = 128
PEER_TOPK = 16
EPS = 1e-6
SCALE = HEAD_DIM ** -0.5
LOG2E = 1.4426950408889634
NEG_INF = float("-inf")

ROWS_PER_POS = 4 * KV_HEADS
WIN_ROWS_PER_POS = 2 * KV_HEADS

ROW_TILE = 512
VMEM_LIMIT = 56 * 1024 * 1024


def _params(*sem):
    return pltpu.CompilerParams(dimension_semantics=sem, vmem_limit_bytes=VMEM_LIMIT)


def _dot(a, b):
    return jnp.dot(a, b, preferred_element_type=F32)


def _dot_nt(a, b):
    return lax.dot_general(a, b, (((1,), (1,)), ((), ())), preferred_element_type=F32)


def _rms(x, g):
    return x * lax.rsqrt(jnp.mean(x * x, axis=-1, keepdims=True) + EPS) * g


def _masked_softmax_parts(parts, masks):
    parts = [jnp.where(mk, s, NEG_INF) for s, mk in zip(parts, masks)]
    same = all(s.shape == parts[0].shape for s in parts)

    def across(op, red, xs):
        if same:
            return red(functools.reduce(op, xs), axis=-1, keepdims=True)
        return functools.reduce(op, [red(x, axis=-1, keepdims=True) for x in xs])

    m = across(jnp.maximum, jnp.max, parts)
    m = jnp.where(m == NEG_INF, 0.0, m)
    es = [jnp.exp(s - m) for s in parts]
    den = across(jnp.add, jnp.sum, es)
    inv = 1.0 / jnp.maximum(den, 1e-30)
    return [e * inv for e in es]


def _topk_select(v, n, axis):
    size = v.shape[axis]
    idx = lax.broadcasted_iota(jnp.int32, v.shape, axis).astype(F32)
    sel = jnp.zeros(v.shape, F32)
    for _ in range(n):
        m = jnp.max(v, axis=axis, keepdims=True)
        first = jnp.min(jnp.where(v == m, idx, float(size)), axis=axis, keepdims=True)
        hit = idx == first
        v = jnp.where(hit, NEG_INF, v)
        sel = jnp.where(hit, 1.0, sel)
    return sel


def _topk_sorted(v, n, payload=None):
    size = v.shape[0]
    idx = lax.broadcasted_iota(jnp.int32, v.shape, 0).astype(F32)
    vals, picks = [], []
    for _ in range(n):
        m = jnp.max(v, axis=0, keepdims=True)
        first = jnp.min(jnp.where(v == m, idx, float(size)), axis=0, keepdims=True)
        hit = idx == first
        vals.append(m)
        if payload is None:
            picks.append(first)
        else:
            picks.append(jnp.max(jnp.where(hit, payload, -1.0), axis=0, keepdims=True))
        v = jnp.where(hit, NEG_INF, v)
    return jnp.concatenate(vals, axis=0), jnp.concatenate(picks, axis=0)


def _proj_kernel(h_ref, w_ref, o_ref, *, act, tn):
    h = h_ref[...]
    for c in range(w_ref.shape[1] // tn):
        z = _dot(h, w_ref[:, c * tn:(c + 1) * tn])
        if act == "sigmoid":
            z = jax.nn.sigmoid(z)
        o_ref[:, c * tn:(c + 1) * tn] = z.astype(o_ref.dtype)


def _proj(h, w, *, act=None, out_dtype=F32, tm=ROW_TILE, tn=512, wn=2048):
    m, d = h.shape
    n = w.shape[1]
    tm = min(tm, m)
    wn = min(wn, n)
    return pl.pallas_call(
        functools.partial(_proj_kernel, act=act, tn=tn),
        grid=(n // wn, m // tm),
        in_specs=[pl.BlockSpec((tm, d), lambda j, i: (i, 0)),
                  pl.BlockSpec((d, wn), lambda j, i: (0, j))],
        out_specs=pl.BlockSpec((tm, wn), lambda j, i: (i, j)),
        out_shape=jax.ShapeDtypeStruct((m, n), out_dtype),
        compiler_params=_params("parallel", "parallel"),
        name="mixer_proj",
    )(h, w)


QKV_COLS = N_HEADS * HEAD_DIM + 6 * KV_HEADS * HEAD_DIM + HEAD_DIM


def _qkv_kernel(x_ref, g_ref, w_ref, cos_ref, sin_ref, qq_ref, nsa_ref, win_ref, gate_ref, kvb_ref, h_ref):
    h = _rms(x_ref[...], g_ref[...]).astype(BF16)
    h_ref[...] = h
    cos = cos_ref[...]
    sin = sin_ref[...]

    def rope(z):
        return z * cos + pltpu.roll(z, HEAD_DIM // 2, 1) * sin

    qw = N_HEADS * HEAD_DIM
    for c in range(qw // 512):
        z = _dot(h, w_ref[:, c * 512:(c + 1) * 512])
        for j in range(4):
            zh = z[:, j * 128:(j + 1) * 128]
            col = c * 512 + j * 128
            qq_ref[:, col:col + 128] = zh.astype(BF16)
            qq_ref[:, qw + col:qw + col + 128] = rope(zh).astype(BF16)
    tm = x_ref.shape[0]

    def put(ref, comp, n_comp, val):
        ref[pl.ds(comp, tm, stride=n_comp), :] = val

    for c in range(3):
        z = _dot(h, w_ref[:, qw + c * 512:qw + (c + 1) * 512])
        for j in range(2):
            zk = z[:, j * 128:(j + 1) * 128]
            zv = z[:, 256 + j * 128:256 + (j + 1) * 128]
            if c == 0:
                put(nsa_ref, j, ROWS_PER_POS, zk)
                put(nsa_ref, KV_HEADS + j, ROWS_PER_POS, zv)
                continue
            zr = rope(zk)
            if c == 1:
                put(nsa_ref, 2 * KV_HEADS + j, ROWS_PER_POS, zr)
                put(nsa_ref, 3 * KV_HEADS + j, ROWS_PER_POS, zv)
            else:
                put(win_ref, j, WIN_ROWS_PER_POS, zr)
                put(win_ref, KV_HEADS + j, WIN_ROWS_PER_POS, zv)
            base = (c - 1) * 512
            kvb_ref[:, base + j * 128:base + (j + 1) * 128] = zr.astype(BF16)
            kvb_ref[:, base + 256 + j * 128:base + 256 + (j + 1) * 128] = zv.astype(BF16)
    z = _dot(h, w_ref[:, qw + 1536:qw + 1536 + 128])
    gate_ref[...] = jax.nn.sigmoid(z)


def _qkv_proj(x, g, w, cos, sin, *, tm=ROW_TILE):
    m, d = x.shape
    tm = min(tm, m)
    row = lambda i: (i, 0)
    const = lambda i: (0, 0)
    return pl.pallas_call(
        _qkv_kernel,
        grid=(m // tm,),
        in_specs=[pl.BlockSpec((tm, d), row), pl.BlockSpec((1, d), const),
                  pl.BlockSpec((d, QKV_COLS), const),
                  pl.BlockSpec((tm, HEAD_DIM), row), pl.BlockSpec((tm, HEAD_DIM), row)],
        out_specs=[pl.BlockSpec((tm, 2048), row), pl.BlockSpec((tm * ROWS_PER_POS, HEAD_DIM), row),
                   pl.BlockSpec((tm * WIN_ROWS_PER_POS, HEAD_DIM), row), pl.BlockSpec((tm, 128), row),
                   pl.BlockSpec((tm, 1024), row), pl.BlockSpec((tm, d), row)],
        out_shape=[jax.ShapeDtypeStruct((m, 2048), BF16), jax.ShapeDtypeStruct((m * ROWS_PER_POS, HEAD_DIM), F32),
                   jax.ShapeDtypeStruct((m * WIN_ROWS_PER_POS, HEAD_DIM), F32), jax.ShapeDtypeStruct((m, 128), F32),
                   jax.ShapeDtypeStruct((m, 1024), BF16), jax.ShapeDtypeStruct((m, d), BF16)],
        compiler_params=_params("parallel"),
        name="qkv_proj",
    )(x, g, w, cos, sin)


SUB_ROWS = CMP_STRIDE * ROWS_PER_POS


def _compress_rows(load, n_rows, wk_ref, wv_ref, y_ref):
    for kv in range(2):
        w_ref = wk_ref if kv == 0 else wv_ref
        for g in range(KV_HEADS):
            comp = kv * KV_HEADS + g
            acc = jnp.zeros((n_rows, 2 * HEAD_DIM), F32)
            for i in range(CMP_STRIDE):
                acc = acc + _dot(load(i, comp).astype(BF16), w_ref[i])
            y_ref[0, :, 2 * comp * HEAD_DIM:2 * (comp + 1) * HEAD_DIM] = acc


def _compress_prompt_kernel(x_ref, wk_ref, wv_ref, y_ref):
    ts = x_ref.shape[0] // SUB_ROWS
    load = lambda i, comp: x_ref[pl.ds(i * ROWS_PER_POS + comp, ts, stride=SUB_ROWS), :]
    _compress_rows(load, ts, wk_ref, wv_ref, y_ref)


def _compress_prompt(nsa_rows, b, wk, wv, *, ts=128):
    n_sub = nsa_rows.shape[0] // (b * SUB_ROWS)
    ts = min(ts, n_sub)
    steps = n_sub // ts
    wspec = pl.BlockSpec((CMP_STRIDE, HEAD_DIM, 2 * HEAD_DIM), lambda bi, ci: (0, 0, 0))
    return pl.pallas_call(
        _compress_prompt_kernel,
        grid=(b, steps),
        in_specs=[pl.BlockSpec((ts * SUB_ROWS, HEAD_DIM), lambda bi, ci: (bi * steps + ci, 0)), wspec, wspec],
        out_specs=pl.BlockSpec((1, ts, 1024), lambda bi, ci: (bi, ci, 0)),
        out_shape=jax.ShapeDtypeStruct((b, n_sub, 1024), F32),
        compiler_params=_params("parallel", "parallel"),
        name="compress_prompt",
    )(nsa_rows, wk, wv)


PAGES_PER_STEP = 16


def _compress_pages_kernel(pt_ref, *refs):
    pages = refs[:PAGES_PER_STEP]
    wk_ref, wv_ref, y_ref, kvs_ref = refs[PAGES_PER_STEP:]
    per_page = pages[0].shape[0] // SUB_ROWS
    page_size = pages[0].shape[0] // ROWS_PER_POS

    def load(i, comp):
        rows = pl.ds(i * ROWS_PER_POS + comp, per_page, stride=SUB_ROWS)
        return jnp.concatenate([p[rows, :] for p in pages], axis=0)

    _compress_rows(load, PAGES_PER_STEP * per_page, wk_ref, wv_ref, y_ref)
    for j in range(2 * KV_HEADS):
        rows = pl.ds(2 * KV_HEADS + j, page_size, stride=ROWS_PER_POS)
        kvs_ref[0, :, j * HEAD_DIM:(j + 1) * HEAD_DIM] = jnp.concatenate(
            [p[rows, :] for p in pages], axis=0).astype(BF16)


def _page_specs(page_rows):
    def spec(k):
        return pl.BlockSpec((page_rows, HEAD_DIM), lambda bi, ci, pt: (pt[bi, ci * PAGES_PER_STEP + k], 0))
    return [spec(k) for k in range(PAGES_PER_STEP)]


def _compress_pages(cache_rows, page_table, page_size, wk, wv):
    nb, n_pages = page_table.shape
    per_page = page_size // CMP_STRIDE
    rows = PAGES_PER_STEP * per_page
    const3 = lambda bi, ci, pt: (0, 0, 0)
    grid_spec = pltpu.PrefetchScalarGridSpec(
        num_scalar_prefetch=1,
        grid=(nb, n_pages // PAGES_PER_STEP),
        in_specs=_page_specs(page_size * ROWS_PER_POS) + [
            pl.BlockSpec((CMP_STRIDE, HEAD_DIM, 2 * HEAD_DIM), const3),
            pl.BlockSpec((CMP_STRIDE, HEAD_DIM, 2 * HEAD_DIM), const3)],
        out_specs=[pl.BlockSpec((1, rows, 1024), lambda bi, ci, pt: (bi, ci, 0)),
                   pl.BlockSpec((1, PAGES_PER_STEP * page_size, 512), lambda bi, ci, pt: (bi, ci, 0))],
    )
    return pl.pallas_call(
        _compress_pages_kernel,
        grid_spec=grid_spec,
        out_shape=[jax.ShapeDtypeStruct((nb, n_pages * per_page, 1024), F32),
                   jax.ShapeDtypeStruct((nb, n_pages * page_size, 512), BF16)],
        compiler_params=_params("parallel", "parallel"),
        name="compress_pages",
    )(page_table, *([cache_rows] * PAGES_PER_STEP), wk, wv)


def _combine_compressed(y_ref, kcp_ref, tmp_ref):
    ns = y_ref.shape[1]
    nb = ns // CMP_PER_SEL
    last = lax.broadcasted_iota(jnp.int32, (ns, 1), 0) == ns - 1
    for a in range(2 * KV_HEADS):
        y1 = y_ref[0, :, a * 256:a * 256 + 128]
        y2 = y_ref[0, :, a * 256 + 128:(a + 1) * 256]
        nxt = jnp.where(last, 0.0, pltpu.roll(y2, ns - 1, 0))
        tmp_ref[...] = y1 + nxt
        for c in range(CMP_PER_SEL):
            kcp_ref[a * CMP_PER_SEL + c] = tmp_ref[pl.ds(c, nb, stride=CMP_PER_SEL), :].astype(BF16)


def _cmp_attention(q_rows, pos, kcp_ref, g):
    nb = kcp_ref.shape[1]
    blk = lax.broadcasted_iota(jnp.int32, (1, nb), 1)
    masks = [(SEL_BLOCK * blk + CMP_STRIDE * c + CMP_LEN - 1) <= pos for c in range(CMP_PER_SEL)]
    outs = []
    imp = None
    for qh in q_rows:
        s = [_dot_nt(qh, kcp_ref[g * CMP_PER_SEL + c]) * SCALE for c in range(CMP_PER_SEL)]
        p = _masked_softmax_parts(s, masks)
        o = functools.reduce(jnp.add, [
            _dot(p[c].astype(BF16), kcp_ref[(KV_HEADS + g) * CMP_PER_SEL + c]) for c in range(CMP_PER_SEL)])
        outs.append(o)
        ps = functools.reduce(jnp.add, p)
        imp = ps if imp is None else imp + ps
    return outs, imp


def _cmp_prompt_kernel(q_ref, y_ref, o_ref, sel_ref, kcp_ref, tmp_ref):
    i = pl.program_id(1)

    @pl.when(i == 0)
    def _():
        _combine_compressed(y_ref, kcp_ref, tmp_ref)

    nb = kcp_ref.shape[1]
    pos = i * Q_BLOCK + lax.broadcasted_iota(jnp.int32, (Q_BLOCK, 1), 0)
    blk = lax.broadcasted_iota(jnp.int32, (1, nb), 1)
    cur = pos // SEL_BLOCK
    forced = (blk == cur) | (blk == 0)
    causal = blk <= cur
    for g in range(KV_HEADS):
        q_rows = [q_ref[0, :, (g * GROUP + r) * HEAD_DIM:(g * GROUP + r + 1) * HEAD_DIM] for r in range(GROUP)]
        outs, imp = _cmp_attention(q_rows, pos, kcp_ref, g)
        for r in range(GROUP):
            hcol = (g * GROUP + r) * HEAD_DIM
            o_ref[0, :, hcol:hcol + HEAD_DIM] = outs[r]
        v = jnp.where(forced, FORCE_SCORE, jnp.where(causal, imp, -1.0))
        sel_t = _topk_select(v.T, SEL_TOPN, 0)
        sel = jnp.where(causal, sel_t.T, 0.0)
        sel_ref[0, :, g * nb:(g + 1) * nb] = sel.astype(BF16)


def _cmp_prompt(qq3, y):
    b, t, _ = qq3.shape
    ns = y.shape[1]
    nb = ns // CMP_PER_SEL
    return pl.pallas_call(
        _cmp_prompt_kernel,
        grid=(b, t // Q_BLOCK),
        in_specs=[pl.BlockSpec((1, Q_BLOCK, 1024), lambda bi, i: (bi, i, 0)),
                  pl.BlockSpec((1, ns, 1024), lambda bi, i: (bi, 0, 0))],
        out_specs=[pl.BlockSpec((1, Q_BLOCK, 1024), lambda bi, i: (bi, i, 0)),
                   pl.BlockSpec((1, Q_BLOCK, KV_HEADS * nb), lambda bi, i: (bi, i, 0))],
        out_shape=[jax.ShapeDtypeStruct((b, t, 1024), F32),
                   jax.ShapeDtypeStruct((b, t, KV_HEADS * nb), BF16)],
        scratch_shapes=[pltpu.VMEM((2 * KV_HEADS * CMP_PER_SEL, nb, HEAD_DIM), BF16),
                        pltpu.VMEM((ns, HEAD_DIM), F32)],
        compiler_params=_params("parallel", "arbitrary"),
        name="cmp_prompt",
    )(qq3, y)


SEL_CHUNK = 512


def _expand_blocks(sel, first_block, width):
    nb = sel.shape[1]
    b = lax.broadcasted_iota(jnp.int32, (nb, width), 0)
    t = lax.broadcasted_iota(jnp.int32, (nb, width), 1)
    e = jnp.where(b == first_block + t // SEL_BLOCK, 1.0, 0.0).astype(BF16)
    return _dot(sel, e)


def _sel_prompt_kernel(q_ref, sel_ref, kv_ref, o_ref):
    i = pl.program_id(1)
    nb = sel_ref.shape[2] // KV_HEADS
    pos = i * Q_BLOCK + lax.broadcasted_iota(jnp.int32, (Q_BLOCK, 1), 0)
    n_chunks = (i * Q_BLOCK + Q_BLOCK + SEL_CHUNK - 1) // SEL_CHUNK
    lane = lax.broadcasted_iota(jnp.int32, (1, SEL_CHUNK), 1)
    rows = GROUP * Q_BLOCK
    for g in range(KV_HEADS):
        q = jnp.concatenate(
            [q_ref[0, :, (g * GROUP + r) * HEAD_DIM:(g * GROUP + r + 1) * HEAD_DIM] for r in range(GROUP)], axis=0)
        sel = sel_ref[0, :, g * nb:(g + 1) * nb]

        def chunk(c, carry):
            m, l, acc = carry
            start = pl.multiple_of(c * SEL_CHUNK, SEL_CHUNK)
            k = kv_ref[0, pl.ds(start, SEL_CHUNK), g * HEAD_DIM:(g + 1) * HEAD_DIM]
            v = kv_ref[0, pl.ds(start, SEL_CHUNK), (KV_HEADS + g) * HEAD_DIM:(KV_HEADS + g + 1) * HEAD_DIM]
            picked = _expand_blocks(sel, c * (SEL_CHUNK // SEL_BLOCK), SEL_CHUNK)
            bias = jnp.where(picked > 0.5, jnp.where((start + lane) <= pos, 0.0, NEG_INF), NEG_INF)
            s = _dot_nt(q, k) * (SCALE * LOG2E) + jnp.concatenate([bias] * GROUP, axis=0)
            m_new = jnp.maximum(m, jnp.max(s, axis=-1, keepdims=True))
            m_safe = jnp.where(m_new == NEG_INF, 0.0, m_new)
            p = jnp.exp2(s - m_safe)
            alpha = jnp.exp2(m - m_safe)
            l = alpha * l + jnp.sum(p, axis=-1, keepdims=True)
            acc = alpha * acc + _dot(p.astype(BF16), v)
            return m_new, l, acc

        def pair(cc, carry):
            return chunk(2 * cc + 1, chunk(2 * cc, carry))

        init = (jnp.full((rows, 1), NEG_INF, F32), jnp.zeros((rows, 1), F32), jnp.zeros((rows, HEAD_DIM), F32))
        state = lax.fori_loop(0, n_chunks // 2, pair, init)
        m, l, acc = lax.cond(n_chunks % 2 == 1, lambda st: chunk(n_chunks - 1, st), lambda st: st, state)
        o = acc * (1.0 / jnp.maximum(l, 1e-30))
        for r in range(GROUP):
            hcol = (g * GROUP + r) * HEAD_DIM
            o_ref[0, :, hcol:hcol + HEAD_DIM] = o[r * Q_BLOCK:(r + 1) * Q_BLOCK]


def _sel_prompt(qq3, sel, kvb3):
    b, t, _ = qq3.shape
    nb2 = sel.shape[2]
    return pl.pallas_call(
        _sel_prompt_kernel,
        grid=(b, t // Q_BLOCK),
        in_specs=[pl.BlockSpec((1, Q_BLOCK, 1024), lambda bi, i: (bi, i, 1)),
                  pl.BlockSpec((1, Q_BLOCK, nb2), lambda bi, i: (bi, i, 0)),
                  pl.BlockSpec((1, t, 512), lambda bi, i: (bi, 0, 0))],
        out_specs=pl.BlockSpec((1, Q_BLOCK, 1024), lambda bi, i: (bi, i, 0)),
        out_shape=jax.ShapeDtypeStruct((b, t, 1024), F32),
        compiler_params=_params("parallel", "arbitrary"),
        name="sel_prompt",
    )(qq3, sel, kvb3)


def _win_prompt_kernel(q_ref, kv_ref, o_ref, *, span):
    i = pl.program_id(1)
    pos = i * Q_BLOCK + lax.broadcasted_iota(jnp.int32, (Q_BLOCK, 1), 0)
    pos = jnp.concatenate([pos] * GROUP, axis=0)
    start = pl.multiple_of(jnp.maximum(i * Q_BLOCK + Q_BLOCK - span, 0), Q_BLOCK)
    kpos = start + lax.broadcasted_iota(jnp.int32, (1, span), 1)
    diff = pos - kpos
    ok = (diff >= 0) & (diff < WINDOW)
    for g in range(KV_HEADS):
        q = jnp.concatenate(
            [q_ref[0, :, (g * GROUP + r) * HEAD_DIM:(g * GROUP + r + 1) * HEAD_DIM] for r in range(GROUP)], axis=0)
        k = kv_ref[0, pl.ds(start, span), g * HEAD_DIM:(g + 1) * HEAD_DIM]
        v = kv_ref[0, pl.ds(start, span), (KV_HEADS + g) * HEAD_DIM:(KV_HEADS + g + 1) * HEAD_DIM]
        s = _dot_nt(q, k) * SCALE
        (p,) = _masked_softmax_parts([s], [ok])
        o = _dot(p.astype(BF16), v)
        for r in range(GROUP):
            hcol = (g * GROUP + r) * HEAD_DIM
            o_ref[0, :, hcol:hcol + HEAD_DIM] = o[r * Q_BLOCK:(r + 1) * Q_BLOCK]


def _win_prompt(qq3, kvb3):
    b, t, _ = qq3.shape
    span = min(WINDOW + Q_BLOCK, t)
    return pl.pallas_call(
        functools.partial(_win_prompt_kernel, span=span),
        grid=(b, t // Q_BLOCK),
        in_specs=[pl.BlockSpec((1, Q_BLOCK, 1024), lambda bi, i: (bi, i, 1)),
                  pl.BlockSpec((1, t, 512), lambda bi, i: (bi, 0, 1))],
        out_specs=pl.BlockSpec((1, Q_BLOCK, 1024), lambda bi, i: (bi, i, 0)),
        out_shape=jax.ShapeDtypeStruct((b, t, 1024), F32),
        compiler_params=_params("parallel", "arbitrary"),
        name="win_prompt",
    )(qq3, kvb3)


def _pool_kernel(prev_ref, u_ref, w_ref, sc_ref, o_ref, *, base, zero_first_prev):
    i = pl.program_id(1)
    tq = u_ref.shape[1]
    cur = u_ref[0]
    prev = prev_ref[0]
    if zero_first_prev:
        prev = jnp.where(i == 0, 0.0, prev)
    ext = jnp.concatenate([prev, cur], axis=0)
    gpos = base + i * tq + lax.broadcasted_iota(jnp.int32, (tq, 1), 0)
    gw = ext.shape[1] // len(POOL_WINDOWS)
    for gi, w in enumerate(POOL_WINDOWS):
        s = ext[:, gi * gw:(gi + 1) * gw]
        span = 1
        while span < w:
            s = s + pltpu.roll(s, span, 0)
            span *= 2
        cnt = jnp.minimum(gpos + 1, w).astype(F32)
        d = s[POOL_PREV:] / cnt - cur[:, gi * gw:(gi + 1) * gw]
        o = _dot(d.astype(BF16), w_ref[gi]) * sc_ref[:, gi * gw:(gi + 1) * gw]
        o_ref[0, :, gi * gw:(gi + 1) * gw] = o


def _pool(prev, u3, w, scale, *, base, zero_first_prev, tq=512):
    b, t, c = u3.shape
    tq = min(tq, t)
    ratio = tq // POOL_PREV
    if zero_first_prev:
        prev_map = lambda bi, i: (bi, jnp.maximum(i * ratio - 1, 0), 0)
    else:
        prev_map = lambda bi, i: (bi, 0, 0)
    ng = len(POOL_WINDOWS)
    return pl.pallas_call(
        functools.partial(_pool_kernel, base=base, zero_first_prev=zero_first_prev),
        grid=(b, t // tq),
        in_specs=[pl.BlockSpec((1, POOL_PREV, c), prev_map),
                  pl.BlockSpec((1, tq, c), lambda bi, i: (bi, i, 0)),
                  pl.BlockSpec((ng, c // ng, c // ng), lambda bi, i: (0, 0, 0)),
                  pl.BlockSpec((1, c), lambda bi, i: (0, 0))],
        out_specs=pl.BlockSpec((1, tq, c), lambda bi, i: (bi, i, 0)),
        out_shape=jax.ShapeDtypeStruct((b, t, c), F32),
        compiler_params=_params("parallel", "parallel"),
        name="pool_mix",
    )(prev, u3, w, scale)


def _mix_kernel(oc_ref, os_ref, ow_ref, gt_ref, po_ref, ga_ref, gb_ref, wa_ref, wp_ref, mix_ref):
    gt = gt_ref[...]
    cols = []
    for h in range(N_HEADS):
        sl = slice(h * HEAD_DIM, (h + 1) * HEAD_DIM)
        o = (gt[:, h:h + 1] * oc_ref[:, sl] + gt[:, N_HEADS + h:N_HEADS + h + 1] * os_ref[:, sl]
             + gt[:, 2 * N_HEADS + h:2 * N_HEADS + h + 1] * ow_ref[:, sl])
        cols.append(o.astype(BF16))
    a = _dot(jnp.concatenate(cols, axis=1), wa_ref[...])
    p = _dot(po_ref[...].astype(BF16), wp_ref[...])
    mix_ref[...] = (ga_ref[...] * a + gb_ref[...] * p).astype(BF16)


def _mix(oc, os_, ow, gt, po, gab, wa, wp, *, tm=ROW_TILE):
    m = oc.shape[0]
    d = wa.shape[1]
    tm = min(tm, m)
    row = lambda i: (i, 0)
    const = lambda i: (0, 0)
    return pl.pallas_call(
        _mix_kernel,
        grid=(m // tm,),
        in_specs=[pl.BlockSpec((tm, 1024), row), pl.BlockSpec((tm, 1024), row), pl.BlockSpec((tm, 1024), row),
                  pl.BlockSpec((tm, 128), row), pl.BlockSpec((tm, 1024), row),
                  pl.BlockSpec((tm, d), lambda i: (i, 0)), pl.BlockSpec((tm, d), lambda i: (i, 1)),
                  pl.BlockSpec((1024, d), const), pl.BlockSpec((1024, d), const)],
        out_specs=pl.BlockSpec((tm, d), row),
        out_shape=jax.ShapeDtypeStruct((m, d), BF16),
        compiler_params=_params("parallel"),
        name="branch_mix",
    )(oc, os_, ow, gt, po, gab, gab, wa, wp)


def _out_kernel(x_ref, mix_ref, w_ref, o_ref):
    o_ref[...] = x_ref[...] + _dot(mix_ref[...], w_ref[...])


def _out_proj(x, mix, w, *, tm=ROW_TILE):
    m, d = x.shape
    tm = min(tm, m)
    return pl.pallas_call(
        _out_kernel,
        grid=(m // tm,),
        in_specs=[pl.BlockSpec((tm, d), lambda i: (i, 0)), pl.BlockSpec((tm, d), lambda i: (i, 0)),
                  pl.BlockSpec((d, d), lambda i: (0, 0))],
        out_specs=pl.BlockSpec((tm, d), lambda i: (i, 0)),
        out_shape=jax.ShapeDtypeStruct((m, d), F32),
        compiler_params=_params("parallel"),
        name="out_proj",
    )(x, mix, w)


def _pair_candidates(sv, si):
    k = PEER_TOPK
    row = lax.broadcasted_iota(jnp.int32, (8, 1), 0)
    vals, ids = [], []
    for a in range(k // 2):
        lim = k // (a + 1)
        nrow = k if a == 0 else 8
        v = sv[0][a:a + 1] + sv[1][:nrow]
        if lim < nrow:
            v = jnp.where(row < lim, v, NEG_INF)
        vals.append(v)
        ids.append(si[0][a:a + 1] * float(PEER_NKEYS) + si[1][:nrow])
    vals.append(sv[0][k // 2:] + sv[1][0:1])
    ids.append(si[0][k // 2:] * float(PEER_NKEYS) + si[1][0:1])
    return jnp.concatenate(vals, axis=0), jnp.concatenate(ids, axis=0)


def _peer_score_kernel(x_ref, g_ref, wq_ref, keys_ref, ids_ref, gw_ref):
    h = _rms(x_ref[...], g_ref[...]).astype(BF16)
    q = _dot(h, wq_ref[...]).astype(BF16)
    dk = PEER_NKEYS
    ids, gws = [], []
    for hd in range(PEER_HEADS):
        sv, si = [], []
        for c in range(2):
            qhc = q[:, (hd * 2 + c) * dk:(hd * 2 + c + 1) * dk]
            st = _dot_nt(keys_ref[hd * 2 + c], qhc)
            v, ix = _topk_sorted(st, PEER_TOPK)
            sv.append(v)
            si.append(ix)
        comb, eid = _pair_candidates(sv, si)
        cv, ce = _topk_sorted(comb, PEER_TOPK, payload=eid)
        e = jnp.exp(cv - cv[0:1])
        gws.append(e / jnp.sum(e, axis=0, keepdims=True))
        ids.append(ce)
    ids_ref[...] = jnp.concatenate(ids, axis=0).T.astype(jnp.int32)
    gw_ref[...] = jnp.concatenate(gws, axis=0)


def _peer_score(x, g, wq, keys, *, tb=128):
    m, d = x.shape
    nk = PEER_HEADS * PEER_TOPK
    return pl.pallas_call(
        _peer_score_kernel,
        grid=(m // tb,),
        in_specs=[pl.BlockSpec((tb, d), lambda i: (i, 0)), pl.BlockSpec((1, d), lambda i: (0, 0)),
                  pl.BlockSpec(wq.shape, lambda i: (0, 0)),
                  pl.BlockSpec(keys.shape, lambda i: (0, 0, 0))],
        out_specs=[pl.BlockSpec((tb, nk), lambda i: (i, 0)), pl.BlockSpec((nk, tb), lambda i: (0, i))],
        out_shape=[jax.ShapeDtypeStruct((m, nk), jnp.int32), jax.ShapeDtypeStruct((nk, m), F32)],
        compiler_params=_params("parallel"),
        name="peer_score",
    )(x, g, wq, keys)


def _gelu(x):
    return 0.5 * x * (1.0 + lax.erf(x * (2.0 ** -0.5)))


PACK_ROWS = 256


def _pack_kernel(u_ref, v_ref, o_ref):
    lo = pltpu.bitcast(u_ref[...].astype(BF16).astype(F32), jnp.uint32) >> 16
    hi = pltpu.bitcast(v_ref[...].astype(BF16).astype(F32), jnp.uint32) & jnp.uint32(0xFFFF0000)
    o_ref[...] = (lo | hi).reshape(o_ref.shape)


def _pack_tables(u, v):
    e, d = u.shape
    return pl.pallas_call(
        _pack_kernel,
        grid=(e // PACK_ROWS,),
        in_specs=[pl.BlockSpec((PACK_ROWS, d), lambda i: (i, 0)), pl.BlockSpec((PACK_ROWS, d), lambda i: (i, 0))],
        out_specs=pl.BlockSpec((PACK_ROWS, 1, d), lambda i: (i, 0, 0)),
        out_shape=jax.ShapeDtypeStruct((e, 1, d), jnp.uint32),
        compiler_params=_params("parallel"),
        name="peer_pack",
    )(u, v)


PEER_RING = 8


def _peer_apply_kernel(ids_ref, nxt_ref, x_ref, gffn_ref, gw_ref, gfin_ref, tab_ref, y_ref, *scratch):
    bufs = scratch[:PEER_RING]
    sem, hbuf, obuf = scratch[PEER_RING:]
    step = pl.program_id(0)
    tb, d = x_ref.shape
    nk = gw_ref.shape[0]
    nchunk = d // 128
    ahead = PEER_RING - 1
    n_groups = tb // PEER_RING
    hbuf[...] = _rms(x_ref[...], gffn_ref[...])

    def row_copy(ids, t, k, slot):
        return pltpu.make_async_copy(tab_ref.at[ids[t, k]], bufs[slot].at[pl.ds(k, 1), :], sem.at[slot])

    def issue(ids, t, slot):
        for k in range(nk):
            row_copy(ids, t, k, slot).start(priority=k % 2)

    def drain(slot):
        for k in range(nk):
            row_copy(ids_ref, 0, k, slot).wait()

    @pl.when(step == 0)
    def _():
        for s in range(ahead):
            issue(ids_ref, s, s)

    lane = lax.broadcasted_iota(jnp.int32, (nk, tb), 1)

    def compute(t, slot):
        buf = bufs[slot]
        hrow = hbuf[pl.ds(t, 1), :]
        acc = jnp.zeros((nk, 128), F32)
        for j in range(nchunk):
            w = buf[:, j * 128:(j + 1) * 128]
            u = pltpu.bitcast(w << 16, F32)
            acc = acc + u * hrow[:, j * 128:(j + 1) * 128]
        act = jnp.sum(acc, axis=1, keepdims=True)
        gcol = jnp.sum(jnp.where(lane == t, gw_ref[...], 0.0), axis=1, keepdims=True)
        coef = _gelu(act) * gcol
        outs = []
        for j in range(nchunk):
            w = buf[:, j * 128:(j + 1) * 128]
            v = pltpu.bitcast(w & jnp.uint32(0xFFFF0000), F32)
            outs.append(jnp.sum(v * coef, axis=0, keepdims=True))
        obuf[pl.ds(t, 1), :] = jnp.concatenate(outs, axis=1)

    def group(p, last):
        for s in range(PEER_RING):
            t = p * PEER_RING + s
            drain(s)
            if last and s > 0:
                issue(nxt_ref, s - 1, (s + ahead) % PEER_RING)
            else:
                issue(ids_ref, t + ahead, (s + ahead) % PEER_RING)
            compute(t, s)

    def body(p, carry):
        group(p, False)
        return carry

    lax.fori_loop(0, n_groups - 1, body, 0)
    group(n_groups - 1, True)

    @pl.when(step == pl.num_programs(0) - 1)
    def _():
        for s in range(ahead):
            drain(s)

    y_ref[...] = _rms(x_ref[...] + obuf[...], gfin_ref[...])


def _peer_apply(ids, x, gffn, gw, gfin, table, *, tb=128):
    m, d = x.shape
    nk = ids.shape[1]
    steps = m // tb
    return pl.pallas_call(
        _peer_apply_kernel,
        grid=(steps,),
        in_specs=[pl.BlockSpec((tb, nk), lambda i: (i, 0), memory_space=pltpu.SMEM),
                  pl.BlockSpec((tb, nk), lambda i: (jnp.minimum(i + 1, steps - 1), 0), memory_space=pltpu.SMEM),
                  pl.BlockSpec((tb, d), lambda i: (i, 0)),
                  pl.BlockSpec((1, d), lambda i: (0, 0)),
                  pl.BlockSpec((nk, tb), lambda i: (0, i)),
                  pl.BlockSpec((1, d), lambda i: (0, 0)),
                  pl.BlockSpec(memory_space=pl.ANY)],
        out_specs=pl.BlockSpec((tb, d), lambda i: (i, 0)),
        out_shape=jax.ShapeDtypeStruct((m, d), F32),
        scratch_shapes=[pltpu.VMEM((nk, d), jnp.uint32)] * PEER_RING + [
            pltpu.SemaphoreType.DMA((PEER_RING,)), pltpu.VMEM((tb, d), F32), pltpu.VMEM((tb, d), F32)],
        compiler_params=_params("arbitrary"),
        name="peer_apply",
    )(ids, ids, x, gffn, gw, gfin, table)


def _stack_heads(q_ref, g):
    return jnp.concatenate(
        [q_ref[0, :, (g * GROUP + r) * HEAD_DIM:(g * GROUP + r + 1) * HEAD_DIM] for r in range(GROUP)], axis=0)


NEW_PAD = 128


def _pad_new(x):
    return jnp.concatenate([x, jnp.zeros((NEW_PAD - x.shape[0], x.shape[1]), F32)], axis=0).astype(BF16)


def _cmp_sample_kernel(q_ref, y_ref, o_ref, sel_ref, kcp_ref, tmp_ref, *, past):
    _combine_compressed(y_ref, kcp_ref, tmp_ref)
    t = q_ref.shape[1]
    nb = kcp_ref.shape[1]
    tpos = past + lax.broadcasted_iota(jnp.int32, (t, 1), 0)
    pos = jnp.concatenate([tpos] * GROUP, axis=0)
    blk = lax.broadcasted_iota(jnp.int32, (1, nb), 1)
    for g in range(KV_HEADS):
        q = _stack_heads(q_ref, g)
        outs, imp = _cmp_attention([q], pos, kcp_ref, g)
        o = outs[0]
        for r in range(GROUP):
            hcol = (g * GROUP + r) * HEAD_DIM
            o_ref[0, :, hcol:hcol + HEAD_DIM] = o[r * t:(r + 1) * t]
        imp_t = functools.reduce(jnp.add, [imp[r * t:(r + 1) * t] for r in range(GROUP)])
        v = jnp.where(blk == 0, FORCE_SCORE, imp_t)
        sel = _topk_select(v, SEL_TOPN - 1, 1)
        sel_ref[0, :, g * nb:(g + 1) * nb] = sel.astype(BF16)


def _cmp_sample(qq3, y, *, past):
    b, t, _ = qq3.shape
    ns = y.shape[1]
    nb = ns // CMP_PER_SEL
    return pl.pallas_call(
        functools.partial(_cmp_sample_kernel, past=past),
        grid=(b,),
        in_specs=[pl.BlockSpec((1, t, 1024), lambda bi: (bi, 0, 0)),
                  pl.BlockSpec((1, ns, 1024), lambda bi: (bi, 0, 0))],
        out_specs=[pl.BlockSpec((1, t, 1024), lambda bi: (bi, 0, 0)),
                   pl.BlockSpec((1, t, KV_HEADS * nb), lambda bi: (bi, 0, 0))],
        out_shape=[jax.ShapeDtypeStruct((b, t, 1024), F32),
                   jax.ShapeDtypeStruct((b, t, KV_HEADS * nb), BF16)],
        scratch_shapes=[pltpu.VMEM((2 * KV_HEADS * CMP_PER_SEL, nb, HEAD_DIM), BF16),
                        pltpu.VMEM((ns, HEAD_DIM), F32)],
        compiler_params=_params("parallel"),
        name="cmp_sample",
    )(qq3, y)


def _sel_sample_kernel(q_ref, sel_ref, kv_ref, new_ref, o_ref, m_ref, l_ref, acc_ref):
    c = pl.program_id(1)
    nc = pl.num_programs(1)
    t = q_ref.shape[1]
    nb = sel_ref.shape[2] // KV_HEADS
    width = kv_ref.shape[1]

    @pl.when(c == 0)
    def _():
        m_ref[...] = jnp.full(m_ref.shape, NEG_INF, F32)
        l_ref[...] = jnp.zeros(l_ref.shape, F32)
        acc_ref[...] = jnp.zeros(acc_ref.shape, F32)

    def update(g, s, ok, v):
        s = jnp.where(ok, s, NEG_INF)
        m = m_ref[g]
        m_new = jnp.maximum(m, jnp.max(s, axis=-1, keepdims=True))
        m_safe = jnp.where(m_new == NEG_INF, 0.0, m_new)
        p = jnp.exp(s - m_safe)
        alpha = jnp.exp(m - m_safe)
        l_ref[g] = alpha * l_ref[g] + jnp.sum(p, axis=-1, keepdims=True)
        acc_ref[g] = alpha * acc_ref[g] + _dot(p.astype(BF16), v)
        m_ref[g] = m_new

    for g in range(KV_HEADS):
        q = _stack_heads(q_ref, g)
        k = kv_ref[0, :, g * HEAD_DIM:(g + 1) * HEAD_DIM]
        v = kv_ref[0, :, (KV_HEADS + g) * HEAD_DIM:(KV_HEADS + g + 1) * HEAD_DIM]
        s = _dot_nt(q, k) * SCALE
        picked = _expand_blocks(sel_ref[0, :, g * nb:(g + 1) * nb], c * (width // SEL_BLOCK), width)
        picked = jnp.concatenate([picked] * GROUP, axis=0)
        update(g, s, picked > 0.5, v)

    @pl.when(c == nc - 1)
    def _():
        qi = lax.broadcasted_iota(jnp.int32, (t, 1), 0)
        qi = jnp.concatenate([qi] * GROUP, axis=0)
        ok = lax.broadcasted_iota(jnp.int32, (1, NEW_PAD), 1) <= qi
        for g in range(KV_HEADS):
            q = _stack_heads(q_ref, g)
            k = _pad_new(new_ref[pl.ds(2 * KV_HEADS + g, t, stride=ROWS_PER_POS), :])
            v = _pad_new(new_ref[pl.ds(3 * KV_HEADS + g, t, stride=ROWS_PER_POS), :])
            update(g, _dot_nt(q, k) * SCALE, ok, v)
            o = acc_ref[g] * (1.0 / jnp.maximum(l_ref[g], 1e-30))
            for r in range(GROUP):
                hcol = (g * GROUP + r) * HEAD_DIM
                o_ref[0, :, hcol:hcol + HEAD_DIM] = o[r * t:(r + 1) * t]


SEL_SAMPLE_CHUNK = 2048


def _sel_sample(kvs, qq3, sel, nsa_rows):
    nbatch, past, _ = kvs.shape
    t = qq3.shape[1]
    nb2 = sel.shape[2]
    rows = GROUP * t
    width = min(SEL_SAMPLE_CHUNK, past)
    return pl.pallas_call(
        _sel_sample_kernel,
        grid=(nbatch, past // width),
        in_specs=[pl.BlockSpec((1, t, 1024), lambda bi, ci: (bi, 0, 1)),
                  pl.BlockSpec((1, t, nb2), lambda bi, ci: (bi, 0, 0)),
                  pl.BlockSpec((1, width, 512), lambda bi, ci: (bi, ci, 0)),
                  pl.BlockSpec((t * ROWS_PER_POS, HEAD_DIM), lambda bi, ci: (bi, 0))],
        out_specs=pl.BlockSpec((1, t, 1024), lambda bi, ci: (bi, 0, 0)),
        out_shape=jax.ShapeDtypeStruct((nbatch, t, 1024), F32),
        scratch_shapes=[pltpu.VMEM((KV_HEADS, rows, 1), F32), pltpu.VMEM((KV_HEADS, rows, 1), F32),
                        pltpu.VMEM((KV_HEADS, rows, HEAD_DIM), F32)],
        compiler_params=_params("parallel", "arbitrary"),
        name="sel_sample",
    )(qq3, sel, kvs, nsa_rows)


def _win_sample_kernel(q_ref, st_ref, new_ref, o_ref):
    t = q_ref.shape[1]
    nw = st_ref.shape[0] // WIN_ROWS_PER_POS
    old = lambda comp: st_ref[pl.ds(comp, nw, stride=WIN_ROWS_PER_POS), :].astype(BF16)
    new = lambda comp: _pad_new(new_ref[pl.ds(comp, t, stride=WIN_ROWS_PER_POS), :])
    qi = lax.broadcasted_iota(jnp.int32, (t, 1), 0)
    qi = jnp.concatenate([qi] * GROUP, axis=0)
    d_old = (nw + qi) - lax.broadcasted_iota(jnp.int32, (1, nw), 1)
    ok_old = (d_old >= 0) & (d_old < WINDOW)
    d_new = qi - lax.broadcasted_iota(jnp.int32, (1, NEW_PAD), 1)
    ok_new = (d_new >= 0) & (d_new < WINDOW)
    for g in range(KV_HEADS):
        q = _stack_heads(q_ref, g)
        k_old, v_old = old(g), old(KV_HEADS + g)
        k_new, v_new = new(g), new(KV_HEADS + g)
        p_old, p_new = _masked_softmax_parts(
            [_dot_nt(q, k_old) * SCALE, _dot_nt(q, k_new) * SCALE], [ok_old, ok_new])
        o = _dot(p_old.astype(BF16), v_old) + _dot(p_new.astype(BF16), v_new)
        for r in range(GROUP):
            hcol = (g * GROUP + r) * HEAD_DIM
            o_ref[0, :, hcol:hcol + HEAD_DIM] = o[r * t:(r + 1) * t]


def _win_sample(qq3, state_rows, win_rows):
    b, t, _ = qq3.shape
    nw = state_rows.shape[0] // (b * WIN_ROWS_PER_POS)
    return pl.pallas_call(
        _win_sample_kernel,
        grid=(b,),
        in_specs=[pl.BlockSpec((1, t, 1024), lambda bi: (bi, 0, 1)),
                  pl.BlockSpec((nw * WIN_ROWS_PER_POS, HEAD_DIM), lambda bi: (bi, 0)),
                  pl.BlockSpec((t * WIN_ROWS_PER_POS, HEAD_DIM), lambda bi: (bi, 0))],
        out_specs=pl.BlockSpec((1, t, 1024), lambda bi: (bi, 0, 0)),
        out_shape=jax.ShapeDtypeStruct((b, t, 1024), F32),
        compiler_params=_params("parallel"),
        name="win_sample",
    )(qq3, state_rows, win_rows)


def _rope_tables(pos):
    half = HEAD_DIM // 2
    inv = ROPE_THETA ** (-jnp.arange(half, dtype=F32) / half)
    ang = pos.astype(F32)[:, None] * inv[None, :]
    cos, sin = jnp.cos(ang), jnp.sin(ang)
    return jnp.concatenate([cos, cos], axis=-1), jnp.concatenate([-sin, sin], axis=-1)


def _prep_weights(w_in, w_phi_k, w_phi_v, w_pool_group, pool_scale, w_branch_attn, w_branch_pool, w_out,
                  peer_w_query, peer_sub_keys, peer_u, peer_v):
    d = w_in.shape[0]
    qw = N_HEADS * HEAD_DIM
    kvw = 6 * KV_HEADS * HEAD_DIM
    ngw = 3 * N_HEADS
    pw = d // 2
    o1, o2, o3, o4 = qw, qw + kvw, qw + kvw + ngw, qw + kvw + ngw + pw
    wb = w_in.astype(BF16)
    w_qkv = jnp.concatenate([wb[:, :o2], wb[:, o2:o3], jnp.zeros((d, HEAD_DIM - ngw), BF16)], axis=1)
    cat = lambda w: jnp.concatenate([w[:CMP_STRIDE], w[CMP_STRIDE:]], axis=-1).astype(BF16)
    return dict(
        w_qkv=w_qkv, w_u=wb[:, o3:o4], w_gab=wb[:, o4:],
        wk=cat(w_phi_k), wv=cat(w_phi_v),
        w_pool=w_pool_group.astype(BF16), pool_scale=pool_scale.reshape(1, -1),
        w_ba=w_branch_attn.astype(BF16), w_bp=w_branch_pool.astype(BF16), w_out=w_out.astype(BF16),
        w_query=peer_w_query.astype(BF16),
        keys=peer_sub_keys.reshape(PEER_HEADS * 2, PEER_NKEYS, -1).astype(BF16),
        table=_pack_tables(peer_u, peer_v),
    )


def _token_tail(x2d, o_cmp, o_sel, o_win, gates, pool_out, gab, wp, g_ffn, g_final):
    mix = _mix(o_cmp, o_sel, o_win, gates, pool_out, gab, wp["w_ba"], wp["w_bp"])
    x2 = _out_proj(x2d, mix, wp["w_out"])
    ids, gw = _peer_score(x2, g_ffn, wp["w_query"], wp["keys"])
    return _peer_apply(ids, x2, g_ffn, gw, g_final, wp["table"])


def kernel(x_prompt, x_sample, cache_kv_nsa, state_win_kv, state_pool, page_table, g_norm_mix, w_in, w_phi_k,
           w_phi_v, w_pool_group, pool_scale, w_branch_attn, w_branch_pool, w_out, g_norm_ffn, peer_w_query,
           peer_sub_keys, peer_u, peer_v, g_norm_final):
    assert g_norm_mix.shape[0] == 1, "single-layer step"
    bp, t, d = x_prompt.shape
    bs, ts, _ = x_sample.shape
    n_pages = page_table.shape[1]
    page_size = cache_kv_nsa.shape[2]
    past = n_pages * page_size
    wp = _prep_weights(w_in[0], w_phi_k[0], w_phi_v[0], w_pool_group[0], pool_scale[0], w_branch_attn[0],
                       w_branch_pool[0], w_out[0], peer_w_query[0], peer_sub_keys[0], peer_u[0], peer_v[0])
    g_mix = g_norm_mix[0].reshape(1, d)
    g_ffn = g_norm_ffn[0].reshape(1, d)
    g_fin = g_norm_final.reshape(1, d)

    xp = x_prompt.reshape(bp * t, d)
    cos, sin = _rope_tables(jnp.arange(t))
    cos, sin = jnp.tile(cos, (bp, 1)), jnp.tile(sin, (bp, 1))
    qq, nsa, win, gates, kvb, hn = _qkv_proj(xp, g_mix, wp["w_qkv"], cos, sin)
    u = _proj(hn, wp["w_u"])
    gab = _proj(hn, wp["w_gab"], act="sigmoid", out_dtype=BF16)
    qq3, kvb3 = qq.reshape(bp, t, -1), kvb.reshape(bp, t, -1)
    y = _compress_prompt(nsa, bp, wp["wk"], wp["wv"])
    o_cmp, sel = _cmp_prompt(qq3, y)
    o_sel = _sel_prompt(qq3, sel, kvb3)
    o_win = _win_prompt(qq3, kvb3)
    u3 = u.reshape(bp, t, -1)
    pool_out = _pool(u3, u3, wp["w_pool"], wp["pool_scale"], base=0, zero_first_prev=True)
    y_prompt = _token_tail(xp, o_cmp.reshape(bp * t, -1), o_sel.reshape(bp * t, -1), o_win.reshape(bp * t, -1),
                           gates, pool_out.reshape(bp * t, -1), gab, wp, g_ffn, g_fin)
    wlen = min(WINDOW, t)
    new_kv_p = nsa.reshape(1, bp, t, 4, KV_HEADS, HEAD_DIM)
    new_win_p = win.reshape(bp, t, 2, KV_HEADS, HEAD_DIM)[None, :, t - wlen:]
    new_pool_p = u3[None, :, t - (POOL_PREV - 1):]

    xs = x_sample.reshape(bs * ts, d)
    cos_s, sin_s = _rope_tables(past + jnp.arange(ts))
    cos_s, sin_s = jnp.tile(cos_s, (bs, 1)), jnp.tile(sin_s, (bs, 1))
    qq_s, nsa_s, win_s, gates_s, _, hn_s = _qkv_proj(xs, g_mix, wp["w_qkv"], cos_s, sin_s)
    u_s = _proj(hn_s, wp["w_u"])
    gab_s = _proj(hn_s, wp["w_gab"], act="sigmoid", out_dtype=BF16)
    qq_s3 = qq_s.reshape(bs, ts, -1)
    cache_rows = cache_kv_nsa.reshape(-1, HEAD_DIM)
    y_s, kvs = _compress_pages(cache_rows, page_table, page_size, wp["wk"], wp["wv"])
    o_cmp_s, sel_s = _cmp_sample(qq_s3, y_s, past=past)
    o_sel_s = _sel_sample(kvs, qq_s3, sel_s, nsa_s)
    o_win_s = _win_sample(qq_s3, state_win_kv.reshape(-1, HEAD_DIM), win_s)
    u_s3 = u_s.reshape(bs, ts, -1)
    st_pool = state_pool[0]
    prev = jnp.pad(st_pool, ((0, 0), (POOL_PREV - st_pool.shape[1], 0), (0, 0)))
    pool_out_s = _pool(prev, u_s3, wp["w_pool"], wp["pool_scale"], base=st_pool.shape[1], zero_first_prev=False)
    y_sample = _token_tail(xs, o_cmp_s.reshape(bs * ts, -1), o_sel_s.reshape(bs * ts, -1),
                           o_win_s.reshape(bs * ts, -1), gates_s, pool_out_s.reshape(bs * ts, -1), gab_s, wp,
                           g_ffn, g_fin)
    new_kv_s = nsa_s.reshape(1, bs, ts, 4, KV_HEADS, HEAD_DIM)
    win_ext = jnp.concatenate([state_win_kv, win_s.reshape(1, bs, ts, 2, KV_HEADS, HEAD_DIM)], axis=2)
    new_win_s = win_ext[:, :, win_ext.shape[2] - min(WINDOW, win_ext.shape[2]):]
    pool_ext = jnp.concatenate([st_pool, u_s3], axis=1)
    new_pool_s = pool_ext[None, :, pool_ext.shape[1] - (POOL_PREV - 1):]

    return (y_prompt.reshape(bp, t, d), y_sample.reshape(bs, ts, d), new_kv_p, new_kv_s, new_win_p, new_win_s,
            new_pool_p, new_pool_s)
```

```python
import functools

import jax
import jax.numpy as jnp
from jax import lax
from jax.experimental import pallas as pl
from jax.experimental.pallas import tpu as pltpu

F32 = jnp.float32
BF16 = jnp.bfloat16

HEAD_DIM = 128
N_HEADS = 8
KV_HEADS = 2
GROUP = N_HEADS // KV_HEADS
CMP_LEN = 32
CMP_STRIDE = 16
SEL_BLOCK = 64
SEL_TOPN = 16
CMP_PER_SEL = SEL_BLOCK // CMP_STRIDE
WINDOW = 512
Q_BLOCK = 128
ROPE_THETA = 10000.0
FORCE_SCORE = 1e4
POOL_WINDOWS = (2, 4, 8, 16)
POOL_PREV = 16
PEER_HEADS = 8
PEER_NKEYS = 128
PEER_TOPK = 16
EPS = 1e-6
SCALE = HEAD_DIM ** -0.5
LOG2E = 1.4426950408889634
NEG_INF = float("-inf")

ROWS_PER_POS = 4 * KV_HEADS
WIN_ROWS_PER_POS = 2 * KV_HEADS

ROW_TILE = 512
VMEM_LIMIT = 56 * 1024 * 1024


def _params(*sem):
    return pltpu.CompilerParams(dimension_semantics=sem, vmem_limit_bytes=VMEM_LIMIT)


def _dot(a, b):
    return jnp.dot(a, b, preferred_element_type=F32)


def _dot_nt(a, b):
    return lax.dot_general(a, b, (((1,), (1,)), ((), ())), preferred_element_type=F32)


def _rms(x, g):
    return x * lax.rsqrt(jnp.mean(x * x, axis=-1, keepdims=True) + EPS) * g


def _masked_softmax_parts(parts, masks):
    parts = [jnp.where(mk, s, NEG_INF) for s, mk in zip(parts, masks)]
    same = all(s.shape == parts[0].shape for s in parts)

    def across(op, red, xs):
        if same:
            return red(functools.reduce(op, xs), axis=-1, keepdims=True)
        return functools.reduce(op, [red(x, axis=-1, keepdims=True) for x in xs])

    m = across(jnp.maximum, jnp.max, parts)
    m = jnp.where(m == NEG_INF, 0.0, m)
    es = [jnp.exp(s - m) for s in parts]
    den = across(jnp.add, jnp.sum, es)
    inv = 1.0 / jnp.maximum(den, 1e-30)
    return [e * inv for e in es]


def _topk_select(v, n, axis):
    size = v.shape[axis]
    idx = lax.broadcasted_iota(jnp.int32, v.shape, axis).astype(F32)
    sel = jnp.zeros(v.shape, F32)
    for _ in range(n):
        m = jnp.max(v, axis=axis, keepdims=True)
        first = jnp.min(jnp.where(v == m, idx, float(size)), axis=axis, keepdims=True)
        hit = idx == first
        v = jnp.where(hit, NEG_INF, v)
        sel = jnp.where(hit, 1.0, sel)
    return sel


def _topk_sorted(v, n, payload=None):
    size = v.shape[0]
    idx = lax.broadcasted_iota(jnp.int32, v.shape, 0).astype(F32)
    vals, picks = [], []
    for _ in range(n):
        m = jnp.max(v, axis=0, keepdims=True)
        first = jnp.min(jnp.where(v == m, idx, float(size)), axis=0, keepdims=True)
        hit = idx == first
        vals.append(m)
        if payload is None:
            picks.append(first)
        else:
            picks.append(jnp.max(jnp.where(hit, payload, -1.0), axis=0, keepdims=True))
        v = jnp.where(hit, NEG_INF, v)
    return jnp.concatenate(vals, axis=0), jnp.concatenate(picks, axis=0)


def _proj_kernel(h_ref, w_ref, o_ref, *, act, tn):
    h = h_ref[...]
    for c in range(w_ref.shape[1] // tn):
        z = _dot(h, w_ref[:, c * tn:(c + 1) * tn])
        if act == "sigmoid":
            z = jax.nn.sigmoid(z)
        o_ref[:, c * tn:(c + 1) * tn] = z.astype(o_ref.dtype)


def _proj(h, w, *, act=None, out_dtype=F32, tm=ROW_TILE, tn=512, wn=2048):
    m, d = h.shape
    n = w.shape[1]
    tm = min(tm, m)
    wn = min(wn, n)
    return pl.pallas_call(
        functools.partial(_proj_kernel, act=act, tn=tn),
        grid=(n // wn, m // tm),
        in_specs=[pl.BlockSpec((tm, d), lambda j, i: (i, 0)),
                  pl.BlockSpec((d, wn), lambda j, i: (0, j))],
        out_specs=pl.BlockSpec((tm, wn), lambda j, i: (i, j)),
        out_shape=jax.ShapeDtypeStruct((m, n), out_dtype),
        compiler_params=_params("parallel", "parallel"),
        name="mixer_proj",
    )(h, w)


QKV_COLS = N_HEADS * HEAD_DIM + 6 * KV_HEADS * HEAD_DIM + HEAD_DIM


def _qkv_kernel(x_ref, g_ref, w_ref, cos_ref, sin_ref, qq_ref, nsa_ref, win_ref, gate_ref, kvb_ref, h_ref):
    h = _rms(x_ref[...], g_ref[...]).astype(BF16)
    h_ref[...] = h
    cos = cos_ref[...]
    sin = sin_ref[...]

    def rope(z):
        return z * cos + pltpu.roll(z, HEAD_DIM // 2, 1) * sin

    qw = N_HEADS * HEAD_DIM
    for c in range(qw // 512):
        z = _dot(h, w_ref[:, c * 512:(c + 1) * 512])
        for j in range(4):
            zh = z[:, j * 128:(j + 1) * 128]
            col = c * 512 + j * 128
            qq_ref[:, col:col + 128] = zh.astype(BF16)
            qq_ref[:, qw + col:qw + col + 128] = rope(zh).astype(BF16)
    tm = x_ref.shape[0]

    def put(ref, comp, n_comp, val):
        ref[pl.ds(comp, tm, stride=n_comp), :] = val

    for c in range(3):
        z = _dot(h, w_ref[:, qw + c * 512:qw + (c + 1) * 512])
        for j in range(2):
            zk = z[:, j * 128:(j + 1) * 128]
            zv = z[:, 256 + j * 128:256 + (j + 1) * 128]
            if c == 0:
                put(nsa_ref, j, ROWS_PER_POS, zk)
                put(nsa_ref, KV_HEADS + j, ROWS_PER_POS, zv)
                continue
            zr = rope(zk)
            if c == 1:
                put(nsa_ref, 2 * KV_HEADS + j, ROWS_PER_POS, zr)
                put(nsa_ref, 3 * KV_HEADS + j, ROWS_PER_POS, zv)
            else:
                put(win_ref, j, WIN_ROWS_PER_POS, zr)
                put(win_ref, KV_HEADS + j, WIN_ROWS_PER_POS, zv)
            base = (c - 1) * 512
            kvb_ref[:, base + j * 128:base + (j + 1) * 128] = zr.astype(BF16)
            kvb_ref[:, base + 256 + j * 128:base + 256 + (j + 1) * 128] = zv.astype(BF16)
    z = _dot(h, w_ref[:, qw + 1536:qw + 1536 + 128])
    gate_ref[...] = jax.nn.sigmoid(z)


def _qkv_proj(x, g, w, cos, sin, *, tm=ROW_TILE):
    m, d = x.shape
    tm = min(tm, m)
    row = lambda i: (i, 0)
    const = lambda i: (0, 0)
    return pl.pallas_call(
        _qkv_kernel,
        grid=(m // tm,),
        in_specs=[pl.BlockSpec((tm, d), row), pl.BlockSpec((1, d), const),
                  pl.BlockSpec((d, QKV_COLS), const),
                  pl.BlockSpec((tm, HEAD_DIM), row), pl.BlockSpec((tm, HEAD_DIM), row)],
        out_specs=[pl.BlockSpec((tm, 2048), row), pl.BlockSpec((tm * ROWS_PER_POS, HEAD_DIM), row),
                   pl.BlockSpec((tm * WIN_ROWS_PER_POS, HEAD_DIM), row), pl.BlockSpec((tm, 128), row),
                   pl.BlockSpec((tm, 1024), row), pl.BlockSpec((tm, d), row)],
        out_shape=[jax.ShapeDtypeStruct((m, 2048), BF16), jax.ShapeDtypeStruct((m * ROWS_PER_POS, HEAD_DIM), F32),
                   jax.ShapeDtypeStruct((m * WIN_ROWS_PER_POS, HEAD_DIM), F32), jax.ShapeDtypeStruct((m, 128), F32),
                   jax.ShapeDtypeStruct((m, 1024), BF16), jax.ShapeDtypeStruct((m, d), BF16)],
        compiler_params=_params("parallel"),
        name="qkv_proj",
    )(x, g, w, cos, sin)


SUB_ROWS = CMP_STRIDE * ROWS_PER_POS


def _compress_rows(load, n_rows, wk_ref, wv_ref, y_ref):
    for kv in range(2):
        w_ref = wk_ref if kv == 0 else wv_ref
        for g in range(KV_HEADS):
            comp = kv * KV_HEADS + g
            acc = jnp.zeros((n_rows, 2 * HEAD_DIM), F32)
            for i in range(CMP_STRIDE):
                acc = acc + _dot(load(i, comp).astype(BF16), w_ref[i])
            y_ref[0, :, 2 * comp * HEAD_DIM:2 * (comp + 1) * HEAD_DIM] = acc


def _compress_prompt_kernel(x_ref, wk_ref, wv_ref, y_ref):
    ts = x_ref.shape[0] // SUB_ROWS
    load = lambda i, comp: x_ref[pl.ds(i * ROWS_PER_POS + comp, ts, stride=SUB_ROWS), :]
    _compress_rows(load, ts, wk_ref, wv_ref, y_ref)


def _compress_prompt(nsa_rows, b, wk, wv, *, ts=128):
    n_sub = nsa_rows.shape[0] // (b * SUB_ROWS)
    ts = min(ts, n_sub)
    steps = n_sub // ts
    wspec = pl.BlockSpec((CMP_STRIDE, HEAD_DIM, 2 * HEAD_DIM), lambda bi, ci: (0, 0, 0))
    return pl.pallas_call(
        _compress_prompt_kernel,
        grid=(b, steps),
        in_specs=[pl.BlockSpec((ts * SUB_ROWS, HEAD_DIM), lambda bi, ci: (bi * steps + ci, 0)), wspec, wspec],
        out_specs=pl.BlockSpec((1, ts, 1024), lambda bi, ci: (bi, ci, 0)),
        out_shape=jax.ShapeDtypeStruct((b, n_sub, 1024), F32),
        compiler_params=_params("parallel", "parallel"),
        name="compress_prompt",
    )(nsa_rows, wk, wv)


PAGES_PER_STEP = 16


def _compress_pages_kernel(pt_ref, *refs):
    pages = refs[:PAGES_PER_STEP]
    wk_ref, wv_ref, y_ref, kvs_ref = refs[PAGES_PER_STEP:]
    per_page = pages[0].shape[0] // SUB_ROWS
    page_size = pages[0].shape[0] // ROWS_PER_POS

    def load(i, comp):
        rows = pl.ds(i * ROWS_PER_POS + comp, per_page, stride=SUB_ROWS)
        return jnp.concatenate([p[rows, :] for p in pages], axis=0)

    _compress_rows(load, PAGES_PER_STEP * per_page, wk_ref, wv_ref, y_ref)
    for j in range(2 * KV_HEADS):
        rows = pl.ds(2 * KV_HEADS + j, page_size, stride=ROWS_PER_POS)
        kvs_ref[0, :, j * HEAD_DIM:(j + 1) * HEAD_DIM] = jnp.concatenate(
            [p[rows, :] for p in pages], axis=0).astype(BF16)


def _page_specs(page_rows):
    def spec(k):
        return pl.BlockSpec((page_rows, HEAD_DIM), lambda bi, ci, pt: (pt[bi, ci * PAGES_PER_STEP + k], 0))
    return [spec(k) for k in range(PAGES_PER_STEP)]


def _compress_pages(cache_rows, page_table, page_size, wk, wv):
    nb, n_pages = page_table.shape
    per_page = page_size // CMP_STRIDE
    rows = PAGES_PER_STEP * per_page
    const3 = lambda bi, ci, pt: (0, 0, 0)
    grid_spec = pltpu.PrefetchScalarGridSpec(
        num_scalar_prefetch=1,
        grid=(nb, n_pages // PAGES_PER_STEP),
        in_specs=_page_specs(page_size * ROWS_PER_POS) + [
            pl.BlockSpec((CMP_STRIDE, HEAD_DIM, 2 * HEAD_DIM), const3),
            pl.BlockSpec((CMP_STRIDE, HEAD_DIM, 2 * HEAD_DIM), const3)],
        out_specs=[pl.BlockSpec((1, rows, 1024), lambda bi, ci, pt: (bi, ci, 0)),
                   pl.BlockSpec((1, PAGES_PER_STEP * page_size, 512), lambda bi, ci, pt: (bi, ci, 0))],
    )
    return pl.pallas_call(
        _compress_pages_kernel,
        grid_spec=grid_spec,
        out_shape=[jax.ShapeDtypeStruct((nb, n_pages * per_page, 1024), F32),
                   jax.ShapeDtypeStruct((nb, n_pages * page_size, 512), BF16)],
        compiler_params=_params("parallel", "parallel"),
        name="compress_pages",
    )(page_table, *([cache_rows] * PAGES_PER_STEP), wk, wv)


def _combine_compressed(y_ref, kcp_ref, tmp_ref):
    ns = y_ref.shape[1]
    nb = ns // CMP_PER_SEL
    last = lax.broadcasted_iota(jnp.int32, (ns, 1), 0) == ns - 1
    for a in range(2 * KV_HEADS):
        y1 = y_ref[0, :, a * 256:a * 256 + 128]
        y2 = y_ref[0, :, a * 256 + 128:(a + 1) * 256]
        nxt = jnp.where(last, 0.0, pltpu.roll(y2, ns - 1, 0))
        tmp_ref[...] = y1 + nxt
        for c in range(CMP_PER_SEL):
            kcp_ref[a * CMP_PER_SEL + c] = tmp_ref[pl.ds(c, nb, stride=CMP_PER_SEL), :].astype(BF16)


def _cmp_attention(q_rows, pos, kcp_ref, g):
    nb = kcp_ref.shape[1]
    blk = lax.broadcasted_iota(jnp.int32, (1, nb), 1)
    masks = [(SEL_BLOCK * blk + CMP_STRIDE * c + CMP_LEN - 1) <= pos for c in range(CMP_PER_SEL)]
    outs = []
    imp = None
    for qh in q_rows:
        s = [_dot_nt(qh, kcp_ref[g * CMP_PER_SEL + c]) * SCALE for c in range(CMP_PER_SEL)]
        p = _masked_softmax_parts(s, masks)
        o = functools.reduce(jnp.add, [
            _dot(p[c].astype(BF16), kcp_ref[(KV_HEADS + g) * CMP_PER_SEL + c]) for c in range(CMP_PER_SEL)])
        outs.append(o)
        ps = functools.reduce(jnp.add, p)
        imp = ps if imp is None else imp + ps
    return outs, imp


def _cmp_prompt_kernel(q_ref, y_ref, o_ref, sel_ref, kcp_ref, tmp_ref):
    i = pl.program_id(1)

    @pl.when(i == 0)
    def _():
        _combine_compressed(y_ref, kcp_ref, tmp_ref)

    nb = kcp_ref.shape[1]
    pos = i * Q_BLOCK + lax.broadcasted_iota(jnp.int32, (Q_BLOCK, 1), 0)
    blk = lax.broadcasted_iota(jnp.int32, (1, nb), 1)
    cur = pos // SEL_BLOCK
    forced = (blk == cur) | (blk == 0)
    causal = blk <= cur
    pos_rows = jnp.concatenate([pos] * GROUP, axis=0)
    for g in range(KV_HEADS):
        (o,), imp_rows = _cmp_attention([_stack_heads(q_ref, g)], pos_rows, kcp_ref, g)
        for r in range(GROUP):
            hcol = (g * GROUP + r) * HEAD_DIM
            o_ref[0, :, hcol:hcol + HEAD_DIM] = o[r * Q_BLOCK:(r + 1) * Q_BLOCK]
        imp = functools.reduce(jnp.add, [imp_rows[r * Q_BLOCK:(r + 1) * Q_BLOCK] for r in range(GROUP)])
        v = jnp.where(forced, FORCE_SCORE, jnp.where(causal, imp, -1.0))
        sel_t = _topk_select(v.T, SEL_TOPN, 0)
        sel = jnp.where(causal, sel_t.T, 0.0)
        sel_ref[0, :, g * nb:(g + 1) * nb] = sel.astype(BF16)


def _cmp_prompt(qq3, y):
    b, t, _ = qq3.shape
    ns = y.shape[1]
    nb = ns // CMP_PER_SEL
    return pl.pallas_call(
        _cmp_prompt_kernel,
        grid=(b, t // Q_BLOCK),
        in_specs=[pl.BlockSpec((1, Q_BLOCK, 1024), lambda bi, i: (bi, i, 0)),
                  pl.BlockSpec((1, ns, 1024), lambda bi, i: (bi, 0, 0))],
        out_specs=[pl.BlockSpec((1, Q_BLOCK, 1024), lambda bi, i: (bi, i, 0)),
                   pl.BlockSpec((1, Q_BLOCK, KV_HEADS * nb), lambda bi, i: (bi, i, 0))],
        out_shape=[jax.ShapeDtypeStruct((b, t, 1024), F32),
                   jax.ShapeDtypeStruct((b, t, KV_HEADS * nb), BF16)],
        scratch_shapes=[pltpu.VMEM((2 * KV_HEADS * CMP_PER_SEL, nb, HEAD_DIM), BF16),
                        pltpu.VMEM((ns, HEAD_DIM), F32)],
        compiler_params=_params("parallel", "arbitrary"),
        name="cmp_prompt",
    )(qq3, y)


SEL_CHUNK = 512


def _expand_blocks(sel, first_block, width):
    nb = sel.shape[1]
    b = lax.broadcasted_iota(jnp.int32, (nb, width), 0)
    t = lax.broadcasted_iota(jnp.int32, (nb, width), 1)
    e = jnp.where(b == first_block + t // SEL_BLOCK, 1.0, 0.0).astype(BF16)
    return _dot(sel, e)


def _sel_prompt_kernel(q_ref, sel_ref, kv_ref, o_ref):
    i = pl.program_id(1)
    nb = sel_ref.shape[2] // KV_HEADS
    pos = i * Q_BLOCK + lax.broadcasted_iota(jnp.int32, (Q_BLOCK, 1), 0)
    n_chunks = (i * Q_BLOCK + Q_BLOCK + SEL_CHUNK - 1) // SEL_CHUNK
    lane = lax.broadcasted_iota(jnp.int32, (1, SEL_CHUNK), 1)
    rows = GROUP * Q_BLOCK
    for g in range(KV_HEADS):
        q = jnp.concatenate(
            [q_ref[0, :, (g * GROUP + r) * HEAD_DIM:(g * GROUP + r + 1) * HEAD_DIM] for r in range(GROUP)], axis=0)
        sel = sel_ref[0, :, g * nb:(g + 1) * nb]

        def chunk(c, carry):
            m, l, acc = carry
            start = pl.multiple_of(c * SEL_CHUNK, SEL_CHUNK)
            k = kv_ref[0, pl.ds(start, SEL_CHUNK), g * HEAD_DIM:(g + 1) * HEAD_DIM]
            v = kv_ref[0, pl.ds(start, SEL_CHUNK), (KV_HEADS + g) * HEAD_DIM:(KV_HEADS + g + 1) * HEAD_DIM]
            picked = _expand_blocks(sel, c * (SEL_CHUNK // SEL_BLOCK), SEL_CHUNK)
            bias = jnp.where(picked > 0.5, jnp.where((start + lane) <= pos, 0.0, NEG_INF), NEG_INF)
            s = _dot_nt(q, k) * (SCALE * LOG2E) + jnp.concatenate([bias] * GROUP, axis=0)
            m_new = jnp.maximum(m, jnp.max(s, axis=-1, keepdims=True))
            m_safe = jnp.where(m_new == NEG_INF, 0.0, m_new)
            p = jnp.exp2(s - m_safe)
            alpha = jnp.exp2(m - m_safe)
            l = alpha * l + jnp.sum(p, axis=-1, keepdims=True)
            acc = alpha * acc + _dot(p.astype(BF16), v)
            return m_new, l, acc

        def pair(cc, carry):
            return chunk(2 * cc + 1, chunk(2 * cc, carry))

        init = (jnp.full((rows, 1), NEG_INF, F32), jnp.zeros((rows, 1), F32), jnp.zeros((rows, HEAD_DIM), F32))
        state = lax.fori_loop(0, n_chunks // 2, pair, init)
        m, l, acc = lax.cond(n_chunks % 2 == 1, lambda st: chunk(n_chunks - 1, st), lambda st: st, state)
        o = acc * (1.0 / jnp.maximum(l, 1e-30))
        for r in range(GROUP):
            hcol = (g * GROUP + r) * HEAD_DIM
            o_ref[0, :, hcol:hcol + HEAD_DIM] = o[r * Q_BLOCK:(r + 1) * Q_BLOCK]


def _sel_prompt(qq3, sel, kvb3):
    b, t, _ = qq3.shape
    nb2 = sel.shape[2]
    return pl.pallas_call(
        _sel_prompt_kernel,
        grid=(b, t // Q_BLOCK),
        in_specs=[pl.BlockSpec((1, Q_BLOCK, 1024), lambda bi, i: (bi, i, 1)),
                  pl.BlockSpec((1, Q_BLOCK, nb2), lambda bi, i: (bi, i, 0)),
                  pl.BlockSpec((1, t, 512), lambda bi, i: (bi, 0, 0))],
        out_specs=pl.BlockSpec((1, Q_BLOCK, 1024), lambda bi, i: (bi, i, 0)),
        out_shape=jax.ShapeDtypeStruct((b, t, 1024), F32),
        compiler_params=_params("parallel", "arbitrary"),
        name="sel_prompt",
    )(qq3, sel, kvb3)


def _win_prompt_kernel(q_ref, kv_ref, o_ref, *, span):
    i = pl.program_id(1)
    pos = i * Q_BLOCK + lax.broadcasted_iota(jnp.int32, (Q_BLOCK, 1), 0)
    pos = jnp.concatenate([pos] * GROUP, axis=0)
    start = pl.multiple_of(jnp.maximum(i * Q_BLOCK + Q_BLOCK - span, 0), Q_BLOCK)
    kpos = start + lax.broadcasted_iota(jnp.int32, (1, span), 1)
    diff = pos - kpos
    ok = (diff >= 0) & (diff < WINDOW)
    for g in range(KV_HEADS):
        q = jnp.concatenate(
            [q_ref[0, :, (g * GROUP + r) * HEAD_DIM:(g * GROUP + r + 1) * HEAD_DIM] for r in range(GROUP)], axis=0)
        k = kv_ref[0, pl.ds(start, span), g * HEAD_DIM:(g + 1) * HEAD_DIM]
        v = kv_ref[0, pl.ds(start, span), (KV_HEADS + g) * HEAD_DIM:(KV_HEADS + g + 1) * HEAD_DIM]
        s = _dot_nt(q, k) * SCALE
        (p,) = _masked_softmax_parts([s], [ok])
        o = _dot(p.astype(BF16), v)
        for r in range(GROUP):
            hcol = (g * GROUP + r) * HEAD_DIM
            o_ref[0, :, hcol:hcol + HEAD_DIM] = o[r * Q_BLOCK:(r + 1) * Q_BLOCK]


def _win_prompt(qq3, kvb3):
    b, t, _ = qq3.shape
    span = min(WINDOW + Q_BLOCK, t)
    return pl.pallas_call(
        functools.partial(_win_prompt_kernel, span=span),
        grid=(b, t // Q_BLOCK),
        in_specs=[pl.BlockSpec((1, Q_BLOCK, 1024), lambda bi, i: (bi, i, 1)),
                  pl.BlockSpec((1, t, 512), lambda bi, i: (bi, 0, 1))],
        out_specs=pl.BlockSpec((1, Q_BLOCK, 1024), lambda bi, i: (bi, i, 0)),
        out_shape=jax.ShapeDtypeStruct((b, t, 1024), F32),
        compiler_params=_params("parallel", "arbitrary"),
        name="win_prompt",
    )(qq3, kvb3)


def _pool_kernel(prev_ref, u_ref, w_ref, sc_ref, o_ref, *, base, zero_first_prev):
    i = pl.program_id(1)
    tq = u_ref.shape[1]
    cur = u_ref[0]
    prev = prev_ref[0]
    if zero_first_prev:
        prev = jnp.where(i == 0, 0.0, prev)
    ext = jnp.concatenate([prev, cur], axis=0)
    gpos = base + i * tq + lax.broadcasted_iota(jnp.int32, (tq, 1), 0)
    gw = ext.shape[1] // len(POOL_WINDOWS)
    for gi, w in enumerate(POOL_WINDOWS):
        s = ext[:, gi * gw:(gi + 1) * gw]
        span = 1
        while span < w:
            s = s + pltpu.roll(s, span, 0)
            span *= 2
        cnt = jnp.minimum(gpos + 1, w).astype(F32)
        d = s[POOL_PREV:] / cnt - cur[:, gi * gw:(gi + 1) * gw]
        o = _dot(d.astype(BF16), w_ref[gi]) * sc_ref[:, gi * gw:(gi + 1) * gw]
        o_ref[0, :, gi * gw:(gi + 1) * gw] = o


def _pool(prev, u3, w, scale, *, base, zero_first_prev, tq=512):
    b, t, c = u3.shape
    tq = min(tq, t)
    ratio = tq // POOL_PREV
    if zero_first_prev:
        prev_map = lambda bi, i: (bi, jnp.maximum(i * ratio - 1, 0), 0)
    else:
        prev_map = lambda bi, i: (bi, 0, 0)
    ng = len(POOL_WINDOWS)
    return pl.pallas_call(
        functools.partial(_pool_kernel, base=base, zero_first_prev=zero_first_prev),
        grid=(b, t // tq),
        in_specs=[pl.BlockSpec((1, POOL_PREV, c), prev_map),
                  pl.BlockSpec((1, tq, c), lambda bi, i: (bi, i, 0)),
                  pl.BlockSpec((ng, c // ng, c // ng), lambda bi, i: (0, 0, 0)),
                  pl.BlockSpec((1, c), lambda bi, i: (0, 0))],
        out_specs=pl.BlockSpec((1, tq, c), lambda bi, i: (bi, i, 0)),
        out_shape=jax.ShapeDtypeStruct((b, t, c), F32),
        compiler_params=_params("parallel", "parallel"),
        name="pool_mix",
    )(prev, u3, w, scale)


def _mix_kernel(oc_ref, os_ref, ow_ref, gt_ref, po_ref, ga_ref, gb_ref, wa_ref, wp_ref, mix_ref):
    gt = gt_ref[...]
    cols = []
    for h in range(N_HEADS):
        sl = slice(h * HEAD_DIM, (h + 1) * HEAD_DIM)
        o = (gt[:, h:h + 1] * oc_ref[:, sl] + gt[:, N_HEADS + h:N_HEADS + h + 1] * os_ref[:, sl]
             + gt[:, 2 * N_HEADS + h:2 * N_HEADS + h + 1] * ow_ref[:, sl])
        cols.append(o.astype(BF16))
    a = _dot(jnp.concatenate(cols, axis=1), wa_ref[...])
    p = _dot(po_ref[...].astype(BF16), wp_ref[...])
    mix_ref[...] = (ga_ref[...] * a + gb_ref[...] * p).astype(BF16)


def _mix(oc, os_, ow, gt, po, gab, wa, wp, *, tm=ROW_TILE):
    m = oc.shape[0]
    d = wa.shape[1]
    tm = min(tm, m)
    row = lambda i: (i, 0)
    const = lambda i: (0, 0)
    return pl.pallas_call(
        _mix_kernel,
        grid=(m // tm,),
        in_specs=[pl.BlockSpec((tm, 1024), row), pl.BlockSpec((tm, 1024), row), pl.BlockSpec((tm, 1024), row),
                  pl.BlockSpec((tm, 128), row), pl.BlockSpec((tm, 1024), row),
                  pl.BlockSpec((tm, d), lambda i: (i, 0)), pl.BlockSpec((tm, d), lambda i: (i, 1)),
                  pl.BlockSpec((1024, d), const), pl.BlockSpec((1024, d), const)],
        out_specs=pl.BlockSpec((tm, d), row),
        out_shape=jax.ShapeDtypeStruct((m, d), BF16),
        compiler_params=_params("parallel"),
        name="branch_mix",
    )(oc, os_, ow, gt, po, gab, gab, wa, wp)


def _out_kernel(x_ref, mix_ref, w_ref, o_ref):
    o_ref[...] = x_ref[...] + _dot(mix_ref[...], w_ref[...])


def _out_proj(x, mix, w, *, tm=ROW_TILE):
    m, d = x.shape
    tm = min(tm, m)
    return pl.pallas_call(
        _out_kernel,
        grid=(m // tm,),
        in_specs=[pl.BlockSpec((tm, d), lambda i: (i, 0)), pl.BlockSpec((tm, d), lambda i: (i, 0)),
                  pl.BlockSpec((d, d), lambda i: (0, 0))],
        out_specs=pl.BlockSpec((tm, d), lambda i: (i, 0)),
        out_shape=jax.ShapeDtypeStruct((m, d), F32),
        compiler_params=_params("parallel"),
        name="out_proj",
    )(x, mix, w)


def _pair_candidates(sv, si):
    k = PEER_TOPK
    row = lax.broadcasted_iota(jnp.int32, (8, 1), 0)
    vals, ids = [], []
    for a in range(k // 2):
        lim = k // (a + 1)
        nrow = k if a == 0 else 8
        v = sv[0][a:a + 1] + sv[1][:nrow]
        if lim < nrow:
            v = jnp.where(row < lim, v, NEG_INF)
        vals.append(v)
        ids.append(si[0][a:a + 1] * float(PEER_NKEYS) + si[1][:nrow])
    vals.append(sv[0][k // 2:] + sv[1][0:1])
    ids.append(si[0][k // 2:] * float(PEER_NKEYS) + si[1][0:1])
    return jnp.concatenate(vals, axis=0), jnp.concatenate(ids, axis=0)


def _peer_score_kernel(x_ref, g_ref, wq_ref, keys_ref, ids_ref, gw_ref):
    h = _rms(x_ref[...], g_ref[...]).astype(BF16)
    q = _dot(h, wq_ref[...]).astype(BF16)
    dk = PEER_NKEYS
    ids, gws = [], []
    for hd in range(PEER_HEADS):
        sv, si = [], []
        for c in range(2):
            qhc = q[:, (hd * 2 + c) * dk:(hd * 2 + c + 1) * dk]
            st = _dot_nt(keys_ref[hd * 2 + c], qhc)
            v, ix = _topk_sorted(st, PEER_TOPK)
            sv.append(v)
            si.append(ix)
        comb, eid = _pair_candidates(sv, si)
        cv, ce = _topk_sorted(comb, PEER_TOPK, payload=eid)
        e = jnp.exp(cv - cv[0:1])
        gws.append(e / jnp.sum(e, axis=0, keepdims=True))
        ids.append(ce)
    ids_ref[...] = jnp.concatenate(ids, axis=0).T.astype(jnp.int32)
    gw_ref[...] = jnp.concatenate(gws, axis=0)


def _peer_score(x, g, wq, keys, *, tb=128):
    m, d = x.shape
    nk = PEER_HEADS * PEER_TOPK
    return pl.pallas_call(
        _peer_score_kernel,
        grid=(m // tb,),
        in_specs=[pl.BlockSpec((tb, d), lambda i: (i, 0)), pl.BlockSpec((1, d), lambda i: (0, 0)),
                  pl.BlockSpec(wq.shape, lambda i: (0, 0)),
                  pl.BlockSpec(keys.shape, lambda i: (0, 0, 0))],
        out_specs=[pl.BlockSpec((tb, nk), lambda i: (i, 0)), pl.BlockSpec((nk, tb), lambda i: (0, i))],
        out_shape=[jax.ShapeDtypeStruct((m, nk), jnp.int32), jax.ShapeDtypeStruct((nk, m), F32)],
        compiler_params=_params("parallel"),
        name="peer_score",
    )(x, g, wq, keys)


def _gelu(x):
    return 0.5 * x * (1.0 + lax.erf(x * (2.0 ** -0.5)))


PACK_ROWS = 256


def _pack_kernel(u_ref, v_ref, o_ref):
    lo = pltpu.bitcast(u_ref[...].astype(BF16).astype(F32), jnp.uint32) >> 16
    hi = pltpu.bitcast(v_ref[...].astype(BF16).astype(F32), jnp.uint32) & jnp.uint32(0xFFFF0000)
    o_ref[...] = (lo | hi).reshape(o_ref.shape)


def _pack_tables(u, v):
    e, d = u.shape
    return pl.pallas_call(
        _pack_kernel,
        grid=(e // PACK_ROWS,),
        in_specs=[pl.BlockSpec((PACK_ROWS, d), lambda i: (i, 0)), pl.BlockSpec((PACK_ROWS, d), lambda i: (i, 0))],
        out_specs=pl.BlockSpec((PACK_ROWS, 1, d), lambda i: (i, 0, 0)),
        out_shape=jax.ShapeDtypeStruct((e, 1, d), jnp.uint32),
        compiler_params=_params("parallel"),
        name="peer_pack",
    )(u, v)


PEER_RING = 8


def _peer_apply_kernel(ids_ref, nxt_ref, x_ref, gffn_ref, gw_ref, gfin_ref, tab_ref, y_ref, *scratch):
    bufs = scratch[:PEER_RING]
    sem, hbuf, obuf = scratch[PEER_RING:]
    step = pl.program_id(0)
    tb, d = x_ref.shape
    nk = gw_ref.shape[0]
    nchunk = d // 128
    ahead = PEER_RING - 1
    n_groups = tb // PEER_RING
    hbuf[...] = _rms(x_ref[...], gffn_ref[...])

    def row_copy(ids, t, k, slot):
        return pltpu.make_async_copy(tab_ref.at[ids[t, k]], bufs[slot].at[pl.ds(k, 1), :], sem.at[slot])

    def issue(ids, t, slot):
        for k in range(nk):
            row_copy(ids, t, k, slot).start(priority=k % 2)

    def drain(slot):
        for k in range(nk):
            row_copy(ids_ref, 0, k, slot).wait()

    @pl.when(step == 0)
    def _():
        for s in range(ahead):
            issue(ids_ref, s, s)

    lane = lax.broadcasted_iota(jnp.int32, (nk, tb), 1)

    def compute(t, slot):
        buf = bufs[slot]
        hrow = hbuf[pl.ds(t, 1), :]
        acc = jnp.zeros((nk, 128), F32)
        for j in range(nchunk):
            w = buf[:, j * 128:(j + 1) * 128]
            u = pltpu.bitcast(w << 16, F32)
            acc = acc + u * hrow[:, j * 128:(j + 1) * 128]
        act = jnp.sum(acc, axis=1, keepdims=True)
        gcol = jnp.sum(jnp.where(lane == t, gw_ref[...], 0.0), axis=1, keepdims=True)
        coef = _gelu(act) * gcol
        outs = []
        for j in range(nchunk):
            w = buf[:, j * 128:(j + 1) * 128]
            v = pltpu.bitcast(w & jnp.uint32(0xFFFF0000), F32)
            outs.append(jnp.sum(v * coef, axis=0, keepdims=True))
        obuf[pl.ds(t, 1), :] = jnp.concatenate(outs, axis=1)

    def group(p, last):
        for s in range(PEER_RING):
            t = p * PEER_RING + s
            drain(s)
            if last and s > 0:
                issue(nxt_ref, s - 1, (s + ahead) % PEER_RING)
            else:
                issue(ids_ref, t + ahead, (s + ahead) % PEER_RING)
            compute(t, s)

    def body(p, carry):
        group(p, False)
        return carry

    lax.fori_loop(0, n_groups - 1, body, 0)
    group(n_groups - 1, True)

    @pl.when(step == pl.num_programs(0) - 1)
    def _():
        for s in range(ahead):
            drain(s)

    y_ref[...] = _rms(x_ref[...] + obuf[...], gfin_ref[...])


def _peer_apply(ids, x, gffn, gw, gfin, table, *, tb=128):
    m, d = x.shape
    nk = ids.shape[1]
    steps = m // tb
    return pl.pallas_call(
        _peer_apply_kernel,
        grid=(steps,),
        in_specs=[pl.BlockSpec((tb, nk), lambda i: (i, 0), memory_space=pltpu.SMEM),
                  pl.BlockSpec((tb, nk), lambda i: (jnp.minimum(i + 1, steps - 1), 0), memory_space=pltpu.SMEM),
                  pl.BlockSpec((tb, d), lambda i: (i, 0)),
                  pl.BlockSpec((1, d), lambda i: (0, 0)),
                  pl.BlockSpec((nk, tb), lambda i: (0, i)),
                  pl.BlockSpec((1, d), lambda i: (0, 0)),
                  pl.BlockSpec(memory_space=pl.ANY)],
        out_specs=pl.BlockSpec((tb, d), lambda i: (i, 0)),
        out_shape=jax.ShapeDtypeStruct((m, d), F32),
        scratch_shapes=[pltpu.VMEM((nk, d), jnp.uint32)] * PEER_RING + [
            pltpu.SemaphoreType.DMA((PEER_RING,)), pltpu.VMEM((tb, d), F32), pltpu.VMEM((tb, d), F32)],
        compiler_params=_params("arbitrary"),
        name="peer_apply",
    )(ids, ids, x, gffn, gw, gfin, table)


def _stack_heads(q_ref, g):
    return jnp.concatenate(
        [q_ref[0, :, (g * GROUP + r) * HEAD_DIM:(g * GROUP + r + 1) * HEAD_DIM] for r in range(GROUP)], axis=0)


NEW_PAD = 128


def _pad_new(x):
    return jnp.concatenate([x, jnp.zeros((NEW_PAD - x.shape[0], x.shape[1]), F32)], axis=0).astype(BF16)


def _cmp_sample_kernel(q_ref, y_ref, o_ref, sel_ref, kcp_ref, tmp_ref, *, past):
    _combine_compressed(y_ref, kcp_ref, tmp_ref)
    t = q_ref.shape[1]
    nb = kcp_ref.shape[1]
    tpos = past + lax.broadcasted_iota(jnp.int32, (t, 1), 0)
    pos = jnp.concatenate([tpos] * GROUP, axis=0)
    blk = lax.broadcasted_iota(jnp.int32, (1, nb), 1)
    for g in range(KV_HEADS):
        q = _stack_heads(q_ref, g)
        outs, imp = _cmp_attention([q], pos, kcp_ref, g)
        o = outs[0]
        for r in range(GROUP):
            hcol = (g * GROUP + r) * HEAD_DIM
            o_ref[0, :, hcol:hcol + HEAD_DIM] = o[r * t:(r + 1) * t]
        imp_t = functools.reduce(jnp.add, [imp[r * t:(r + 1) * t] for r in range(GROUP)])
        v = jnp.where(blk == 0, FORCE_SCORE, imp_t)
        sel = _topk_select(v, SEL_TOPN - 1, 1)
        sel_ref[0, :, g * nb:(g + 1) * nb] = sel.astype(BF16)


def _cmp_sample(qq3, y, *, past):
    b, t, _ = qq3.shape
    ns = y.shape[1]
    nb = ns // CMP_PER_SEL
    return pl.pallas_call(
        functools.partial(_cmp_sample_kernel, past=past),
        grid=(b,),
        in_specs=[pl.BlockSpec((1, t, 1024), lambda bi: (bi, 0, 0)),
                  pl.BlockSpec((1, ns, 1024), lambda bi: (bi, 0, 0))],
        out_specs=[pl.BlockSpec((1, t, 1024), lambda bi: (bi, 0, 0)),
                   pl.BlockSpec((1, t, KV_HEADS * nb), lambda bi: (bi, 0, 0))],
        out_shape=[jax.ShapeDtypeStruct((b, t, 1024), F32),
                   jax.ShapeDtypeStruct((b, t, KV_HEADS * nb), BF16)],
        scratch_shapes=[pltpu.VMEM((2 * KV_HEADS * CMP_PER_SEL, nb, HEAD_DIM), BF16),
                        pltpu.VMEM((ns, HEAD_DIM), F32)],
        compiler_params=_params("parallel"),
        name="cmp_sample",
    )(qq3, y)


def _sel_sample_kernel(q_ref, sel_ref, kv_ref, new_ref, o_ref, m_ref, l_ref, acc_ref):
    c = pl.program_id(1)
    nc = pl.num_programs(1)
    t = q_ref.shape[1]
    nb = sel_ref.shape[2] // KV_HEADS
    width = kv_ref.shape[1]

    @pl.when(c == 0)
    def _():
        m_ref[...] = jnp.full(m_ref.shape, NEG_INF, F32)
        l_ref[...] = jnp.zeros(l_ref.shape, F32)
        acc_ref[...] = jnp.zeros(acc_ref.shape, F32)

    def update(g, s, ok, v):
        s = jnp.where(ok, s, NEG_INF)
        m = m_ref[g]
        m_new = jnp.maximum(m, jnp.max(s, axis=-1, keepdims=True))
        m_safe = jnp.where(m_new == NEG_INF, 0.0, m_new)
        p = jnp.exp(s - m_safe)
        alpha = jnp.exp(m - m_safe)
        l_ref[g] = alpha * l_ref[g] + jnp.sum(p, axis=-1, keepdims=True)
        acc_ref[g] = alpha * acc_ref[g] + _dot(p.astype(BF16), v)
        m_ref[g] = m_new

    for g in range(KV_HEADS):
        q = _stack_heads(q_ref, g)
        k = kv_ref[0, :, g * HEAD_DIM:(g + 1) * HEAD_DIM]
        v = kv_ref[0, :, (KV_HEADS + g) * HEAD_DIM:(KV_HEADS + g + 1) * HEAD_DIM]
        s = _dot_nt(q, k) * SCALE
        picked = _expand_blocks(sel_ref[0, :, g * nb:(g + 1) * nb], c * (width // SEL_BLOCK), width)
        picked = jnp.concatenate([picked] * GROUP, axis=0)
        update(g, s, picked > 0.5, v)

    @pl.when(c == nc - 1)
    def _():
        qi = lax.broadcasted_iota(jnp.int32, (t, 1), 0)
        qi = jnp.concatenate([qi] * GROUP, axis=0)
        ok = lax.broadcasted_iota(jnp.int32, (1, NEW_PAD), 1) <= qi
        for g in range(KV_HEADS):
            q = _stack_heads(q_ref, g)
            k = _pad_new(new_ref[pl.ds(2 * KV_HEADS + g, t, stride=ROWS_PER_POS), :])
            v = _pad_new(new_ref[pl.ds(3 * KV_HEADS + g, t, stride=ROWS_PER_POS), :])
            update(g, _dot_nt(q, k) * SCALE, ok, v)
            o = acc_ref[g] * (1.0 / jnp.maximum(l_ref[g], 1e-30))
            for r in range(GROUP):
                hcol = (g * GROUP + r) * HEAD_DIM
                o_ref[0, :, hcol:hcol + HEAD_DIM] = o[r * t:(r + 1) * t]


SEL_SAMPLE_CHUNK = 2048


def _sel_sample(kvs, qq3, sel, nsa_rows):
    nbatch, past, _ = kvs.shape
    t = qq3.shape[1]
    nb2 = sel.shape[2]
    rows = GROUP * t
    width = min(SEL_SAMPLE_CHUNK, past)
    return pl.pallas_call(
        _sel_sample_kernel,
        grid=(nbatch, past // width),
        in_specs=[pl.BlockSpec((1, t, 1024), lambda bi, ci: (bi, 0, 1)),
                  pl.BlockSpec((1, t, nb2), lambda bi, ci: (bi, 0, 0)),
                  pl.BlockSpec((1, width, 512), lambda bi, ci: (bi, ci, 0)),
                  pl.BlockSpec((t * ROWS_PER_POS, HEAD_DIM), lambda bi, ci: (bi, 0))],
        out_specs=pl.BlockSpec((1, t, 1024), lambda bi, ci: (bi, 0, 0)),
        out_shape=jax.ShapeDtypeStruct((nbatch, t, 1024), F32),
        scratch_shapes=[pltpu.VMEM((KV_HEADS, rows, 1), F32), pltpu.VMEM((KV_HEADS, rows, 1), F32),
                        pltpu.VMEM((KV_HEADS, rows, HEAD_DIM), F32)],
        compiler_params=_params("parallel", "arbitrary"),
        name="sel_sample",
    )(qq3, sel, kvs, nsa_rows)


def _win_sample_kernel(q_ref, st_ref, new_ref, o_ref):
    t = q_ref.shape[1]
    nw = st_ref.shape[0] // WIN_ROWS_PER_POS
    old = lambda comp: st_ref[pl.ds(comp, nw, stride=WIN_ROWS_PER_POS), :].astype(BF16)
    new = lambda comp: _pad_new(new_ref[pl.ds(comp, t, stride=WIN_ROWS_PER_POS), :])
    qi = lax.broadcasted_iota(jnp.int32, (t, 1), 0)
    qi = jnp.concatenate([qi] * GROUP, axis=0)
    d_old = (nw + qi) - lax.broadcasted_iota(jnp.int32, (1, nw), 1)
    ok_old = (d_old >= 0) & (d_old < WINDOW)
    d_new = qi - lax.broadcasted_iota(jnp.int32, (1, NEW_PAD), 1)
    ok_new = (d_new >= 0) & (d_new < WINDOW)
    for g in range(KV_HEADS):
        q = _stack_heads(q_ref, g)
        k_old, v_old = old(g), old(KV_HEADS + g)
        k_new, v_new = new(g), new(KV_HEADS + g)
        p_old, p_new = _masked_softmax_parts(
            [_dot_nt(q, k_old) * SCALE, _dot_nt(q, k_new) * SCALE], [ok_old, ok_new])
        o = _dot(p_old.astype(BF16), v_old) + _dot(p_new.astype(BF16), v_new)
        for r in range(GROUP):
            hcol = (g * GROUP + r) * HEAD_DIM
            o_ref[0, :, hcol:hcol + HEAD_DIM] = o[r * t:(r + 1) * t]


def _win_sample(qq3, state_rows, win_rows):
    b, t, _ = qq3.shape
    nw = state_rows.shape[0] // (b * WIN_ROWS_PER_POS)
    return pl.pallas_call(
        _win_sample_kernel,
        grid=(b,),
        in_specs=[pl.BlockSpec((1, t, 1024), lambda bi: (bi, 0, 1)),
                  pl.BlockSpec((nw * WIN_ROWS_PER_POS, HEAD_DIM), lambda bi: (bi, 0)),
                  pl.BlockSpec((t * WIN_ROWS_PER_POS, HEAD_DIM), lambda bi: (bi, 0))],
        out_specs=pl.BlockSpec((1, t, 1024), lambda bi: (bi, 0, 0)),
        out_shape=jax.ShapeDtypeStruct((b, t, 1024), F32),
        compiler_params=_params("parallel"),
        name="win_sample",
    )(qq3, state_rows, win_rows)


def _rope_tables(pos):
    half = HEAD_DIM // 2
    inv = ROPE_THETA ** (-jnp.arange(half, dtype=F32) / half)
    ang = pos.astype(F32)[:, None] * inv[None, :]
    cos, sin = jnp.cos(ang), jnp.sin(ang)
    return jnp.concatenate([cos, cos], axis=-1), jnp.concatenate([-sin, sin], axis=-1)


def _prep_weights(w_in, w_phi_k, w_phi_v, w_pool_group, pool_scale, w_branch_attn, w_branch_pool, w_out,
                  peer_w_query, peer_sub_keys, peer_u, peer_v):
    d = w_in.shape[0]
    qw = N_HEADS * HEAD_DIM
    kvw = 6 * KV_HEADS * HEAD_DIM
    ngw = 3 * N_HEADS
    pw = d // 2
    o1, o2, o3, o4 = qw, qw + kvw, qw + kvw + ngw, qw + kvw + ngw + pw
    wb = w_in.astype(BF16)
    w_qkv = jnp.concatenate([wb[:, :o2], wb[:, o2:o3], jnp.zeros((d, HEAD_DIM - ngw), BF16)], axis=1)
    cat = lambda w: jnp.concatenate([w[:CMP_STRIDE], w[CMP_STRIDE:]], axis=-1).astype(BF16)
    return dict(
        w_qkv=w_qkv, w_u=wb[:, o3:o4], w_gab=wb[:, o4:],
        wk=cat(w_phi_k), wv=cat(w_phi_v),
        w_pool=w_pool_group.astype(BF16), pool_scale=pool_scale.reshape(1, -1),
        w_ba=w_branch_attn.astype(BF16), w_bp=w_branch_pool.astype(BF16), w_out=w_out.astype(BF16),
        w_query=peer_w_query.astype(BF16),
        keys=peer_sub_keys.reshape(PEER_HEADS * 2, PEER_NKEYS, -1).astype(BF16),
        table=_pack_tables(peer_u, peer_v),
    )


def _token_tail(x2d, o_cmp, o_sel, o_win, gates, pool_out, gab, wp, g_ffn, g_final):
    mix = _mix(o_cmp, o_sel, o_win, gates, pool_out, gab, wp["w_ba"], wp["w_bp"])
    x2 = _out_proj(x2d, mix, wp["w_out"])
    ids, gw = _peer_score(x2, g_ffn, wp["w_query"], wp["keys"])
    return _peer_apply(ids, x2, g_ffn, gw, g_final, wp["table"])


def kernel(x_prompt, x_sample, cache_kv_nsa, state_win_kv, state_pool, page_table, g_norm_mix, w_in, w_phi_k,
           w_phi_v, w_pool_group, pool_scale, w_branch_attn, w_branch_pool, w_out, g_norm_ffn, peer_w_query,
           peer_sub_keys, peer_u, peer_v, g_norm_final):
    assert g_norm_mix.shape[0] == 1, "single-layer step"
    bp, t, d = x_prompt.shape
    bs, ts, _ = x_sample.shape
    n_pages = page_table.shape[1]
    page_size = cache_kv_nsa.shape[2]
    past = n_pages * page_size
    wp = _prep_weights(w_in[0], w_phi_k[0], w_phi_v[0], w_pool_group[0], pool_scale[0], w_branch_attn[0],
                       w_branch_pool[0], w_out[0], peer_w_query[0], peer_sub_keys[0], peer_u[0], peer_v[0])
    g_mix = g_norm_mix[0].reshape(1, d)
    g_ffn = g_norm_ffn[0].reshape(1, d)
    g_fin = g_norm_final.reshape(1, d)

    xp = x_prompt.reshape(bp * t, d)
    cos, sin = _rope_tables(jnp.arange(t))
    cos, sin = jnp.tile(cos, (bp, 1)), jnp.tile(sin, (bp, 1))
    qq, nsa, win, gates, kvb, hn = _qkv_proj(xp, g_mix, wp["w_qkv"], cos, sin)
    u = _proj(hn, wp["w_u"])
    gab = _proj(hn, wp["w_gab"], act="sigmoid", out_dtype=BF16)
    qq3, kvb3 = qq.reshape(bp, t, -1), kvb.reshape(bp, t, -1)
    y = _compress_prompt(nsa, bp, wp["wk"], wp["wv"])
    o_cmp, sel = _cmp_prompt(qq3, y)
    o_sel = _sel_prompt(qq3, sel, kvb3)
    o_win = _win_prompt(qq3, kvb3)
    u3 = u.reshape(bp, t, -1)
    pool_out = _pool(u3, u3, wp["w_pool"], wp["pool_scale"], base=0, zero_first_prev=True)
    y_prompt = _token_tail(xp, o_cmp.reshape(bp * t, -1), o_sel.reshape(bp * t, -1), o_win.reshape(bp * t, -1),
                           gates, pool_out.reshape(bp * t, -1), gab, wp, g_ffn, g_fin)
    wlen = min(WINDOW, t)
    new_kv_p = nsa.reshape(1, bp, t, 4, KV_HEADS, HEAD_DIM)
    new_win_p = win.reshape(bp, t, 2, KV_HEADS, HEAD_DIM)[None, :, t - wlen:]
    new_pool_p = u3[None, :, t - (POOL_PREV - 1):]

    xs = x_sample.reshape(bs * ts, d)
    cos_s, sin_s = _rope_tables(past + jnp.arange(ts))
    cos_s, sin_s = jnp.tile(cos_s, (bs, 1)), jnp.tile(sin_s, (bs, 1))
    qq_s, nsa_s, win_s, gates_s, _, hn_s = _qkv_proj(xs, g_mix, wp["w_qkv"], cos_s, sin_s)
    u_s = _proj(hn_s, wp["w_u"])
    gab_s = _proj(hn_s, wp["w_gab"], act="sigmoid", out_dtype=BF16)
    qq_s3 = qq_s.reshape(bs, ts, -1)
    cache_rows = cache_kv_nsa.reshape(-1, HEAD_DIM)
    y_s, kvs = _compress_pages(cache_rows, page_table, page_size, wp["wk"], wp["wv"])
    o_cmp_s, sel_s = _cmp_sample(qq_s3, y_s, past=past)
    o_sel_s = _sel_sample(kvs, qq_s3, sel_s, nsa_s)
    o_win_s = _win_sample(qq_s3, state_win_kv.reshape(-1, HEAD_DIM), win_s)
    u_s3 = u_s.reshape(bs, ts, -1)
    st_pool = state_pool[0]
    prev = jnp.pad(st_pool, ((0, 0), (POOL_PREV - st_pool.shape[1], 0), (0, 0)))
    pool_out_s = _pool(prev, u_s3, wp["w_pool"], wp["pool_scale"], base=st_pool.shape[1], zero_first_prev=False)
    y_sample = _token_tail(xs, o_cmp_s.reshape(bs * ts, -1), o_sel_s.reshape(bs * ts, -1),
                           o_win_s.reshape(bs * ts, -1), gates_s, pool_out_s.reshape(bs * ts, -1), gab_s, wp,
                           g_ffn, g_fin)
    new_kv_s = nsa_s.reshape(1, bs, ts, 4, KV_HEADS, HEAD_DIM)
    win_ext = jnp.concatenate([state_win_kv, win_s.reshape(1, bs, ts, 2, KV_HEADS, HEAD_DIM)], axis=2)
    new_win_s = win_ext[:, :, win_ext.shape[2] - min(WINDOW, win_ext.shape[2]):]
    pool_ext = jnp.concatenate([st_pool, u_s3], axis=1)
    new_pool_s = pool_ext[None, :, pool_ext.shape[1] - (POOL_PREV - 1):]

    return (y_prompt.reshape(bp, t, d), y_sample.reshape(bs, ts, d), new_kv_p, new_kv_s, new_win_p, new_win_s,
            new_pool_p, new_pool_s)
```

```python
import functools

import jax
import jax.numpy as jnp
from jax import lax
from jax.experimental import pallas as pl
from jax.experimental.pallas import tpu as pltpu

F32 = jnp.float32
BF16 = jnp.bfloat16

HEAD_DIM = 128
N_HEADS = 8
KV_HEADS = 2
GROUP = N_HEADS // KV_HEADS
CMP_LEN = 32
CMP_STRIDE = 16
SEL_BLOCK = 64
SEL_TOPN = 16
CMP_PER_SEL = SEL_BLOCK // CMP_STRIDE
WINDOW = 512
Q_BLOCK = 128
ROPE_THETA = 10000.0
FORCE_SCORE = 1e4
POOL_WINDOWS = (2, 4, 8, 16)
POOL_PREV = 16
PEER_HEADS = 8
PEER_NKEYS = 128
PEER_TOPK = 16
EPS = 1e-6
SCALE = HEAD_DIM ** -0.5
LOG2E = 1.4426950408889634
NEG_INF = float("-inf")

ROWS_PER_POS = 4 * KV_HEADS
WIN_ROWS_PER_POS = 2 * KV_HEADS

ROW_TILE = 512
VMEM_LIMIT = 56 * 1024 * 1024


def _params(*sem):
    return pltpu.CompilerParams(dimension_semantics=sem, vmem_limit_bytes=VMEM_LIMIT)


def _dot(a, b):
    return jnp.dot(a, b, preferred_element_type=F32)


def _dot_nt(a, b):
    return lax.dot_general(a, b, (((1,), (1,)), ((), ())), preferred_element_type=F32)


def _rms(x, g):
    return x * lax.rsqrt(jnp.mean(x * x, axis=-1, keepdims=True) + EPS) * g


def _masked_softmax_parts(parts, masks):
    parts = [jnp.where(mk, s, NEG_INF) for s, mk in zip(parts, masks)]
    same = all(s.shape == parts[0].shape for s in parts)

    def across(op, red, xs):
        if same:
            return red(functools.reduce(op, xs), axis=-1, keepdims=True)
        return functools.reduce(op, [red(x, axis=-1, keepdims=True) for x in xs])

    m = across(jnp.maximum, jnp.max, parts)
    m = jnp.where(m == NEG_INF, 0.0, m)
    es = [jnp.exp(s - m) for s in parts]
    den = across(jnp.add, jnp.sum, es)
    inv = 1.0 / jnp.maximum(den, 1e-30)
    return [e * inv for e in es]


def _topk_select(v, n, axis):
    size = v.shape[axis]
    idx = lax.broadcasted_iota(jnp.int32, v.shape, axis).astype(F32)
    sel = jnp.zeros(v.shape, F32)
    for _ in range(n):
        m = jnp.max(v, axis=axis, keepdims=True)
        first = jnp.min(jnp.where(v == m, idx, float(size)), axis=axis, keepdims=True)
        hit = idx == first
        v = jnp.where(hit, NEG_INF, v)
        sel = jnp.where(hit, 1.0, sel)
    return sel


def _topk_sorted(v, n, payload=None):
    size = v.shape[0]
    idx = lax.broadcasted_iota(jnp.int32, v.shape, 0).astype(F32)
    vals, picks = [], []
    for _ in range(n):
        m = jnp.max(v, axis=0, keepdims=True)
        first = jnp.min(jnp.where(v == m, idx, float(size)), axis=0, keepdims=True)
        hit = idx == first
        vals.append(m)
        if payload is None:
            picks.append(first)
        else:
            picks.append(jnp.max(jnp.where(hit, payload, -1.0), axis=0, keepdims=True))
        v = jnp.where(hit, NEG_INF, v)
    return jnp.concatenate(vals, axis=0), jnp.concatenate(picks, axis=0)


def _proj_kernel(h_ref, w_ref, o_ref, *, act, tn):
    h = h_ref[...]
    for c in range(w_ref.shape[1] // tn):
        z = _dot(h, w_ref[:, c * tn:(c + 1) * tn])
        if act == "sigmoid":
            z = jax.nn.sigmoid(z)
        o_ref[:, c * tn:(c + 1) * tn] = z.astype(o_ref.dtype)


def _proj(h, w, *, act=None, out_dtype=F32, tm=ROW_TILE, tn=512, wn=2048):
    m, d = h.shape
    n = w.shape[1]
    tm = min(tm, m)
    wn = min(wn, n)
    return pl.pallas_call(
        functools.partial(_proj_kernel, act=act, tn=tn),
        grid=(n // wn, m // tm),
        in_specs=[pl.BlockSpec((tm, d), lambda j, i: (i, 0)),
                  pl.BlockSpec((d, wn), lambda j, i: (0, j))],
        out_specs=pl.BlockSpec((tm, wn), lambda j, i: (i, j)),
        out_shape=jax.ShapeDtypeStruct((m, n), out_dtype),
        compiler_params=_params("parallel", "parallel"),
        name="mixer_proj",
    )(h, w)


QKV_COLS = N_HEADS * HEAD_DIM + 6 * KV_HEADS * HEAD_DIM + HEAD_DIM


def _qkv_kernel(x_ref, g_ref, w_ref, cos_ref, sin_ref, qq_ref, nsa_ref, win_ref, gate_ref, kvb_ref, h_ref):
    h = _rms(x_ref[...], g_ref[...]).astype(BF16)
    h_ref[...] = h
    cos = cos_ref[...]
    sin = sin_ref[...]

    def rope(z):
        return z * cos + pltpu.roll(z, HEAD_DIM // 2, 1) * sin

    qw = N_HEADS * HEAD_DIM
    for c in range(qw // 512):
        z = _dot(h, w_ref[:, c * 512:(c + 1) * 512])
        for j in range(4):
            zh = z[:, j * 128:(j + 1) * 128]
            col = c * 512 + j * 128
            qq_ref[:, col:col + 128] = zh.astype(BF16)
            qq_ref[:, qw + col:qw + col + 128] = rope(zh).astype(BF16)
    tm = x_ref.shape[0]

    def put(ref, comp, n_comp, val):
        ref[pl.ds(comp, tm, stride=n_comp), :] = val

    for c in range(3):
        z = _dot(h, w_ref[:, qw + c * 512:qw + (c + 1) * 512])
        for j in range(2):
            zk = z[:, j * 128:(j + 1) * 128]
            zv = z[:, 256 + j * 128:256 + (j + 1) * 128]
            if c == 0:
                put(nsa_ref, j, ROWS_PER_POS, zk)
                put(nsa_ref, KV_HEADS + j, ROWS_PER_POS, zv)
                continue
            zr = rope(zk)
            if c == 1:
                put(nsa_ref, 2 * KV_HEADS + j, ROWS_PER_POS, zr)
                put(nsa_ref, 3 * KV_HEADS + j, ROWS_PER_POS, zv)
            else:
                put(win_ref, j, WIN_ROWS_PER_POS, zr)
                put(win_ref, KV_HEADS + j, WIN_ROWS_PER_POS, zv)
            base = (c - 1) * 512
            kvb_ref[:, base + j * 128:base + (j + 1) * 128] = zr.astype(BF16)
            kvb_ref[:, base + 256 + j * 128:base + 256 + (j + 1) * 128] = zv.astype(BF16)
    z = _dot(h, w_ref[:, qw + 1536:qw + 1536 + 128])
    gate_ref[...] = jax.nn.sigmoid(z)


def _qkv_proj(x, g, w, cos, sin, *, tm=ROW_TILE):
    m, d = x.shape
    tm = min(tm, m)
    row = lambda i: (i, 0)
    const = lambda i: (0, 0)
    return pl.pallas_call(
        _qkv_kernel,
        grid=(m // tm,),
        in_specs=[pl.BlockSpec((tm, d), row), pl.BlockSpec((1, d), const),
                  pl.BlockSpec((d, QKV_COLS), const),
                  pl.BlockSpec((tm, HEAD_DIM), row), pl.BlockSpec((tm, HEAD_DIM), row)],
        out_specs=[pl.BlockSpec((tm, 2048), row), pl.BlockSpec((tm * ROWS_PER_POS, HEAD_DIM), row),
                   pl.BlockSpec((tm * WIN_ROWS_PER_POS, HEAD_DIM), row), pl.BlockSpec((tm, 128), row),
                   pl.BlockSpec((tm, 1024), row), pl.BlockSpec((tm, d), row)],
        out_shape=[jax.ShapeDtypeStruct((m, 2048), BF16), jax.ShapeDtypeStruct((m * ROWS_PER_POS, HEAD_DIM), F32),
                   jax.ShapeDtypeStruct((m * WIN_ROWS_PER_POS, HEAD_DIM), F32), jax.ShapeDtypeStruct((m, 128), F32),
                   jax.ShapeDtypeStruct((m, 1024), BF16), jax.ShapeDtypeStruct((m, d), BF16)],
        compiler_params=_params("parallel"),
        name="qkv_proj",
    )(x, g, w, cos, sin)


SUB_ROWS = CMP_STRIDE * ROWS_PER_POS


def _compress_rows(load, n_rows, wk_ref, wv_ref, y_ref):
    for kv in range(2):
        w_ref = wk_ref if kv == 0 else wv_ref
        for g in range(KV_HEADS):
            comp = kv * KV_HEADS + g
            acc = jnp.zeros((n_rows, 2 * HEAD_DIM), F32)
            for i in range(CMP_STRIDE):
                acc = acc + _dot(load(i, comp).astype(BF16), w_ref[i])
            y_ref[0, :, 2 * comp * HEAD_DIM:2 * (comp + 1) * HEAD_DIM] = acc


def _compress_prompt_kernel(x_ref, wk_ref, wv_ref, y_ref):
    ts = x_ref.shape[0] // SUB_ROWS
    load = lambda i, comp: x_ref[pl.ds(i * ROWS_PER_POS + comp, ts, stride=SUB_ROWS), :]
    _compress_rows(load, ts, wk_ref, wv_ref, y_ref)


def _compress_prompt(nsa_rows, b, wk, wv, *, ts=128):
    n_sub = nsa_rows.shape[0] // (b * SUB_ROWS)
    ts = min(ts, n_sub)
    steps = n_sub // ts
    wspec = pl.BlockSpec((CMP_STRIDE, HEAD_DIM, 2 * HEAD_DIM), lambda bi, ci: (0, 0, 0))
    return pl.pallas_call(
        _compress_prompt_kernel,
        grid=(b, steps),
        in_specs=[pl.BlockSpec((ts * SUB_ROWS, HEAD_DIM), lambda bi, ci: (bi * steps + ci, 0)), wspec, wspec],
        out_specs=pl.BlockSpec((1, ts, 1024), lambda bi, ci: (bi, ci, 0)),
        out_shape=jax.ShapeDtypeStruct((b, n_sub, 1024), F32),
        compiler_params=_params("parallel", "parallel"),
        name="compress_prompt",
    )(nsa_rows, wk, wv)


PAGES_PER_STEP = 16


def _compress_pages_kernel(pt_ref, *refs):
    pages = refs[:PAGES_PER_STEP]
    wk_ref, wv_ref, y_ref, kvs_ref = refs[PAGES_PER_STEP:]
    per_page = pages[0].shape[0] // SUB_ROWS
    page_size = pages[0].shape[0] // ROWS_PER_POS

    def load(i, comp):
        rows = pl.ds(i * ROWS_PER_POS + comp, per_page, stride=SUB_ROWS)
        return jnp.concatenate([p[rows, :] for p in pages], axis=0)

    _compress_rows(load, PAGES_PER_STEP * per_page, wk_ref, wv_ref, y_ref)
    for j in range(2 * KV_HEADS):
        rows = pl.ds(2 * KV_HEADS + j, page_size, stride=ROWS_PER_POS)
        kvs_ref[0, :, j * HEAD_DIM:(j + 1) * HEAD_DIM] = jnp.concatenate(
            [p[rows, :] for p in pages], axis=0).astype(BF16)


def _page_specs(page_rows):
    def spec(k):
        return pl.BlockSpec((page_rows, HEAD_DIM), lambda bi, ci, pt: (pt[bi, ci * PAGES_PER_STEP + k], 0))
    return [spec(k) for k in range(PAGES_PER_STEP)]


def _compress_pages(cache_rows, page_table, page_size, wk, wv):
    nb, n_pages = page_table.shape
    per_page = page_size // CMP_STRIDE
    rows = PAGES_PER_STEP * per_page
    const3 = lambda bi, ci, pt: (0, 0, 0)
    grid_spec = pltpu.PrefetchScalarGridSpec(
        num_scalar_prefetch=1,
        grid=(nb, n_pages // PAGES_PER_STEP),
        in_specs=_page_specs(page_size * ROWS_PER_POS) + [
            pl.BlockSpec((CMP_STRIDE, HEAD_DIM, 2 * HEAD_DIM), const3),
            pl.BlockSpec((CMP_STRIDE, HEAD_DIM, 2 * HEAD_DIM), const3)],
        out_specs=[pl.BlockSpec((1, rows, 1024), lambda bi, ci, pt: (bi, ci, 0)),
                   pl.BlockSpec((1, PAGES_PER_STEP * page_size, 512), lambda bi, ci, pt: (bi, ci, 0))],
    )
    return pl.pallas_call(
        _compress_pages_kernel,
        grid_spec=grid_spec,
        out_shape=[jax.ShapeDtypeStruct((nb, n_pages * per_page, 1024), F32),
                   jax.ShapeDtypeStruct((nb, n_pages * page_size, 512), BF16)],
        compiler_params=_params("parallel", "parallel"),
        name="compress_pages",
    )(page_table, *([cache_rows] * PAGES_PER_STEP), wk, wv)


def _combine_compressed(y_ref, kcp_ref, tmp_ref):
    ns = y_ref.shape[1]
    nb = ns // CMP_PER_SEL
    last = lax.broadcasted_iota(jnp.int32, (ns, 1), 0) == ns - 1
    for a in range(2 * KV_HEADS):
        y1 = y_ref[0, :, a * 256:a * 256 + 128]
        y2 = y_ref[0, :, a * 256 + 128:(a + 1) * 256]
        nxt = jnp.where(last, 0.0, pltpu.roll(y2, ns - 1, 0))
        tmp_ref[...] = y1 + nxt
        for c in range(CMP_PER_SEL):
            kcp_ref[a * CMP_PER_SEL + c] = tmp_ref[pl.ds(c, nb, stride=CMP_PER_SEL), :].astype(BF16)


def _cmp_attention(q_rows, pos, kcp_ref, g):
    nb = kcp_ref.shape[1]
    blk = lax.broadcasted_iota(jnp.int32, (1, nb), 1)
    masks = [(SEL_BLOCK * blk + CMP_STRIDE * c + CMP_LEN - 1) <= pos for c in range(CMP_PER_SEL)]
    outs = []
    imp = None
    for qh in q_rows:
        s = [_dot_nt(qh, kcp_ref[g * CMP_PER_SEL + c]) * SCALE for c in range(CMP_PER_SEL)]
        p = _masked_softmax_parts(s, masks)
        o = functools.reduce(jnp.add, [
            _dot(p[c].astype(BF16), kcp_ref[(KV_HEADS + g) * CMP_PER_SEL + c]) for c in range(CMP_PER_SEL)])
        outs.append(o)
        ps = functools.reduce(jnp.add, p)
        imp = ps if imp is None else imp + ps
    return outs, imp


def _cmp_prompt_kernel(q_ref, y_ref, o_ref, sel_ref, kcp_ref, tmp_ref):
    i = pl.program_id(1)

    @pl.when(i == 0)
    def _():
        _combine_compressed(y_ref, kcp_ref, tmp_ref)

    nb = kcp_ref.shape[1]
    pos = i * Q_BLOCK + lax.broadcasted_iota(jnp.int32, (Q_BLOCK, 1), 0)
    blk = lax.broadcasted_iota(jnp.int32, (1, nb), 1)
    cur = pos // SEL_BLOCK
    forced = (blk == cur) | (blk == 0)
    causal = blk <= cur
    pos_rows = jnp.concatenate([pos] * GROUP, axis=0)
    for g in range(KV_HEADS):
        (o,), imp_rows = _cmp_attention([_stack_heads(q_ref, g)], pos_rows, kcp_ref, g)
        for r in range(GROUP):
            hcol = (g * GROUP + r) * HEAD_DIM
            o_ref[0, :, hcol:hcol + HEAD_DIM] = o[r * Q_BLOCK:(r + 1) * Q_BLOCK]
        imp = functools.reduce(jnp.add, [imp_rows[r * Q_BLOCK:(r + 1) * Q_BLOCK] for r in range(GROUP)])
        v = jnp.where(forced, FORCE_SCORE, jnp.where(causal, imp, -1.0))
        sel_t = _topk_select(v.T, SEL_TOPN, 0)
        sel = jnp.where(causal, sel_t.T, 0.0)
        sel_ref[0, :, g * nb:(g + 1) * nb] = sel.astype(BF16)


def _cmp_prompt(qq3, y):
    b, t, _ = qq3.shape
    ns = y.shape[1]
    nb = ns // CMP_PER_SEL
    return pl.pallas_call(
        _cmp_prompt_kernel,
        grid=(b, t // Q_BLOCK),
        in_specs=[pl.BlockSpec((1, Q_BLOCK, 1024), lambda bi, i: (bi, i, 0)),
                  pl.BlockSpec((1, ns, 1024), lambda bi, i: (bi, 0, 0))],
        out_specs=[pl.BlockSpec((1, Q_BLOCK, 1024), lambda bi, i: (bi, i, 0)),
                   pl.BlockSpec((1, Q_BLOCK, KV_HEADS * nb), lambda bi, i: (bi, i, 0))],
        out_shape=[jax.ShapeDtypeStruct((b, t, 1024), F32),
                   jax.ShapeDtypeStruct((b, t, KV_HEADS * nb), BF16)],
        scratch_shapes=[pltpu.VMEM((2 * KV_HEADS * CMP_PER_SEL, nb, HEAD_DIM), BF16),
                        pltpu.VMEM((ns, HEAD_DIM), F32)],
        compiler_params=_params("parallel", "arbitrary"),
        name="cmp_prompt",
    )(qq3, y)


SEL_CHUNK = 512


def _expand_blocks(sel, first_block, width):
    nb = sel.shape[1]
    b = lax.broadcasted_iota(jnp.int32, (nb, width), 0)
    t = lax.broadcasted_iota(jnp.int32, (nb, width), 1)
    e = jnp.where(b == first_block + t // SEL_BLOCK, 1.0, 0.0).astype(BF16)
    return _dot(sel, e)


def _sel_prompt_kernel(q_ref, sel_ref, kv_ref, o_ref):
    i = pl.program_id(1)
    nb = sel_ref.shape[2] // KV_HEADS
    pos = i * Q_BLOCK + lax.broadcasted_iota(jnp.int32, (Q_BLOCK, 1), 0)
    n_chunks = (i * Q_BLOCK + Q_BLOCK + SEL_CHUNK - 1) // SEL_CHUNK
    lane = lax.broadcasted_iota(jnp.int32, (1, SEL_CHUNK), 1)
    rows = GROUP * Q_BLOCK
    for g in range(KV_HEADS):
        q = jnp.concatenate(
            [q_ref[0, :, (g * GROUP + r) * HEAD_DIM:(g * GROUP + r + 1) * HEAD_DIM] for r in range(GROUP)], axis=0)
        sel = sel_ref[0, :, g * nb:(g + 1) * nb]

        def chunk(c, carry):
            m, l, acc = carry
            start = pl.multiple_of(c * SEL_CHUNK, SEL_CHUNK)
            k = kv_ref[0, pl.ds(start, SEL_CHUNK), g * HEAD_DIM:(g + 1) * HEAD_DIM]
            v = kv_ref[0, pl.ds(start, SEL_CHUNK), (KV_HEADS + g) * HEAD_DIM:(KV_HEADS + g + 1) * HEAD_DIM]
            picked = _expand_blocks(sel, c * (SEL_CHUNK // SEL_BLOCK), SEL_CHUNK)
            bias = jnp.where(picked > 0.5, jnp.where((start + lane) <= pos, 0.0, NEG_INF), NEG_INF)
            s = _dot_nt(q, k) * (SCALE * LOG2E) + jnp.concatenate([bias] * GROUP, axis=0)
            m_new = jnp.maximum(m, jnp.max(s, axis=-1, keepdims=True))
            m_safe = jnp.where(m_new == NEG_INF, 0.0, m_new)
            p = jnp.exp2(s - m_safe)
            alpha = jnp.exp2(m - m_safe)
            l = alpha * l + jnp.sum(p, axis=-1, keepdims=True)
            acc = alpha * acc + _dot(p.astype(BF16), v)
            return m_new, l, acc

        def pair(cc, carry):
            return chunk(2 * cc + 1, chunk(2 * cc, carry))

        init = (jnp.full((rows, 1), NEG_INF, F32), jnp.zeros((rows, 1), F32), jnp.zeros((rows, HEAD_DIM), F32))
        state = lax.fori_loop(0, n_chunks // 2, pair, init)
        m, l, acc = lax.cond(n_chunks % 2 == 1, lambda st: chunk(n_chunks - 1, st), lambda st: st, state)
        o = acc * (1.0 / jnp.maximum(l, 1e-30))
        for r in range(GROUP):
            hcol = (g * GROUP + r) * HEAD_DIM
            o_ref[0, :, hcol:hcol + HEAD_DIM] = o[r * Q_BLOCK:(r + 1) * Q_BLOCK]


def _sel_prompt(qq3, sel, kvb3):
    b, t, _ = qq3.shape
    nb2 = sel.shape[2]
    return pl.pallas_call(
        _sel_prompt_kernel,
        grid=(b, t // Q_BLOCK),
        in_specs=[pl.BlockSpec((1, Q_BLOCK, 1024), lambda bi, i: (bi, i, 1)),
                  pl.BlockSpec((1, Q_BLOCK, nb2), lambda bi, i: (bi, i, 0)),
                  pl.BlockSpec((1, t, 512), lambda bi, i: (bi, 0, 0))],
        out_specs=pl.BlockSpec((1, Q_BLOCK, 1024), lambda bi, i: (bi, i, 0)),
        out_shape=jax.ShapeDtypeStruct((b, t, 1024), F32),
        compiler_params=_params("parallel", "arbitrary"),
        name="sel_prompt",
    )(qq3, sel, kvb3)


def _win_prompt_kernel(q_ref, kv_ref, o_ref, *, span):
    i = pl.program_id(1)
    pos = i * Q_BLOCK + lax.broadcasted_iota(jnp.int32, (Q_BLOCK, 1), 0)
    pos = jnp.concatenate([pos] * GROUP, axis=0)
    start = pl.multiple_of(jnp.maximum(i * Q_BLOCK + Q_BLOCK - span, 0), Q_BLOCK)
    kpos = start + lax.broadcasted_iota(jnp.int32, (1, span), 1)
    diff = pos - kpos
    ok = (diff >= 0) & (diff < WINDOW)
    for g in range(KV_HEADS):
        q = jnp.concatenate(
            [q_ref[0, :, (g * GROUP + r) * HEAD_DIM:(g * GROUP + r + 1) * HEAD_DIM] for r in range(GROUP)], axis=0)
        k = kv_ref[0, pl.ds(start, span), g * HEAD_DIM:(g + 1) * HEAD_DIM]
        v = kv_ref[0, pl.ds(start, span), (KV_HEADS + g) * HEAD_DIM:(KV_HEADS + g + 1) * HEAD_DIM]
        s = _dot_nt(q, k) * SCALE
        (p,) = _masked_softmax_parts([s], [ok])
        o = _dot(p.astype(BF16), v)
        for r in range(GROUP):
            hcol = (g * GROUP + r) * HEAD_DIM
            o_ref[0, :, hcol:hcol + HEAD_DIM] = o[r * Q_BLOCK:(r + 1) * Q_BLOCK]


def _win_prompt(qq3, kvb3):
    b, t, _ = qq3.shape
    span = min(WINDOW + Q_BLOCK, t)
    return pl.pallas_call(
        functools.partial(_win_prompt_kernel, span=span),
        grid=(b, t // Q_BLOCK),
        in_specs=[pl.BlockSpec((1, Q_BLOCK, 1024), lambda bi, i: (bi, i, 1)),
                  pl.BlockSpec((1, t, 512), lambda bi, i: (bi, 0, 1))],
        out_specs=pl.BlockSpec((1, Q_BLOCK, 1024), lambda bi, i: (bi, i, 0)),
        out_shape=jax.ShapeDtypeStruct((b, t, 1024), F32),
        compiler_params=_params("parallel", "arbitrary"),
        name="win_prompt",
    )(qq3, kvb3)


def _pool_kernel(prev_ref, u_ref, w_ref, sc_ref, o_ref, *, base, zero_first_prev):
    i = pl.program_id(1)
    tq = u_ref.shape[1]
    cur = u_ref[0]
    prev = prev_ref[0]
    if zero_first_prev:
        prev = jnp.where(i == 0, 0.0, prev)
    ext = jnp.concatenate([prev, cur], axis=0)
    gpos = base + i * tq + lax.broadcasted_iota(jnp.int32, (tq, 1), 0)
    gw = ext.shape[1] // len(POOL_WINDOWS)
    for gi, w in enumerate(POOL_WINDOWS):
        s = ext[:, gi * gw:(gi + 1) * gw]
        span = 1
        while span < w:
            s = s + pltpu.roll(s, span, 0)
            span *= 2
        cnt = jnp.minimum(gpos + 1, w).astype(F32)
        d = s[POOL_PREV:] / cnt - cur[:, gi * gw:(gi + 1) * gw]
        o = _dot(d.astype(BF16), w_ref[gi]) * sc_ref[:, gi * gw:(gi + 1) * gw]
        o_ref[0, :, gi * gw:(gi + 1) * gw] = o


def _pool(prev, u3, w, scale, *, base, zero_first_prev, tq=512):
    b, t, c = u3.shape
    tq = min(tq, t)
    ratio = tq // POOL_PREV
    if zero_first_prev:
        prev_map = lambda bi, i: (bi, jnp.maximum(i * ratio - 1, 0), 0)
    else:
        prev_map = lambda bi, i: (bi, 0, 0)
    ng = len(POOL_WINDOWS)
    return pl.pallas_call(
        functools.partial(_pool_kernel, base=base, zero_first_prev=zero_first_prev),
        grid=(b, t // tq),
        in_specs=[pl.BlockSpec((1, POOL_PREV, c), prev_map),
                  pl.BlockSpec((1, tq, c), lambda bi, i: (bi, i, 0)),
                  pl.BlockSpec((ng, c // ng, c // ng), lambda bi, i: (0, 0, 0)),
                  pl.BlockSpec((1, c), lambda bi, i: (0, 0))],
        out_specs=pl.BlockSpec((1, tq, c), lambda bi, i: (bi, i, 0)),
        out_shape=jax.ShapeDtypeStruct((b, t, c), F32),
        compiler_params=_params("parallel", "parallel"),
        name="pool_mix",
    )(prev, u3, w, scale)


def _mix_kernel(oc_ref, os_ref, ow_ref, gt_ref, po_ref, ga_ref, gb_ref, wa_ref, wp_ref, mix_ref):
    gt = gt_ref[...]
    cols = []
    for h in range(N_HEADS):
        sl = slice(h * HEAD_DIM, (h + 1) * HEAD_DIM)
        o = (gt[:, h:h + 1] * oc_ref[:, sl] + gt[:, N_HEADS + h:N_HEADS + h + 1] * os_ref[:, sl]
             + gt[:, 2 * N_HEADS + h:2 * N_HEADS + h + 1] * ow_ref[:, sl])
        cols.append(o.astype(BF16))
    a = _dot(jnp.concatenate(cols, axis=1), wa_ref[...])
    p = _dot(po_ref[...].astype(BF16), wp_ref[...])
    mix_ref[...] = (ga_ref[...] * a + gb_ref[...] * p).astype(BF16)


def _mix(oc, os_, ow, gt, po, gab, wa, wp, *, tm=ROW_TILE):
    m = oc.shape[0]
    d = wa.shape[1]
    tm = min(tm, m)
    row = lambda i: (i, 0)
    const = lambda i: (0, 0)
    return pl.pallas_call(
        _mix_kernel,
        grid=(m // tm,),
        in_specs=[pl.BlockSpec((tm, 1024), row), pl.BlockSpec((tm, 1024), row), pl.BlockSpec((tm, 1024), row),
                  pl.BlockSpec((tm, 128), row), pl.BlockSpec((tm, 1024), row),
                  pl.BlockSpec((tm, d), lambda i: (i, 0)), pl.BlockSpec((tm, d), lambda i: (i, 1)),
                  pl.BlockSpec((1024, d), const), pl.BlockSpec((1024, d), const)],
        out_specs=pl.BlockSpec((tm, d), row),
        out_shape=jax.ShapeDtypeStruct((m, d), BF16),
        compiler_params=_params("parallel"),
        name="branch_mix",
    )(oc, os_, ow, gt, po, gab, gab, wa, wp)


def _out_kernel(x_ref, mix_ref, w_ref, o_ref):
    o_ref[...] = x_ref[...] + _dot(mix_ref[...], w_ref[...])


def _out_proj(x, mix, w, *, tm=ROW_TILE):
    m, d = x.shape
    tm = min(tm, m)
    return pl.pallas_call(
        _out_kernel,
        grid=(m // tm,),
        in_specs=[pl.BlockSpec((tm, d), lambda i: (i, 0)), pl.BlockSpec((tm, d), lambda i: (i, 0)),
                  pl.BlockSpec((d, d), lambda i: (0, 0))],
        out_specs=pl.BlockSpec((tm, d), lambda i: (i, 0)),
        out_shape=jax.ShapeDtypeStruct((m, d), F32),
        compiler_params=_params("parallel"),
        name="out_proj",
    )(x, mix, w)


def _pair_candidates(sv, si):
    k = PEER_TOPK
    row = lax.broadcasted_iota(jnp.int32, (8, 1), 0)
    vals, ids = [], []
    for a in range(k // 2):
        lim = k // (a + 1)
        nrow = k if a == 0 else 8
        v = sv[0][a:a + 1] + sv[1][:nrow]
        if lim < nrow:
            v = jnp.where(row < lim, v, NEG_INF)
        vals.append(v)
        ids.append(si[0][a:a + 1] * float(PEER_NKEYS) + si[1][:nrow])
    vals.append(sv[0][k // 2:] + sv[1][0:1])
    ids.append(si[0][k // 2:] * float(PEER_NKEYS) + si[1][0:1])
    return jnp.concatenate(vals, axis=0), jnp.concatenate(ids, axis=0)


def _peer_score_kernel(x_ref, g_ref, wq_ref, keys_ref, ids_ref, gw_ref):
    h = _rms(x_ref[...], g_ref[...]).astype(BF16)
    q = _dot(h, wq_ref[...]).astype(BF16)
    dk = PEER_NKEYS
    ids, gws = [], []
    for hd in range(PEER_HEADS):
        sv, si = [], []
        for c in range(2):
            qhc = q[:, (hd * 2 + c) * dk:(hd * 2 + c + 1) * dk]
            st = _dot_nt(keys_ref[hd * 2 + c], qhc)
            v, ix = _topk_sorted(st, PEER_TOPK)
            sv.append(v)
            si.append(ix)
        comb, eid = _pair_candidates(sv, si)
        cv, ce = _topk_sorted(comb, PEER_TOPK, payload=eid)
        e = jnp.exp(cv - cv[0:1])
        gws.append(e / jnp.sum(e, axis=0, keepdims=True))
        ids.append(ce)
    ids_ref[...] = jnp.concatenate(ids, axis=0).T.astype(jnp.int32)
    gw_ref[...] = jnp.concatenate(gws, axis=0)


def _peer_score(x, g, wq, keys, *, tb=128):
    m, d = x.shape
    nk = PEER_HEADS * PEER_TOPK
    return pl.pallas_call(
        _peer_score_kernel,
        grid=(m // tb,),
        in_specs=[pl.BlockSpec((tb, d), lambda i: (i, 0)), pl.BlockSpec((1, d), lambda i: (0, 0)),
                  pl.BlockSpec(wq.shape, lambda i: (0, 0)),
                  pl.BlockSpec(keys.shape, lambda i: (0, 0, 0))],
        out_specs=[pl.BlockSpec((tb, nk), lambda i: (i, 0)), pl.BlockSpec((nk, tb), lambda i: (0, i))],
        out_shape=[jax.ShapeDtypeStruct((m, nk), jnp.int32), jax.ShapeDtypeStruct((nk, m), F32)],
        compiler_params=_params("parallel"),
        name="peer_score",
    )(x, g, wq, keys)


def _gelu(x):
    return 0.5 * x * (1.0 + lax.erf(x * (2.0 ** -0.5)))


PACK_ROWS = 256


def _pack_kernel(u_ref, v_ref, o_ref):
    lo = pltpu.bitcast(u_ref[...].astype(BF16).astype(F32), jnp.uint32) >> 16
    hi = pltpu.bitcast(v_ref[...].astype(BF16).astype(F32), jnp.uint32) & jnp.uint32(0xFFFF0000)
    o_ref[...] = (lo | hi).reshape(o_ref.shape)


def _pack_tables(u, v):
    e, d = u.shape
    return pl.pallas_call(
        _pack_kernel,
        grid=(e // PACK_ROWS,),
        in_specs=[pl.BlockSpec((PACK_ROWS, d), lambda i: (i, 0)), pl.BlockSpec((PACK_ROWS, d), lambda i: (i, 0))],
        out_specs=pl.BlockSpec((PACK_ROWS, 1, d), lambda i: (i, 0, 0)),
        out_shape=jax.ShapeDtypeStruct((e, 1, d), jnp.uint32),
        compiler_params=_params("parallel"),
        name="peer_pack",
    )(u, v)


PEER_RING = 8


def _peer_apply_kernel(ids_ref, nxt_ref, x_ref, gffn_ref, gw_ref, gfin_ref, tab_ref, y_ref, *scratch):
    bufs = scratch[:PEER_RING]
    sem, hbuf, obuf = scratch[PEER_RING:]
    step = pl.program_id(0)
    tb, d = x_ref.shape
    nk = gw_ref.shape[0]
    nchunk = d // 128
    ahead = PEER_RING - 1
    n_groups = tb // PEER_RING
    hbuf[...] = _rms(x_ref[...], gffn_ref[...])

    def row_copy(ids, t, k, slot):
        return pltpu.make_async_copy(tab_ref.at[ids[t, k]], bufs[slot].at[pl.ds(k, 1), :], sem.at[slot])

    def issue(ids, t, slot):
        for k in range(nk):
            row_copy(ids, t, k, slot).start(priority=k % 2)

    def drain(slot):
        for k in range(nk):
            row_copy(ids_ref, 0, k, slot).wait()

    @pl.when(step == 0)
    def _():
        for s in range(ahead):
            issue(ids_ref, s, s)

    lane = lax.broadcasted_iota(jnp.int32, (nk, tb), 1)

    def compute(t, slot):
        buf = bufs[slot]
        hrow = hbuf[pl.ds(t, 1), :]
        acc = jnp.zeros((nk, 128), F32)
        for j in range(nchunk):
            w = buf[:, j * 128:(j + 1) * 128]
            u = pltpu.bitcast(w << 16, F32)
            acc = acc + u * hrow[:, j * 128:(j + 1) * 128]
        act = jnp.sum(acc, axis=1, keepdims=True)
        gcol = jnp.sum(jnp.where(lane == t, gw_ref[...], 0.0), axis=1, keepdims=True)
        coef = _gelu(act) * gcol
        outs = []
        for j in range(nchunk):
            w = buf[:, j * 128:(j + 1) * 128]
            v = pltpu.bitcast(w & jnp.uint32(0xFFFF0000), F32)
            outs.append(jnp.sum(v * coef, axis=0, keepdims=True))
        obuf[pl.ds(t, 1), :] = jnp.concatenate(outs, axis=1)

    def group(p, last):
        for s in range(PEER_RING):
            t = p * PEER_RING + s
            drain(s)
            if last and s > 0:
                issue(nxt_ref, s - 1, (s + ahead) % PEER_RING)
            else:
                issue(ids_ref, t + ahead, (s + ahead) % PEER_RING)
            compute(t, s)

    def body(p, carry):
        group(p, False)
        return carry

    lax.fori_loop(0, n_groups - 1, body, 0)
    group(n_groups - 1, True)

    @pl.when(step == pl.num_programs(0) - 1)
    def _():
        for s in range(ahead):
            drain(s)

    y_ref[...] = _rms(x_ref[...] + obuf[...], gfin_ref[...])


def _peer_apply(ids, x, gffn, gw, gfin, table, *, tb=128):
    m, d = x.shape
    nk = ids.shape[1]
    steps = m // tb
    return pl.pallas_call(
        _peer_apply_kernel,
        grid=(steps,),
        in_specs=[pl.BlockSpec((tb, nk), lambda i: (i, 0), memory_space=pltpu.SMEM),
                  pl.BlockSpec((tb, nk), lambda i: (jnp.minimum(i + 1, steps - 1), 0), memory_space=pltpu.SMEM),
                  pl.BlockSpec((tb, d), lambda i: (i, 0)),
                  pl.BlockSpec((1, d), lambda i: (0, 0)),
                  pl.BlockSpec((nk, tb), lambda i: (0, i)),
                  pl.BlockSpec((1, d), lambda i: (0, 0)),
                  pl.BlockSpec(memory_space=pl.ANY)],
        out_specs=pl.BlockSpec((tb, d), lambda i: (i, 0)),
        out_shape=jax.ShapeDtypeStruct((m, d), F32),
        scratch_shapes=[pltpu.VMEM((nk, d), jnp.uint32)] * PEER_RING + [
            pltpu.SemaphoreType.DMA((PEER_RING,)), pltpu.VMEM((tb, d), F32), pltpu.VMEM((tb, d), F32)],
        compiler_params=_params("arbitrary"),
        name="peer_apply",
    )(ids, ids, x, gffn, gw, gfin, table)


def _stack_heads(q_ref, g):
    return jnp.concatenate(
        [q_ref[0, :, (g * GROUP + r) * HEAD_DIM:(g * GROUP + r + 1) * HEAD_DIM] for r in range(GROUP)], axis=0)


NEW_PAD = 128


def _pad_new(x):
    return jnp.concatenate([x, jnp.zeros((NEW_PAD - x.shape[0], x.shape[1]), F32)], axis=0).astype(BF16)


def _cmp_sample_kernel(q_ref, y_ref, o_ref, sel_ref, kcp_ref, tmp_ref, *, past):
    _combine_compressed(y_ref, kcp_ref, tmp_ref)
    t = q_ref.shape[1]
    nb = kcp_ref.shape[1]
    tpos = past + lax.broadcasted_iota(jnp.int32, (t, 1), 0)
    pos = jnp.concatenate([tpos] * GROUP, axis=0)
    blk = lax.broadcasted_iota(jnp.int32, (1, nb), 1)
    for g in range(KV_HEADS):
        q = _stack_heads(q_ref, g)
        outs, imp = _cmp_attention([q], pos, kcp_ref, g)
        o = outs[0]
        for r in range(GROUP):
            hcol = (g * GROUP + r) * HEAD_DIM
            o_ref[0, :, hcol:hcol + HEAD_DIM] = o[r * t:(r + 1) * t]
        imp_t = functools.reduce(jnp.add, [imp[r * t:(r + 1) * t] for r in range(GROUP)])
        v = jnp.where(blk == 0, FORCE_SCORE, imp_t)
        sel = _topk_select(v, SEL_TOPN - 1, 1)
        sel_ref[0, :, g * nb:(g + 1) * nb] = sel.astype(BF16)


def _cmp_sample(qq3, y, *, past):
    b, t, _ = qq3.shape
    ns = y.shape[1]
    nb = ns // CMP_PER_SEL
    return pl.pallas_call(
        functools.partial(_cmp_sample_kernel, past=past),
        grid=(b,),
        in_specs=[pl.BlockSpec((1, t, 1024), lambda bi: (bi, 0, 0)),
                  pl.BlockSpec((1, ns, 1024), lambda bi: (bi, 0, 0))],
        out_specs=[pl.BlockSpec((1, t, 1024), lambda bi: (bi, 0, 0)),
                   pl.BlockSpec((1, t, KV_HEADS * nb), lambda bi: (bi, 0, 0))],
        out_shape=[jax.ShapeDtypeStruct((b, t, 1024), F32),
                   jax.ShapeDtypeStruct((b, t, KV_HEADS * nb), BF16)],
        scratch_shapes=[pltpu.VMEM((2 * KV_HEADS * CMP_PER_SEL, nb, HEAD_DIM), BF16),
                        pltpu.VMEM((ns, HEAD_DIM), F32)],
        compiler_params=_params("parallel"),
        name="cmp_sample",
    )(qq3, y)


def _sel_sample_kernel(q_ref, sel_ref, kv_ref, new_ref, o_ref, m_ref, l_ref, acc_ref):
    c = pl.program_id(1)
    nc = pl.num_programs(1)
    t = q_ref.shape[1]
    nb = sel_ref.shape[2] // KV_HEADS
    width = kv_ref.shape[1]

    @pl.when(c == 0)
    def _():
        m_ref[...] = jnp.full(m_ref.shape, NEG_INF, F32)
        l_ref[...] = jnp.zeros(l_ref.shape, F32)
        acc_ref[...] = jnp.zeros(acc_ref.shape, F32)

    def update(g, s, ok, v):
        s = jnp.where(ok, s, NEG_INF)
        m = m_ref[g]
        m_new = jnp.maximum(m, jnp.max(s, axis=-1, keepdims=True))
        m_safe = jnp.where(m_new == NEG_INF, 0.0, m_new)
        p = jnp.exp(s - m_safe)
        alpha = jnp.exp(m - m_safe)
        l_ref[g] = alpha * l_ref[g] + jnp.sum(p, axis=-1, keepdims=True)
        acc_ref[g] = alpha * acc_ref[g] + _dot(p.astype(BF16), v)
        m_ref[g] = m_new

    for g in range(KV_HEADS):
        q = _stack_heads(q_ref, g)
        k = kv_ref[0, :, g * HEAD_DIM:(g + 1) * HEAD_DIM]
        v = kv_ref[0, :, (KV_HEADS + g) * HEAD_DIM:(KV_HEADS + g + 1) * HEAD_DIM]
        s = _dot_nt(q, k) * SCALE
        picked = _expand_blocks(sel_ref[0, :, g * nb:(g + 1) * nb], c * (width // SEL_BLOCK), width)
        picked = jnp.concatenate([picked] * GROUP, axis=0)
        update(g, s, picked > 0.5, v)

    @pl.when(c == nc - 1)
    def _():
        qi = lax.broadcasted_iota(jnp.int32, (t, 1), 0)
        qi = jnp.concatenate([qi] * GROUP, axis=0)
        ok = lax.broadcasted_iota(jnp.int32, (1, NEW_PAD), 1) <= qi
        for g in range(KV_HEADS):
            q = _stack_heads(q_ref, g)
            k = _pad_new(new_ref[pl.ds(2 * KV_HEADS + g, t, stride=ROWS_PER_POS), :])
            v = _pad_new(new_ref[pl.ds(3 * KV_HEADS + g, t, stride=ROWS_PER_POS), :])
            update(g, _dot_nt(q, k) * SCALE, ok, v)
            o = acc_ref[g] * (1.0 / jnp.maximum(l_ref[g], 1e-30))
            for r in range(GROUP):
                hcol = (g * GROUP + r) * HEAD_DIM
                o_ref[0, :, hcol:hcol + HEAD_DIM] = o[r * t:(r + 1) * t]


SEL_SAMPLE_CHUNK = 8192


def _sel_sample(kvs, qq3, sel, nsa_rows):
    nbatch, past, _ = kvs.shape
    t = qq3.shape[1]
    nb2 = sel.shape[2]
    rows = GROUP * t
    width = min(SEL_SAMPLE_CHUNK, past)
    return pl.pallas_call(
        _sel_sample_kernel,
        grid=(nbatch, past // width),
        in_specs=[pl.BlockSpec((1, t, 1024), lambda bi, ci: (bi, 0, 1)),
                  pl.BlockSpec((1, t, nb2), lambda bi, ci: (bi, 0, 0)),
                  pl.BlockSpec((1, width, 512), lambda bi, ci: (bi, ci, 0)),
                  pl.BlockSpec((t * ROWS_PER_POS, HEAD_DIM), lambda bi, ci: (bi, 0))],
        out_specs=pl.BlockSpec((1, t, 1024), lambda bi, ci: (bi, 0, 0)),
        out_shape=jax.ShapeDtypeStruct((nbatch, t, 1024), F32),
        scratch_shapes=[pltpu.VMEM((KV_HEADS, rows, 1), F32), pltpu.VMEM((KV_HEADS, rows, 1), F32),
                        pltpu.VMEM((KV_HEADS, rows, HEAD_DIM), F32)],
        compiler_params=_params("parallel", "arbitrary"),
        name="sel_sample",
    )(qq3, sel, kvs, nsa_rows)


def _win_sample_kernel(q_ref, st_ref, new_ref, o_ref):
    t = q_ref.shape[1]
    nw = st_ref.shape[0] // WIN_ROWS_PER_POS
    old = lambda comp: st_ref[pl.ds(comp, nw, stride=WIN_ROWS_PER_POS), :].astype(BF16)
    new = lambda comp: _pad_new(new_ref[pl.ds(comp, t, stride=WIN_ROWS_PER_POS), :])
    qi = lax.broadcasted_iota(jnp.int32, (t, 1), 0)
    qi = jnp.concatenate([qi] * GROUP, axis=0)
    d_old = (nw + qi) - lax.broadcasted_iota(jnp.int32, (1, nw), 1)
    ok_old = (d_old >= 0) & (d_old < WINDOW)
    d_new = qi - lax.broadcasted_iota(jnp.int32, (1, NEW_PAD), 1)
    ok_new = (d_new >= 0) & (d_new < WINDOW)
    for g in range(KV_HEADS):
        q = _stack_heads(q_ref, g)
        k_old, v_old = old(g), old(KV_HEADS + g)
        k_new, v_new = new(g), new(KV_HEADS + g)
        p_old, p_new = _masked_softmax_parts(
            [_dot_nt(q, k_old) * SCALE, _dot_nt(q, k_new) * SCALE], [ok_old, ok_new])
        o = _dot(p_old.astype(BF16), v_old) + _dot(p_new.astype(BF16), v_new)
        for r in range(GROUP):
            hcol = (g * GROUP + r) * HEAD_DIM
            o_ref[0, :, hcol:hcol + HEAD_DIM] = o[r * t:(r + 1) * t]


def _win_sample(qq3, state_rows, win_rows):
    b, t, _ = qq3.shape
    nw = state_rows.shape[0] // (b * WIN_ROWS_PER_POS)
    return pl.pallas_call(
        _win_sample_kernel,
        grid=(b,),
        in_specs=[pl.BlockSpec((1, t, 1024), lambda bi: (bi, 0, 1)),
                  pl.BlockSpec((nw * WIN_ROWS_PER_POS, HEAD_DIM), lambda bi: (bi, 0)),
                  pl.BlockSpec((t * WIN_ROWS_PER_POS, HEAD_DIM), lambda bi: (bi, 0))],
        out_specs=pl.BlockSpec((1, t, 1024), lambda bi: (bi, 0, 0)),
        out_shape=jax.ShapeDtypeStruct((b, t, 1024), F32),
        compiler_params=_params("parallel"),
        name="win_sample",
    )(qq3, state_rows, win_rows)


def _rope_tables(pos):
    half = HEAD_DIM // 2
    inv = ROPE_THETA ** (-jnp.arange(half, dtype=F32) / half)
    ang = pos.astype(F32)[:, None] * inv[None, :]
    cos, sin = jnp.cos(ang), jnp.sin(ang)
    return jnp.concatenate([cos, cos], axis=-1), jnp.concatenate([-sin, sin], axis=-1)


def _prep_weights(w_in, w_phi_k, w_phi_v, w_pool_group, pool_scale, w_branch_attn, w_branch_pool, w_out,
                  peer_w_query, peer_sub_keys, peer_u, peer_v):
    d = w_in.shape[0]
    qw = N_HEADS * HEAD_DIM
    kvw = 6 * KV_HEADS * HEAD_DIM
    ngw = 3 * N_HEADS
    pw = d // 2
    o1, o2, o3, o4 = qw, qw + kvw, qw + kvw + ngw, qw + kvw + ngw + pw
    wb = w_in.astype(BF16)
    w_qkv = jnp.concatenate([wb[:, :o2], wb[:, o2:o3], jnp.zeros((d, HEAD_DIM - ngw), BF16)], axis=1)
    cat = lambda w: jnp.concatenate([w[:CMP_STRIDE], w[CMP_STRIDE:]], axis=-1).astype(BF16)
    return dict(
        w_qkv=w_qkv, w_u=wb[:, o3:o4], w_gab=wb[:, o4:],
        wk=cat(w_phi_k), wv=cat(w_phi_v),
        w_pool=w_pool_group.astype(BF16), pool_scale=pool_scale.reshape(1, -1),
        w_ba=w_branch_attn.astype(BF16), w_bp=w_branch_pool.astype(BF16), w_out=w_out.astype(BF16),
        w_query=peer_w_query.astype(BF16),
        keys=peer_sub_keys.reshape(PEER_HEADS * 2, PEER_NKEYS, -1).astype(BF16),
        table=_pack_tables(peer_u, peer_v),
    )


def _token_tail(x2d, o_cmp, o_sel, o_win, gates, pool_out, gab, wp, g_ffn, g_final):
    mix = _mix(o_cmp, o_sel, o_win, gates, pool_out, gab, wp["w_ba"], wp["w_bp"])
    x2 = _out_proj(x2d, mix, wp["w_out"])
    ids, gw = _peer_score(x2, g_ffn, wp["w_query"], wp["keys"])
    return _peer_apply(ids, x2, g_ffn, gw, g_final, wp["table"])


def kernel(x_prompt, x_sample, cache_kv_nsa, state_win_kv, state_pool, page_table, g_norm_mix, w_in, w_phi_k,
           w_phi_v, w_pool_group, pool_scale, w_branch_attn, w_branch_pool, w_out, g_norm_ffn, peer_w_query,
           peer_sub_keys, peer_u, peer_v, g_norm_final):
    assert g_norm_mix.shape[0] == 1, "single-layer step"
    bp, t, d = x_prompt.shape
    bs, ts, _ = x_sample.shape
    n_pages = page_table.shape[1]
    page_size = cache_kv_nsa.shape[2]
    past = n_pages * page_size
    wp = _prep_weights(w_in[0], w_phi_k[0], w_phi_v[0], w_pool_group[0], pool_scale[0], w_branch_attn[0],
                       w_branch_pool[0], w_out[0], peer_w_query[0], peer_sub_keys[0], peer_u[0], peer_v[0])
    g_mix = g_norm_mix[0].reshape(1, d)
    g_ffn = g_norm_ffn[0].reshape(1, d)
    g_fin = g_norm_final.reshape(1, d)

    xp = x_prompt.reshape(bp * t, d)
    cos, sin = _rope_tables(jnp.arange(t))
    cos, sin = jnp.tile(cos, (bp, 1)), jnp.tile(sin, (bp, 1))
    qq, nsa, win, gates, kvb, hn = _qkv_proj(xp, g_mix, wp["w_qkv"], cos, sin)
    u = _proj(hn, wp["w_u"])
    gab = _proj(hn, wp["w_gab"], act="sigmoid", out_dtype=BF16)
    qq3, kvb3 = qq.reshape(bp, t, -1), kvb.reshape(bp, t, -1)
    y = _compress_prompt(nsa, bp, wp["wk"], wp["wv"])
    o_cmp, sel = _cmp_prompt(qq3, y)
    o_sel = _sel_prompt(qq3, sel, kvb3)
    o_win = _win_prompt(qq3, kvb3)
    u3 = u.reshape(bp, t, -1)
    pool_out = _pool(u3, u3, wp["w_pool"], wp["pool_scale"], base=0, zero_first_prev=True)
    y_prompt = _token_tail(xp, o_cmp.reshape(bp * t, -1), o_sel.reshape(bp * t, -1), o_win.reshape(bp * t, -1),
                           gates, pool_out.reshape(bp * t, -1), gab, wp, g_ffn, g_fin)
    wlen = min(WINDOW, t)
    new_kv_p = nsa.reshape(1, bp, t, 4, KV_HEADS, HEAD_DIM)
    new_win_p = win.reshape(bp, t, 2, KV_HEADS, HEAD_DIM)[None, :, t - wlen:]
    new_pool_p = u3[None, :, t - (POOL_PREV - 1):]

    xs = x_sample.reshape(bs * ts, d)
    cos_s, sin_s = _rope_tables(past + jnp.arange(ts))
    cos_s, sin_s = jnp.tile(cos_s, (bs, 1)), jnp.tile(sin_s, (bs, 1))
    qq_s, nsa_s, win_s, gates_s, _, hn_s = _qkv_proj(xs, g_mix, wp["w_qkv"], cos_s, sin_s)
    u_s = _proj(hn_s, wp["w_u"])
    gab_s = _proj(hn_s, wp["w_gab"], act="sigmoid", out_dtype=BF16)
    qq_s3 = qq_s.reshape(bs, ts, -1)
    cache_rows = cache_kv_nsa.reshape(-1, HEAD_DIM)
    y_s, kvs = _compress_pages(cache_rows, page_table, page_size, wp["wk"], wp["wv"])
    o_cmp_s, sel_s = _cmp_sample(qq_s3, y_s, past=past)
    o_sel_s = _sel_sample(kvs, qq_s3, sel_s, nsa_s)
    o_win_s = _win_sample(qq_s3, state_win_kv.reshape(-1, HEAD_DIM), win_s)
    u_s3 = u_s.reshape(bs, ts, -1)
    st_pool = state_pool[0]
    prev = jnp.pad(st_pool, ((0, 0), (POOL_PREV - st_pool.shape[1], 0), (0, 0)))
    pool_out_s = _pool(prev, u_s3, wp["w_pool"], wp["pool_scale"], base=st_pool.shape[1], zero_first_prev=False)
    y_sample = _token_tail(xs, o_cmp_s.reshape(bs * ts, -1), o_sel_s.reshape(bs * ts, -1),
                           o_win_s.reshape(bs * ts, -1), gates_s, pool_out_s.reshape(bs * ts, -1), gab_s, wp,
                           g_ffn, g_fin)
    new_kv_s = nsa_s.reshape(1, bs, ts, 4, KV_HEADS, HEAD_DIM)
    win_ext = jnp.concatenate([state_win_kv, win_s.reshape(1, bs, ts, 2, KV_HEADS, HEAD_DIM)], axis=2)
    new_win_s = win_ext[:, :, win_ext.shape[2] - min(WINDOW, win_ext.shape[2]):]
    pool_ext = jnp.concatenate([st_pool, u_s3], axis=1)
    new_pool_s = pool_ext[None, :, pool_ext.shape[1] - (POOL_PREV - 1):]

    return (y_prompt.reshape(bp, t, d), y_sample.reshape(bs, ts, d), new_kv_p, new_kv_s, new_win_p, new_win_s,
            new_pool_p, new_pool_s)
```

```python
import functools

import jax
import jax.numpy as jnp
from jax import lax
from jax.experimental import pallas as pl
from jax.experimental.pallas import tpu as pltpu

F32 = jnp.float32
BF16 = jnp.bfloat16

HEAD_DIM = 128
N_HEADS = 8
KV_HEADS = 2
GROUP = N_HEADS // KV_HEADS
CMP_LEN = 32
CMP_STRIDE = 16
SEL_BLOCK = 64
SEL_TOPN = 16
CMP_PER_SEL = SEL_BLOCK // CMP_STRIDE
WINDOW = 512
Q_BLOCK = 128
ROPE_THETA = 10000.0
FORCE_SCORE = 1e4
POOL_WINDOWS = (2, 4, 8, 16)
POOL_PREV = 16
PEER_HEADS = 8
PEER_NKEYS = 128
PEER_TOPK = 16
EPS = 1e-6
SCALE = HEAD_DIM ** -0.5
LOG2E = 1.4426950408889634
NEG_INF = float("-inf")

ROWS_PER_POS = 4 * KV_HEADS
WIN_ROWS_PER_POS = 2 * KV_HEADS

ROW_TILE = 512
VMEM_LIMIT = 56 * 1024 * 1024


def _params(*sem):
    return pltpu.CompilerParams(dimension_semantics=sem, vmem_limit_bytes=VMEM_LIMIT)


def _dot(a, b):
    return jnp.dot(a, b, preferred_element_type=F32)


def _dot_nt(a, b):
    return lax.dot_general(a, b, (((1,), (1,)), ((), ())), preferred_element_type=F32)


def _rms(x, g):
    return x * lax.rsqrt(jnp.mean(x * x, axis=-1, keepdims=True) + EPS) * g


def _masked_softmax_parts(parts, masks):
    parts = [jnp.where(mk, s, NEG_INF) for s, mk in zip(parts, masks)]
    same = all(s.shape == parts[0].shape for s in parts)

    def across(op, red, xs):
        if same:
            return red(functools.reduce(op, xs), axis=-1, keepdims=True)
        return functools.reduce(op, [red(x, axis=-1, keepdims=True) for x in xs])

    m = across(jnp.maximum, jnp.max, parts)
    m = jnp.where(m == NEG_INF, 0.0, m)
    es = [jnp.exp(s - m) for s in parts]
    den = across(jnp.add, jnp.sum, es)
    inv = 1.0 / jnp.maximum(den, 1e-30)
    return [e * inv for e in es]


def _topk_select(v, n, axis):
    size = v.shape[axis]
    idx = lax.broadcasted_iota(jnp.int32, v.shape, axis).astype(F32)
    sel = jnp.zeros(v.shape, F32)
    for _ in range(n):
        m = jnp.max(v, axis=axis, keepdims=True)
        first = jnp.min(jnp.where(v == m, idx, float(size)), axis=axis, keepdims=True)
        hit = idx == first
        v = jnp.where(hit, NEG_INF, v)
        sel = jnp.where(hit, 1.0, sel)
    return sel


def _topk_sorted(v, n, payload=None):
    size = v.shape[0]
    idx = lax.broadcasted_iota(jnp.int32, v.shape, 0).astype(F32)
    vals, picks = [], []
    for _ in range(n):
        m = jnp.max(v, axis=0, keepdims=True)
        first = jnp.min(jnp.where(v == m, idx, float(size)), axis=0, keepdims=True)
        hit = idx == first
        vals.append(m)
        if payload is None:
            picks.append(first)
        else:
            picks.append(jnp.max(jnp.where(hit, payload, -1.0), axis=0, keepdims=True))
        v = jnp.where(hit, NEG_INF, v)
    return jnp.concatenate(vals, axis=0), jnp.concatenate(picks, axis=0)


def _proj_kernel(h_ref, w_ref, o_ref, *, act, tn):
    h = h_ref[...]
    for c in range(w_ref.shape[1] // tn):
        z = _dot(h, w_ref[:, c * tn:(c + 1) * tn])
        if act == "sigmoid":
            z = jax.nn.sigmoid(z)
        o_ref[:, c * tn:(c + 1) * tn] = z.astype(o_ref.dtype)


def _proj(h, w, *, act=None, out_dtype=F32, tm=ROW_TILE, tn=512, wn=2048):
    m, d = h.shape
    n = w.shape[1]
    tm = min(tm, m)
    wn = min(wn, n)
    return pl.pallas_call(
        functools.partial(_proj_kernel, act=act, tn=tn),
        grid=(n // wn, m // tm),
        in_specs=[pl.BlockSpec((tm, d), lambda j, i: (i, 0)),
                  pl.BlockSpec((d, wn), lambda j, i: (0, j))],
        out_specs=pl.BlockSpec((tm, wn), lambda j, i: (i, j)),
        out_shape=jax.ShapeDtypeStruct((m, n), out_dtype),
        compiler_params=_params("parallel", "parallel"),
        name="mixer_proj",
    )(h, w)


QKV_COLS = N_HEADS * HEAD_DIM + 6 * KV_HEADS * HEAD_DIM + HEAD_DIM


def _qkv_kernel(x_ref, g_ref, w_ref, cos_ref, sin_ref, qq_ref, nsa_ref, win_ref, gate_ref, kvb_ref, h_ref):
    h = _rms(x_ref[...], g_ref[...]).astype(BF16)
    h_ref[...] = h
    cos = cos_ref[...]
    sin = sin_ref[...]

    def rope(z):
        return z * cos + pltpu.roll(z, HEAD_DIM // 2, 1) * sin

    qw = N_HEADS * HEAD_DIM
    for c in range(qw // 512):
        z = _dot(h, w_ref[:, c * 512:(c + 1) * 512])
        for j in range(4):
            zh = z[:, j * 128:(j + 1) * 128]
            col = c * 512 + j * 128
            qq_ref[:, col:col + 128] = zh.astype(BF16)
            qq_ref[:, qw + col:qw + col + 128] = rope(zh).astype(BF16)
    tm = x_ref.shape[0]

    def put(ref, comp, n_comp, val):
        ref[pl.ds(comp, tm, stride=n_comp), :] = val

    for c in range(3):
        z = _dot(h, w_ref[:, qw + c * 512:qw + (c + 1) * 512])
        for j in range(2):
            zk = z[:, j * 128:(j + 1) * 128]
            zv = z[:, 256 + j * 128:256 + (j + 1) * 128]
            if c == 0:
                put(nsa_ref, j, ROWS_PER_POS, zk)
                put(nsa_ref, KV_HEADS + j, ROWS_PER_POS, zv)
                continue
            zr = rope(zk)
            if c == 1:
                put(nsa_ref, 2 * KV_HEADS + j, ROWS_PER_POS, zr)
                put(nsa_ref, 3 * KV_HEADS + j, ROWS_PER_POS, zv)
            else:
                put(win_ref, j, WIN_ROWS_PER_POS, zr)
                put(win_ref, KV_HEADS + j, WIN_ROWS_PER_POS, zv)
            base = (c - 1) * 512
            kvb_ref[:, base + j * 128:base + (j + 1) * 128] = zr.astype(BF16)
            kvb_ref[:, base + 256 + j * 128:base + 256 + (j + 1) * 128] = zv.astype(BF16)
    z = _dot(h, w_ref[:, qw + 1536:qw + 1536 + 128])
    gate_ref[...] = jax.nn.sigmoid(z)


def _qkv_proj(x, g, w, cos, sin, *, tm=ROW_TILE):
    m, d = x.shape
    tm = min(tm, m)
    row = lambda i: (i, 0)
    const = lambda i: (0, 0)
    return pl.pallas_call(
        _qkv_kernel,
        grid=(m // tm,),
        in_specs=[pl.BlockSpec((tm, d), row), pl.BlockSpec((1, d), const),
                  pl.BlockSpec((d, QKV_COLS), const),
                  pl.BlockSpec((tm, HEAD_DIM), row), pl.BlockSpec((tm, HEAD_DIM), row)],
        out_specs=[pl.BlockSpec((tm, 2048), row), pl.BlockSpec((tm * ROWS_PER_POS, HEAD_DIM), row),
                   pl.BlockSpec((tm * WIN_ROWS_PER_POS, HEAD_DIM), row), pl.BlockSpec((tm, 128), row),
                   pl.BlockSpec((tm, 1024), row), pl.BlockSpec((tm, d), row)],
        out_shape=[jax.ShapeDtypeStruct((m, 2048), BF16), jax.ShapeDtypeStruct((m * ROWS_PER_POS, HEAD_DIM), F32),
                   jax.ShapeDtypeStruct((m * WIN_ROWS_PER_POS, HEAD_DIM), F32), jax.ShapeDtypeStruct((m, 128), F32),
                   jax.ShapeDtypeStruct((m, 1024), BF16), jax.ShapeDtypeStruct((m, d), BF16)],
        compiler_params=_params("parallel"),
        name="qkv_proj",
    )(x, g, w, cos, sin)


SUB_ROWS = CMP_STRIDE * ROWS_PER_POS


def _compress_rows(load, n_rows, wk_ref, wv_ref, y_ref):
    for kv in range(2):
        w_ref = wk_ref if kv == 0 else wv_ref
        for g in range(KV_HEADS):
            comp = kv * KV_HEADS + g
            acc = jnp.zeros((n_rows, 2 * HEAD_DIM), F32)
            for i in range(CMP_STRIDE):
                acc = acc + _dot(load(i, comp).astype(BF16), w_ref[i])
            y_ref[0, :, 2 * comp * HEAD_DIM:2 * (comp + 1) * HEAD_DIM] = acc


def _compress_prompt_kernel(x_ref, wk_ref, wv_ref, y_ref):
    ts = x_ref.shape[0] // SUB_ROWS
    load = lambda i, comp: x_ref[pl.ds(i * ROWS_PER_POS + comp, ts, stride=SUB_ROWS), :]
    _compress_rows(load, ts, wk_ref, wv_ref, y_ref)


def _compress_prompt(nsa_rows, b, wk, wv, *, ts=128):
    n_sub = nsa_rows.shape[0] // (b * SUB_ROWS)
    ts = min(ts, n_sub)
    steps = n_sub // ts
    wspec = pl.BlockSpec((CMP_STRIDE, HEAD_DIM, 2 * HEAD_DIM), lambda bi, ci: (0, 0, 0))
    return pl.pallas_call(
        _compress_prompt_kernel,
        grid=(b, steps),
        in_specs=[pl.BlockSpec((ts * SUB_ROWS, HEAD_DIM), lambda bi, ci: (bi * steps + ci, 0)), wspec, wspec],
        out_specs=pl.BlockSpec((1, ts, 1024), lambda bi, ci: (bi, ci, 0)),
        out_shape=jax.ShapeDtypeStruct((b, n_sub, 1024), F32),
        compiler_params=_params("parallel", "parallel"),
        name="compress_prompt",
    )(nsa_rows, wk, wv)


PAGES_PER_STEP = 16


def _compress_pages_kernel(pt_ref, *refs):
    pages = refs[:PAGES_PER_STEP]
    wk_ref, wv_ref, y_ref, kvs_ref = refs[PAGES_PER_STEP:]
    per_page = pages[0].shape[0] // SUB_ROWS
    page_size = pages[0].shape[0] // ROWS_PER_POS

    def load(i, comp):
        rows = pl.ds(i * ROWS_PER_POS + comp, per_page, stride=SUB_ROWS)
        return jnp.concatenate([p[rows, :] for p in pages], axis=0)

    _compress_rows(load, PAGES_PER_STEP * per_page, wk_ref, wv_ref, y_ref)
    for j in range(2 * KV_HEADS):
        rows = pl.ds(2 * KV_HEADS + j, page_size, stride=ROWS_PER_POS)
        kvs_ref[0, :, j * HEAD_DIM:(j + 1) * HEAD_DIM] = jnp.concatenate(
            [p[rows, :] for p in pages], axis=0).astype(BF16)


def _page_specs(page_rows):
    def spec(k):
        return pl.BlockSpec((page_rows, HEAD_DIM), lambda bi, ci, pt: (pt[bi, ci * PAGES_PER_STEP + k], 0))
    return [spec(k) for k in range(PAGES_PER_STEP)]


def _compress_pages(cache_rows, page_table, page_size, wk, wv):
    nb, n_pages = page_table.shape
    per_page = page_size // CMP_STRIDE
    rows = PAGES_PER_STEP * per_page
    const3 = lambda bi, ci, pt: (0, 0, 0)
    grid_spec = pltpu.PrefetchScalarGridSpec(
        num_scalar_prefetch=1,
        grid=(nb, n_pages // PAGES_PER_STEP),
        in_specs=_page_specs(page_size * ROWS_PER_POS) + [
            pl.BlockSpec((CMP_STRIDE, HEAD_DIM, 2 * HEAD_DIM), const3),
            pl.BlockSpec((CMP_STRIDE, HEAD_DIM, 2 * HEAD_DIM), const3)],
        out_specs=[pl.BlockSpec((1, rows, 1024), lambda bi, ci, pt: (bi, ci, 0)),
                   pl.BlockSpec((1, PAGES_PER_STEP * page_size, 512), lambda bi, ci, pt: (bi, ci, 0))],
    )
    return pl.pallas_call(
        _compress_pages_kernel,
        grid_spec=grid_spec,
        out_shape=[jax.ShapeDtypeStruct((nb, n_pages * per_page, 1024), F32),
                   jax.ShapeDtypeStruct((nb, n_pages * page_size, 512), BF16)],
        compiler_params=_params("parallel", "parallel"),
        name="compress_pages",
    )(page_table, *([cache_rows] * PAGES_PER_STEP), wk, wv)


def _combine_compressed(y_ref, kcp_ref, tmp_ref):
    ns = y_ref.shape[1]
    nb = ns // CMP_PER_SEL
    last = lax.broadcasted_iota(jnp.int32, (ns, 1), 0) == ns - 1
    for a in range(2 * KV_HEADS):
        y1 = y_ref[0, :, a * 256:a * 256 + 128]
        y2 = y_ref[0, :, a * 256 + 128:(a + 1) * 256]
        nxt = jnp.where(last, 0.0, pltpu.roll(y2, ns - 1, 0))
        tmp_ref[...] = y1 + nxt
        for c in range(CMP_PER_SEL):
            kcp_ref[a * CMP_PER_SEL + c] = tmp_ref[pl.ds(c, nb, stride=CMP_PER_SEL), :].astype(BF16)


def _cmp_attention(q_rows, pos, kcp_ref, g):
    nb = kcp_ref.shape[1]
    blk = lax.broadcasted_iota(jnp.int32, (1, nb), 1)
    masks = [(SEL_BLOCK * blk + CMP_STRIDE * c + CMP_LEN - 1) <= pos for c in range(CMP_PER_SEL)]
    outs = []
    imp = None
    for qh in q_rows:
        s = [_dot_nt(qh, kcp_ref[g * CMP_PER_SEL + c]) * SCALE for c in range(CMP_PER_SEL)]
        p = _masked_softmax_parts(s, masks)
        o = functools.reduce(jnp.add, [
            _dot(p[c].astype(BF16), kcp_ref[(KV_HEADS + g) * CMP_PER_SEL + c]) for c in range(CMP_PER_SEL)])
        outs.append(o)
        ps = functools.reduce(jnp.add, p)
        imp = ps if imp is None else imp + ps
    return outs, imp


def _cmp_prompt_kernel(q_ref, y_ref, o_ref, sel_ref, kcp_ref, tmp_ref):
    i = pl.program_id(1)

    @pl.when(i == 0)
    def _():
        _combine_compressed(y_ref, kcp_ref, tmp_ref)

    nb = kcp_ref.shape[1]
    pos = i * Q_BLOCK + lax.broadcasted_iota(jnp.int32, (Q_BLOCK, 1), 0)
    blk = lax.broadcasted_iota(jnp.int32, (1, nb), 1)
    cur = pos // SEL_BLOCK
    forced = (blk == cur) | (blk == 0)
    causal = blk <= cur
    pos_rows = jnp.concatenate([pos] * GROUP, axis=0)
    for g in range(KV_HEADS):
        (o,), imp_rows = _cmp_attention([_stack_heads(q_ref, g)], pos_rows, kcp_ref, g)
        for r in range(GROUP):
            hcol = (g * GROUP + r) * HEAD_DIM
            o_ref[0, :, hcol:hcol + HEAD_DIM] = o[r * Q_BLOCK:(r + 1) * Q_BLOCK]
        imp = functools.reduce(jnp.add, [imp_rows[r * Q_BLOCK:(r + 1) * Q_BLOCK] for r in range(GROUP)])
        v = jnp.where(forced, FORCE_SCORE, jnp.where(causal, imp, -1.0))
        sel_t = _topk_select(v.T, SEL_TOPN, 0)
        sel = jnp.where(causal, sel_t.T, 0.0)
        sel_ref[0, :, g * nb:(g + 1) * nb] = sel.astype(BF16)


def _cmp_prompt(qq3, y):
    b, t, _ = qq3.shape
    ns = y.shape[1]
    nb = ns // CMP_PER_SEL
    return pl.pallas_call(
        _cmp_prompt_kernel,
        grid=(b, t // Q_BLOCK),
        in_specs=[pl.BlockSpec((1, Q_BLOCK, 1024), lambda bi, i: (bi, i, 0)),
                  pl.BlockSpec((1, ns, 1024), lambda bi, i: (bi, 0, 0))],
        out_specs=[pl.BlockSpec((1, Q_BLOCK, 1024), lambda bi, i: (bi, i, 0)),
                   pl.BlockSpec((1, Q_BLOCK, KV_HEADS * nb), lambda bi, i: (bi, i, 0))],
        out_shape=[jax.ShapeDtypeStruct((b, t, 1024), F32),
                   jax.ShapeDtypeStruct((b, t, KV_HEADS * nb), BF16)],
        scratch_shapes=[pltpu.VMEM((2 * KV_HEADS * CMP_PER_SEL, nb, HEAD_DIM), BF16),
                        pltpu.VMEM((ns, HEAD_DIM), F32)],
        compiler_params=_params("parallel", "arbitrary"),
        name="cmp_prompt",
    )(qq3, y)


SEL_CHUNK = 512


def _expand_blocks(sel, first_block, width):
    nb = sel.shape[1]
    b = lax.broadcasted_iota(jnp.int32, (nb, width), 0)
    t = lax.broadcasted_iota(jnp.int32, (nb, width), 1)
    e = jnp.where(b == first_block + t // SEL_BLOCK, 1.0, 0.0).astype(BF16)
    return _dot(sel, e)


def _sel_prompt_kernel(q_ref, sel_ref, kv_ref, o_ref):
    i = pl.program_id(1)
    nb = sel_ref.shape[2] // KV_HEADS
    pos = i * Q_BLOCK + lax.broadcasted_iota(jnp.int32, (Q_BLOCK, 1), 0)
    n_chunks = (i * Q_BLOCK + Q_BLOCK + SEL_CHUNK - 1) // SEL_CHUNK
    lane = lax.broadcasted_iota(jnp.int32, (1, SEL_CHUNK), 1)
    rows = GROUP * Q_BLOCK
    for g in range(KV_HEADS):
        q = jnp.concatenate(
            [q_ref[0, :, (g * GROUP + r) * HEAD_DIM:(g * GROUP + r + 1) * HEAD_DIM] for r in range(GROUP)], axis=0)
        sel = sel_ref[0, :, g * nb:(g + 1) * nb]

        def chunk(c, carry):
            m, l, acc = carry
            start = pl.multiple_of(c * SEL_CHUNK, SEL_CHUNK)
            k = kv_ref[0, pl.ds(start, SEL_CHUNK), g * HEAD_DIM:(g + 1) * HEAD_DIM]
            v = kv_ref[0, pl.ds(start, SEL_CHUNK), (KV_HEADS + g) * HEAD_DIM:(KV_HEADS + g + 1) * HEAD_DIM]
            picked = _expand_blocks(sel, c * (SEL_CHUNK // SEL_BLOCK), SEL_CHUNK)
            bias = jnp.where(picked > 0.5, jnp.where((start + lane) <= pos, 0.0, NEG_INF), NEG_INF)
            s = _dot_nt(q, k) * (SCALE * LOG2E) + jnp.concatenate([bias] * GROUP, axis=0)
            m_new = jnp.maximum(m, jnp.max(s, axis=-1, keepdims=True))
            m_safe = jnp.where(m_new == NEG_INF, 0.0, m_new)
            p = jnp.exp2(s - m_safe)
            alpha = jnp.exp2(m - m_safe)
            l = alpha * l + jnp.sum(p, axis=-1, keepdims=True)
            acc = alpha * acc + _dot(p.astype(BF16), v)
            return m_new, l, acc

        def pair(cc, carry):
            return chunk(2 * cc + 1, chunk(2 * cc, carry))

        init = (jnp.full((rows, 1), NEG_INF, F32), jnp.zeros((rows, 1), F32), jnp.zeros((rows, HEAD_DIM), F32))
        state = lax.fori_loop(0, n_chunks // 2, pair, init)
        m, l, acc = lax.cond(n_chunks % 2 == 1, lambda st: chunk(n_chunks - 1, st), lambda st: st, state)
        o = acc * (1.0 / jnp.maximum(l, 1e-30))
        for r in range(GROUP):
            hcol = (g * GROUP + r) * HEAD_DIM
            o_ref[0, :, hcol:hcol + HEAD_DIM] = o[r * Q_BLOCK:(r + 1) * Q_BLOCK]


def _sel_prompt(qq3, sel, kvb3):
    b, t, _ = qq3.shape
    nb2 = sel.shape[2]
    return pl.pallas_call(
        _sel_prompt_kernel,
        grid=(b, t // Q_BLOCK),
        in_specs=[pl.BlockSpec((1, Q_BLOCK, 1024), lambda bi, i: (bi, i, 1)),
                  pl.BlockSpec((1, Q_BLOCK, nb2), lambda bi, i: (bi, i, 0)),
                  pl.BlockSpec((1, t, 512), lambda bi, i: (bi, 0, 0))],
        out_specs=pl.BlockSpec((1, Q_BLOCK, 1024), lambda bi, i: (bi, i, 0)),
        out_shape=jax.ShapeDtypeStruct((b, t, 1024), F32),
        compiler_params=_params("parallel", "arbitrary"),
        name="sel_prompt",
    )(qq3, sel, kvb3)


def _win_prompt_kernel(q_ref, kv_ref, o_ref, *, span):
    i = pl.program_id(1)
    pos = i * Q_BLOCK + lax.broadcasted_iota(jnp.int32, (Q_BLOCK, 1), 0)
    pos = jnp.concatenate([pos] * GROUP, axis=0)
    start = pl.multiple_of(jnp.maximum(i * Q_BLOCK + Q_BLOCK - span, 0), Q_BLOCK)
    kpos = start + lax.broadcasted_iota(jnp.int32, (1, span), 1)
    diff = pos - kpos
    ok = (diff >= 0) & (diff < WINDOW)
    for g in range(KV_HEADS):
        q = jnp.concatenate(
            [q_ref[0, :, (g * GROUP + r) * HEAD_DIM:(g * GROUP + r + 1) * HEAD_DIM] for r in range(GROUP)], axis=0)
        k = kv_ref[0, pl.ds(start, span), g * HEAD_DIM:(g + 1) * HEAD_DIM]
        v = kv_ref[0, pl.ds(start, span), (KV_HEADS + g) * HEAD_DIM:(KV_HEADS + g + 1) * HEAD_DIM]
        s = _dot_nt(q, k) * SCALE
        (p,) = _masked_softmax_parts([s], [ok])
        o = _dot(p.astype(BF16), v)
        for r in range(GROUP):
            hcol = (g * GROUP + r) * HEAD_DIM
            o_ref[0, :, hcol:hcol + HEAD_DIM] = o[r * Q_BLOCK:(r + 1) * Q_BLOCK]


def _win_prompt(qq3, kvb3):
    b, t, _ = qq3.shape
    span = min(WINDOW + Q_BLOCK, t)
    return pl.pallas_call(
        functools.partial(_win_prompt_kernel, span=span),
        grid=(b, t // Q_BLOCK),
        in_specs=[pl.BlockSpec((1, Q_BLOCK, 1024), lambda bi, i: (bi, i, 1)),
                  pl.BlockSpec((1, t, 512), lambda bi, i: (bi, 0, 1))],
        out_specs=pl.BlockSpec((1, Q_BLOCK, 1024), lambda bi, i: (bi, i, 0)),
        out_shape=jax.ShapeDtypeStruct((b, t, 1024), F32),
        compiler_params=_params("parallel", "arbitrary"),
        name="win_prompt",
    )(qq3, kvb3)


def _pool_kernel(prev_ref, u_ref, w_ref, sc_ref, o_ref, *, base, zero_first_prev):
    i = pl.program_id(1)
    tq = u_ref.shape[1]
    cur = u_ref[0]
    prev = prev_ref[0]
    if zero_first_prev:
        prev = jnp.where(i == 0, 0.0, prev)
    ext = jnp.concatenate([prev, cur], axis=0)
    gpos = base + i * tq + lax.broadcasted_iota(jnp.int32, (tq, 1), 0)
    gw = ext.shape[1] // len(POOL_WINDOWS)
    for gi, w in enumerate(POOL_WINDOWS):
        s = ext[:, gi * gw:(gi + 1) * gw]
        span = 1
        while span < w:
            s = s + pltpu.roll(s, span, 0)
            span *= 2
        cnt = jnp.minimum(gpos + 1, w).astype(F32)
        d = s[POOL_PREV:] / cnt - cur[:, gi * gw:(gi + 1) * gw]
        o = _dot(d.astype(BF16), w_ref[gi]) * sc_ref[:, gi * gw:(gi + 1) * gw]
        o_ref[0, :, gi * gw:(gi + 1) * gw] = o


def _pool(prev, u3, w, scale, *, base, zero_first_prev, tq=512):
    b, t, c = u3.shape
    tq = min(tq, t)
    ratio = tq // POOL_PREV
    if zero_first_prev:
        prev_map = lambda bi, i: (bi, jnp.maximum(i * ratio - 1, 0), 0)
    else:
        prev_map = lambda bi, i: (bi, 0, 0)
    ng = len(POOL_WINDOWS)
    return pl.pallas_call(
        functools.partial(_pool_kernel, base=base, zero_first_prev=zero_first_prev),
        grid=(b, t // tq),
        in_specs=[pl.BlockSpec((1, POOL_PREV, c), prev_map),
                  pl.BlockSpec((1, tq, c), lambda bi, i: (bi, i, 0)),
                  pl.BlockSpec((ng, c // ng, c // ng), lambda bi, i: (0, 0, 0)),
                  pl.BlockSpec((1, c), lambda bi, i: (0, 0))],
        out_specs=pl.BlockSpec((1, tq, c), lambda bi, i: (bi, i, 0)),
        out_shape=jax.ShapeDtypeStruct((b, t, c), F32),
        compiler_params=_params("parallel", "parallel"),
        name="pool_mix",
    )(prev, u3, w, scale)


def _mix_kernel(oc_ref, os_ref, ow_ref, gt_ref, po_ref, ga_ref, gb_ref, wa_ref, wp_ref, mix_ref):
    gt = gt_ref[...]
    cols = []
    for h in range(N_HEADS):
        sl = slice(h * HEAD_DIM, (h + 1) * HEAD_DIM)
        o = (gt[:, h:h + 1] * oc_ref[:, sl] + gt[:, N_HEADS + h:N_HEADS + h + 1] * os_ref[:, sl]
             + gt[:, 2 * N_HEADS + h:2 * N_HEADS + h + 1] * ow_ref[:, sl])
        cols.append(o.astype(BF16))
    a = _dot(jnp.concatenate(cols, axis=1), wa_ref[...])
    p = _dot(po_ref[...].astype(BF16), wp_ref[...])
    mix_ref[...] = (ga_ref[...] * a + gb_ref[...] * p).astype(BF16)


def _mix(oc, os_, ow, gt, po, gab, wa, wp, *, tm=ROW_TILE):
    m = oc.shape[0]
    d = wa.shape[1]
    tm = min(tm, m)
    row = lambda i: (i, 0)
    const = lambda i: (0, 0)
    return pl.pallas_call(
        _mix_kernel,
        grid=(m // tm,),
        in_specs=[pl.BlockSpec((tm, 1024), row), pl.BlockSpec((tm, 1024), row), pl.BlockSpec((tm, 1024), row),
                  pl.BlockSpec((tm, 128), row), pl.BlockSpec((tm, 1024), row),
                  pl.BlockSpec((tm, d), lambda i: (i, 0)), pl.BlockSpec((tm, d), lambda i: (i, 1)),
                  pl.BlockSpec((1024, d), const), pl.BlockSpec((1024, d), const)],
        out_specs=pl.BlockSpec((tm, d), row),
        out_shape=jax.ShapeDtypeStruct((m, d), BF16),
        compiler_params=_params("parallel"),
        name="branch_mix",
    )(oc, os_, ow, gt, po, gab, gab, wa, wp)


def _out_kernel(x_ref, mix_ref, w_ref, o_ref):
    o_ref[...] = x_ref[...] + _dot(mix_ref[...], w_ref[...])


def _out_proj(x, mix, w, *, tm=ROW_TILE):
    m, d = x.shape
    tm = min(tm, m)
    return pl.pallas_call(
        _out_kernel,
        grid=(m // tm,),
        in_specs=[pl.BlockSpec((tm, d), lambda i: (i, 0)), pl.BlockSpec((tm, d), lambda i: (i, 0)),
                  pl.BlockSpec((d, d), lambda i: (0, 0))],
        out_specs=pl.BlockSpec((tm, d), lambda i: (i, 0)),
        out_shape=jax.ShapeDtypeStruct((m, d), F32),
        compiler_params=_params("parallel"),
        name="out_proj",
    )(x, mix, w)


def _pair_candidates(sv, si):
    k = PEER_TOPK
    row = lax.broadcasted_iota(jnp.int32, (8, 1), 0)
    vals, ids = [], []
    for a in range(k // 2):
        lim = k // (a + 1)
        nrow = k if a == 0 else 8
        v = sv[0][a:a + 1] + sv[1][:nrow]
        if lim < nrow:
            v = jnp.where(row < lim, v, NEG_INF)
        vals.append(v)
        ids.append(si[0][a:a + 1] * float(PEER_NKEYS) + si[1][:nrow])
    vals.append(sv[0][k // 2:] + sv[1][0:1])
    ids.append(si[0][k // 2:] * float(PEER_NKEYS) + si[1][0:1])
    return jnp.concatenate(vals, axis=0), jnp.concatenate(ids, axis=0)


def _peer_score_kernel(x_ref, g_ref, wq_ref, keys_ref, ids_ref, gw_ref):
    h = _rms(x_ref[...], g_ref[...]).astype(BF16)
    q = _dot(h, wq_ref[...]).astype(BF16)
    dk = PEER_NKEYS
    ids, gws = [], []
    for hd in range(PEER_HEADS):
        sv, si = [], []
        for c in range(2):
            qhc = q[:, (hd * 2 + c) * dk:(hd * 2 + c + 1) * dk]
            st = _dot_nt(keys_ref[hd * 2 + c], qhc)
            v, ix = _topk_sorted(st, PEER_TOPK)
            sv.append(v)
            si.append(ix)
        comb, eid = _pair_candidates(sv, si)
        cv, ce = _topk_sorted(comb, PEER_TOPK, payload=eid)
        e = jnp.exp(cv - cv[0:1])
        gws.append(e / jnp.sum(e, axis=0, keepdims=True))
        ids.append(ce)
    ids_ref[...] = jnp.concatenate(ids, axis=0).T.astype(jnp.int32)
    gw_ref[...] = jnp.concatenate(gws, axis=0)


def _peer_score(x, g, wq, keys, *, tb=128):
    m, d = x.shape
    nk = PEER_HEADS * PEER_TOPK
    return pl.pallas_call(
        _peer_score_kernel,
        grid=(m // tb,),
        in_specs=[pl.BlockSpec((tb, d), lambda i: (i, 0)), pl.BlockSpec((1, d), lambda i: (0, 0)),
                  pl.BlockSpec(wq.shape, lambda i: (0, 0)),
                  pl.BlockSpec(keys.shape, lambda i: (0, 0, 0))],
        out_specs=[pl.BlockSpec((tb, nk), lambda i: (i, 0)), pl.BlockSpec((nk, tb), lambda i: (0, i))],
        out_shape=[jax.ShapeDtypeStruct((m, nk), jnp.int32), jax.ShapeDtypeStruct((nk, m), F32)],
        compiler_params=_params("parallel"),
        name="peer_score",
    )(x, g, wq, keys)


def _gelu(x):
    return 0.5 * x * (1.0 + lax.erf(x * (2.0 ** -0.5)))


PACK_ROWS = 256


def _pack_kernel(u_ref, v_ref, o_ref):
    lo = pltpu.bitcast(u_ref[...].astype(BF16).astype(F32), jnp.uint32) >> 16
    hi = pltpu.bitcast(v_ref[...].astype(BF16).astype(F32), jnp.uint32) & jnp.uint32(0xFFFF0000)
    o_ref[...] = (lo | hi).reshape(o_ref.shape)


def _pack_tables(u, v):
    e, d = u.shape
    return pl.pallas_call(
        _pack_kernel,
        grid=(e // PACK_ROWS,),
        in_specs=[pl.BlockSpec((PACK_ROWS, d), lambda i: (i, 0)), pl.BlockSpec((PACK_ROWS, d), lambda i: (i, 0))],
        out_specs=pl.BlockSpec((PACK_ROWS, 1, d), lambda i: (i, 0, 0)),
        out_shape=jax.ShapeDtypeStruct((e, 1, d), jnp.uint32),
        compiler_params=_params("parallel"),
        name="peer_pack",
    )(u, v)


PEER_RING = 8


def _peer_apply_kernel(ids_ref, nxt_ref, x_ref, gffn_ref, gw_ref, gfin_ref, tab_ref, y_ref, *scratch):
    bufs = scratch[:PEER_RING]
    sem, hbuf, obuf = scratch[PEER_RING:]
    step = pl.program_id(0)
    tb, d = x_ref.shape
    nk = gw_ref.shape[0]
    nchunk = d // 128
    ahead = PEER_RING - 1
    n_groups = tb // PEER_RING
    hbuf[...] = _rms(x_ref[...], gffn_ref[...])

    def row_copy(ids, t, k, slot):
        return pltpu.make_async_copy(tab_ref.at[ids[t, k]], bufs[slot].at[pl.ds(k, 1), :], sem.at[slot])

    def issue(ids, t, slot):
        for k in range(nk):
            row_copy(ids, t, k, slot).start(priority=k % 2)

    def drain(slot):
        for k in range(nk):
            row_copy(ids_ref, 0, k, slot).wait()

    @pl.when(step == 0)
    def _():
        for s in range(ahead):
            issue(ids_ref, s, s)

    lane = lax.broadcasted_iota(jnp.int32, (nk, tb), 1)

    def compute(t, slot):
        buf = bufs[slot]
        hrow = hbuf[pl.ds(t, 1), :]
        acc = jnp.zeros((nk, 128), F32)
        for j in range(nchunk):
            w = buf[:, j * 128:(j + 1) * 128]
            u = pltpu.bitcast(w << 16, F32)
            acc = acc + u * hrow[:, j * 128:(j + 1) * 128]
        act = jnp.sum(acc, axis=1, keepdims=True)
        gcol = jnp.sum(jnp.where(lane == t, gw_ref[...], 0.0), axis=1, keepdims=True)
        coef = _gelu(act) * gcol
        outs = []
        for j in range(nchunk):
            w = buf[:, j * 128:(j + 1) * 128]
            v = pltpu.bitcast(w & jnp.uint32(0xFFFF0000), F32)
            outs.append(jnp.sum(v * coef, axis=0, keepdims=True))
        obuf[pl.ds(t, 1), :] = jnp.concatenate(outs, axis=1)

    def group(p, last):
        for s in range(PEER_RING):
            t = p * PEER_RING + s
            drain(s)
            compute(t, s)
            if last and s > 0:
                issue(nxt_ref, s - 1, (s + ahead) % PEER_RING)
            else:
                issue(ids_ref, t + ahead, (s + ahead) % PEER_RING)

    def body(p, carry):
        group(p, False)
        return carry

    lax.fori_loop(0, n_groups - 1, body, 0)
    group(n_groups - 1, True)

    @pl.when(step == pl.num_programs(0) - 1)
    def _():
        for s in range(ahead):
            drain(s)

    y_ref[...] = _rms(x_ref[...] + obuf[...], gfin_ref[...])


def _peer_apply(ids, x, gffn, gw, gfin, table, *, tb=128):
    m, d = x.shape
    nk = ids.shape[1]
    steps = m // tb
    return pl.pallas_call(
        _peer_apply_kernel,
        grid=(steps,),
        in_specs=[pl.BlockSpec((tb, nk), lambda i: (i, 0), memory_space=pltpu.SMEM),
                  pl.BlockSpec((tb, nk), lambda i: (jnp.minimum(i + 1, steps - 1), 0), memory_space=pltpu.SMEM),
                  pl.BlockSpec((tb, d), lambda i: (i, 0)),
                  pl.BlockSpec((1, d), lambda i: (0, 0)),
                  pl.BlockSpec((nk, tb), lambda i: (0, i)),
                  pl.BlockSpec((1, d), lambda i: (0, 0)),
                  pl.BlockSpec(memory_space=pl.ANY)],
        out_specs=pl.BlockSpec((tb, d), lambda i: (i, 0)),
        out_shape=jax.ShapeDtypeStruct((m, d), F32),
        scratch_shapes=[pltpu.VMEM((nk, d), jnp.uint32)] * PEER_RING + [
            pltpu.SemaphoreType.DMA((PEER_RING,)), pltpu.VMEM((tb, d), F32), pltpu.VMEM((tb, d), F32)],
        compiler_params=_params("arbitrary"),
        name="peer_apply",
    )(ids, ids, x, gffn, gw, gfin, table)


def _stack_heads(q_ref, g):
    return jnp.concatenate(
        [q_ref[0, :, (g * GROUP + r) * HEAD_DIM:(g * GROUP + r + 1) * HEAD_DIM] for r in range(GROUP)], axis=0)


NEW_PAD = 128


def _pad_new(x):
    return jnp.concatenate([x, jnp.zeros((NEW_PAD - x.shape[0], x.shape[1]), F32)], axis=0).astype(BF16)


def _cmp_sample_kernel(q_ref, y_ref, o_ref, sel_ref, kcp_ref, tmp_ref, *, past):
    _combine_compressed(y_ref, kcp_ref, tmp_ref)
    t = q_ref.shape[1]
    nb = kcp_ref.shape[1]
    tpos = past + lax.broadcasted_iota(jnp.int32, (t, 1), 0)
    pos = jnp.concatenate([tpos] * GROUP, axis=0)
    blk = lax.broadcasted_iota(jnp.int32, (1, nb), 1)
    for g in range(KV_HEADS):
        q = _stack_heads(q_ref, g)
        outs, imp = _cmp_attention([q], pos, kcp_ref, g)
        o = outs[0]
        for r in range(GROUP):
            hcol = (g * GROUP + r) * HEAD_DIM
            o_ref[0, :, hcol:hcol + HEAD_DIM] = o[r * t:(r + 1) * t]
        imp_t = functools.reduce(jnp.add, [imp[r * t:(r + 1) * t] for r in range(GROUP)])
        v = jnp.where(blk == 0, FORCE_SCORE, imp_t)
        sel = _topk_select(v, SEL_TOPN - 1, 1)
        sel_ref[0, :, g * nb:(g + 1) * nb] = sel.astype(BF16)


def _cmp_sample(qq3, y, *, past):
    b, t, _ = qq3.shape
    ns = y.shape[1]
    nb = ns // CMP_PER_SEL
    return pl.pallas_call(
        functools.partial(_cmp_sample_kernel, past=past),
        grid=(b,),
        in_specs=[pl.BlockSpec((1, t, 1024), lambda bi: (bi, 0, 0)),
                  pl.BlockSpec((1, ns, 1024), lambda bi: (bi, 0, 0))],
        out_specs=[pl.BlockSpec((1, t, 1024), lambda bi: (bi, 0, 0)),
                   pl.BlockSpec((1, t, KV_HEADS * nb), lambda bi: (bi, 0, 0))],
        out_shape=[jax.ShapeDtypeStruct((b, t, 1024), F32),
                   jax.ShapeDtypeStruct((b, t, KV_HEADS * nb), BF16)],
        scratch_shapes=[pltpu.VMEM((2 * KV_HEADS * CMP_PER_SEL, nb, HEAD_DIM), BF16),
                        pltpu.VMEM((ns, HEAD_DIM), F32)],
        compiler_params=_params("parallel"),
        name="cmp_sample",
    )(qq3, y)


def _sel_sample_kernel(q_ref, sel_ref, kv_ref, new_ref, o_ref, m_ref, l_ref, acc_ref):
    c = pl.program_id(1)
    nc = pl.num_programs(1)
    t = q_ref.shape[1]
    nb = sel_ref.shape[2] // KV_HEADS
    width = kv_ref.shape[1]

    @pl.when(c == 0)
    def _():
        m_ref[...] = jnp.full(m_ref.shape, NEG_INF, F32)
        l_ref[...] = jnp.zeros(l_ref.shape, F32)
        acc_ref[...] = jnp.zeros(acc_ref.shape, F32)

    def update(g, s, ok, v):
        s = jnp.where(ok, s, NEG_INF)
        m = m_ref[g]
        m_new = jnp.maximum(m, jnp.max(s, axis=-1, keepdims=True))
        m_safe = jnp.where(m_new == NEG_INF, 0.0, m_new)
        p = jnp.exp(s - m_safe)
        alpha = jnp.exp(m - m_safe)
        l_ref[g] = alpha * l_ref[g] + jnp.sum(p, axis=-1, keepdims=True)
        acc_ref[g] = alpha * acc_ref[g] + _dot(p.astype(BF16), v)
        m_ref[g] = m_new

    for g in range(KV_HEADS):
        q = _stack_heads(q_ref, g)
        k = kv_ref[0, :, g * HEAD_DIM:(g + 1) * HEAD_DIM]
        v = kv_ref[0, :, (KV_HEADS + g) * HEAD_DIM:(KV_HEADS + g + 1) * HEAD_DIM]
        s = _dot_nt(q, k) * SCALE
        picked = _expand_blocks(sel_ref[0, :, g * nb:(g + 1) * nb], c * (width // SEL_BLOCK), width)
        picked = jnp.concatenate([picked] * GROUP, axis=0)
        update(g, s, picked > 0.5, v)

    @pl.when(c == nc - 1)
    def _():
        qi = lax.broadcasted_iota(jnp.int32, (t, 1), 0)
        qi = jnp.concatenate([qi] * GROUP, axis=0)
        ok = lax.broadcasted_iota(jnp.int32, (1, NEW_PAD), 1) <= qi
        for g in range(KV_HEADS):
            q = _stack_heads(q_ref, g)
            k = _pad_new(new_ref[pl.ds(2 * KV_HEADS + g, t, stride=ROWS_PER_POS), :])
            v = _pad_new(new_ref[pl.ds(3 * KV_HEADS + g, t, stride=ROWS_PER_POS), :])
            update(g, _dot_nt(q, k) * SCALE, ok, v)
            o = acc_ref[g] * (1.0 / jnp.maximum(l_ref[g], 1e-30))
            for r in range(GROUP):
                hcol = (g * GROUP + r) * HEAD_DIM
                o_ref[0, :, hcol:hcol + HEAD_DIM] = o[r * t:(r + 1) * t]


SEL_SAMPLE_CHUNK = 8192


def _sel_sample(kvs, qq3, sel, nsa_rows):
    nbatch, past, _ = kvs.shape
    t = qq3.shape[1]
    nb2 = sel.shape[2]
    rows = GROUP * t
    width = min(SEL_SAMPLE_CHUNK, past)
    return pl.pallas_call(
        _sel_sample_kernel,
        grid=(nbatch, past // width),
        in_specs=[pl.BlockSpec((1, t, 1024), lambda bi, ci: (bi, 0, 1)),
                  pl.BlockSpec((1, t, nb2), lambda bi, ci: (bi, 0, 0)),
                  pl.BlockSpec((1, width, 512), lambda bi, ci: (bi, ci, 0)),
                  pl.BlockSpec((t * ROWS_PER_POS, HEAD_DIM), lambda bi, ci: (bi, 0))],
        out_specs=pl.BlockSpec((1, t, 1024), lambda bi, ci: (bi, 0, 0)),
        out_shape=jax.ShapeDtypeStruct((nbatch, t, 1024), F32),
        scratch_shapes=[pltpu.VMEM((KV_HEADS, rows, 1), F32), pltpu.VMEM((KV_HEADS, rows, 1), F32),
                        pltpu.VMEM((KV_HEADS, rows, HEAD_DIM), F32)],
        compiler_params=_params("parallel", "arbitrary"),
        name="sel_sample",
    )(qq3, sel, kvs, nsa_rows)


def _win_sample_kernel(q_ref, st_ref, new_ref, o_ref):
    t = q_ref.shape[1]
    nw = st_ref.shape[0] // WIN_ROWS_PER_POS
    old = lambda comp: st_ref[pl.ds(comp, nw, stride=WIN_ROWS_PER_POS), :].astype(BF16)
    new = lambda comp: _pad_new(new_ref[pl.ds(comp, t, stride=WIN_ROWS_PER_POS), :])
    qi = lax.broadcasted_iota(jnp.int32, (t, 1), 0)
    qi = jnp.concatenate([qi] * GROUP, axis=0)
    d_old = (nw + qi) - lax.broadcasted_iota(jnp.int32, (1, nw), 1)
    ok_old = (d_old >= 0) & (d_old < WINDOW)
    d_new = qi - lax.broadcasted_iota(jnp.int32, (1, NEW_PAD), 1)
    ok_new = (d_new >= 0) & (d_new < WINDOW)
    for g in range(KV_HEADS):
        q = _stack_heads(q_ref, g)
        k_old, v_old = old(g), old(KV_HEADS + g)
        k_new, v_new = new(g), new(KV_HEADS + g)
        p_old, p_new = _masked_softmax_parts(
            [_dot_nt(q, k_old) * SCALE, _dot_nt(q, k_new) * SCALE], [ok_old, ok_new])
        o = _dot(p_old.astype(BF16), v_old) + _dot(p_new.astype(BF16), v_new)
        for r in range(GROUP):
            hcol = (g * GROUP + r) * HEAD_DIM
            o_ref[0, :, hcol:hcol + HEAD_DIM] = o[r * t:(r + 1) * t]


def _win_sample(qq3, state_rows, win_rows):
    b, t, _ = qq3.shape
    nw = state_rows.shape[0] // (b * WIN_ROWS_PER_POS)
    return pl.pallas_call(
        _win_sample_kernel,
        grid=(b,),
        in_specs=[pl.BlockSpec((1, t, 1024), lambda bi: (bi, 0, 1)),
                  pl.BlockSpec((nw * WIN_ROWS_PER_POS, HEAD_DIM), lambda bi: (bi, 0)),
                  pl.BlockSpec((t * WIN_ROWS_PER_POS, HEAD_DIM), lambda bi: (bi, 0))],
        out_specs=pl.BlockSpec((1, t, 1024), lambda bi: (bi, 0, 0)),
        out_shape=jax.ShapeDtypeStruct((b, t, 1024), F32),
        compiler_params=_params("parallel"),
        name="win_sample",
    )(qq3, state_rows, win_rows)


def _rope_tables(pos):
    half = HEAD_DIM // 2
    inv = ROPE_THETA ** (-jnp.arange(half, dtype=F32) / half)
    ang = pos.astype(F32)[:, None] * inv[None, :]
    cos, sin = jnp.cos(ang), jnp.sin(ang)
    return jnp.concatenate([cos, cos], axis=-1), jnp.concatenate([-sin, sin], axis=-1)


def _prep_weights(w_in, w_phi_k, w_phi_v, w_pool_group, pool_scale, w_branch_attn, w_branch_pool, w_out,
                  peer_w_query, peer_sub_keys, peer_u, peer_v):
    d = w_in.shape[0]
    qw = N_HEADS * HEAD_DIM
    kvw = 6 * KV_HEADS * HEAD_DIM
    ngw = 3 * N_HEADS
    pw = d // 2
    o1, o2, o3, o4 = qw, qw + kvw, qw + kvw + ngw, qw + kvw + ngw + pw
    wb = w_in.astype(BF16)
    w_qkv = jnp.concatenate([wb[:, :o2], wb[:, o2:o3], jnp.zeros((d, HEAD_DIM - ngw), BF16)], axis=1)
    cat = lambda w: jnp.concatenate([w[:CMP_STRIDE], w[CMP_STRIDE:]], axis=-1).astype(BF16)
    return dict(
        w_qkv=w_qkv, w_u=wb[:, o3:o4], w_gab=wb[:, o4:],
        wk=cat(w_phi_k), wv=cat(w_phi_v),
        w_pool=w_pool_group.astype(BF16), pool_scale=pool_scale.reshape(1, -1),
        w_ba=w_branch_attn.astype(BF16), w_bp=w_branch_pool.astype(BF16), w_out=w_out.astype(BF16),
        w_query=peer_w_query.astype(BF16),
        keys=peer_sub_keys.reshape(PEER_HEADS * 2, PEER_NKEYS, -1).astype(BF16),
        table=_pack_tables(peer_u, peer_v),
    )


def _token_tail(x2d, o_cmp, o_sel, o_win, gates, pool_out, gab, wp, g_ffn, g_final):
    mix = _mix(o_cmp, o_sel, o_win, gates, pool_out, gab, wp["w_ba"], wp["w_bp"])
    x2 = _out_proj(x2d, mix, wp["w_out"])
    ids, gw = _peer_score(x2, g_ffn, wp["w_query"], wp["keys"])
    return _peer_apply(ids, x2, g_ffn, gw, g_final, wp["table"])


def kernel(x_prompt, x_sample, cache_kv_nsa, state_win_kv, state_pool, page_table, g_norm_mix, w_in, w_phi_k,
           w_phi_v, w_pool_group, pool_scale, w_branch_attn, w_branch_pool, w_out, g_norm_ffn, peer_w_query,
           peer_sub_keys, peer_u, peer_v, g_norm_final):
    assert g_norm_mix.shape[0] == 1, "single-layer step"
    bp, t, d = x_prompt.shape
    bs, ts, _ = x_sample.shape
    n_pages = page_table.shape[1]
    page_size = cache_kv_nsa.shape[2]
    past = n_pages * page_size
    wp = _prep_weights(w_in[0], w_phi_k[0], w_phi_v[0], w_pool_group[0], pool_scale[0], w_branch_attn[0],
                       w_branch_pool[0], w_out[0], peer_w_query[0], peer_sub_keys[0], peer_u[0], peer_v[0])
    g_mix = g_norm_mix[0].reshape(1, d)
    g_ffn = g_norm_ffn[0].reshape(1, d)
    g_fin = g_norm_final.reshape(1, d)

    xp = x_prompt.reshape(bp * t, d)
    cos, sin = _rope_tables(jnp.arange(t))
    cos, sin = jnp.tile(cos, (bp, 1)), jnp.tile(sin, (bp, 1))
    qq, nsa, win, gates, kvb, hn = _qkv_proj(xp, g_mix, wp["w_qkv"], cos, sin)
    u = _proj(hn, wp["w_u"])
    gab = _proj(hn, wp["w_gab"], act="sigmoid", out_dtype=BF16)
    qq3, kvb3 = qq.reshape(bp, t, -1), kvb.reshape(bp, t, -1)
    y = _compress_prompt(nsa, bp, wp["wk"], wp["wv"])
    o_cmp, sel = _cmp_prompt(qq3, y)
    o_sel = _sel_prompt(qq3, sel, kvb3)
    o_win = _win_prompt(qq3, kvb3)
    u3 = u.reshape(bp, t, -1)
    pool_out = _pool(u3, u3, wp["w_pool"], wp["pool_scale"], base=0, zero_first_prev=True)
    y_prompt = _token_tail(xp, o_cmp.reshape(bp * t, -1), o_sel.reshape(bp * t, -1), o_win.reshape(bp * t, -1),
                           gates, pool_out.reshape(bp * t, -1), gab, wp, g_ffn, g_fin)
    wlen = min(WINDOW, t)
    new_kv_p = nsa.reshape(1, bp, t, 4, KV_HEADS, HEAD_DIM)
    new_win_p = win.reshape(bp, t, 2, KV_HEADS, HEAD_DIM)[None, :, t - wlen:]
    new_pool_p = u3[None, :, t - (POOL_PREV - 1):]

    xs = x_sample.reshape(bs * ts, d)
    cos_s, sin_s = _rope_tables(past + jnp.arange(ts))
    cos_s, sin_s = jnp.tile(cos_s, (bs, 1)), jnp.tile(sin_s, (bs, 1))
    qq_s, nsa_s, win_s, gates_s, _, hn_s = _qkv_proj(xs, g_mix, wp["w_qkv"], cos_s, sin_s)
    u_s = _proj(hn_s, wp["w_u"])
    gab_s = _proj(hn_s, wp["w_gab"], act="sigmoid", out_dtype=BF16)
    qq_s3 = qq_s.reshape(bs, ts, -1)
    cache_rows = cache_kv_nsa.reshape(-1, HEAD_DIM)
    y_s, kvs = _compress_pages(cache_rows, page_table, page_size, wp["wk"], wp["wv"])
    o_cmp_s, sel_s = _cmp_sample(qq_s3, y_s, past=past)
    o_sel_s = _sel_sample(kvs, qq_s3, sel_s, nsa_s)
    o_win_s = _win_sample(qq_s3, state_win_kv.reshape(-1, HEAD_DIM), win_s)
    u_s3 = u_s.reshape(bs, ts, -1)
    st_pool = state_pool[0]
    prev = jnp.pad(st_pool, ((0, 0), (POOL_PREV - st_pool.shape[1], 0), (0, 0)))
    pool_out_s = _pool(prev, u_s3, wp["w_pool"], wp["pool_scale"], base=st_pool.shape[1], zero_first_prev=False)
    y_sample = _token_tail(xs, o_cmp_s.reshape(bs * ts, -1), o_sel_s.reshape(bs * ts, -1),
                           o_win_s.reshape(bs * ts, -1), gates_s, pool_out_s.reshape(bs * ts, -1), gab_s, wp,
                           g_ffn, g_fin)
    new_kv_s = nsa_s.reshape(1, bs, ts, 4, KV_HEADS, HEAD_DIM)
    win_ext = jnp.concatenate([state_win_kv, win_s.reshape(1, bs, ts, 2, KV_HEADS, HEAD_DIM)], axis=2)
    new_win_s = win_ext[:, :, win_ext.shape[2] - min(WINDOW, win_ext.shape[2]):]
    pool_ext = jnp.concatenate([st_pool, u_s3], axis=1)
    new_pool_s = pool_ext[None, :, pool_ext.shape[1] - (POOL_PREV - 1):]

    return (y_prompt.reshape(bp, t, d), y_sample.reshape(bs, ts, d), new_kv_p, new_kv_s, new_win_p, new_win_s,
            new_pool_p, new_pool_s)
```
